```python
import jax, jax.numpy as jnp
from jax import lax
import numpy as np

D_MODEL = 1024
BATCH = 8
SEQ = 2048
DEPTH = 2

N_META = 16
BLOCK = 128
META_PAD = BLOCK - N_META
HEAD_DIM = 64
ROPE_THETA = 10000.0
NORM_EPS = 1e-6
NEG_INF = -1e30
SWA_HEADS = D_MODEL // (2 * HEAD_DIM)
SWA_KV_HEADS = SWA_HEADS // 4
SWA_GROUP = SWA_HEADS // SWA_KV_HEADS
SWA_WINDOW = 128
SWA_WIDTH = SWA_HEADS * HEAD_DIM
SWA_KV_WIDTH = SWA_KV_HEADS * HEAD_DIM
CONV_CHANNELS = D_MODEL // 2
CONV_WIDTH = 31
CONV_LN_EPS = 1e-5
SB_HEADS = D_MODEL // HEAD_DIM
SB_WIDTH = SB_HEADS * HEAD_DIM
AB_SPLITS = (SWA_WIDTH, SWA_KV_WIDTH, SWA_KV_WIDTH, SWA_WIDTH, 2 * CONV_CHANNELS, CONV_CHANNELS)
AB_IN = sum(AB_SPLITS)
AB_MIX = SWA_WIDTH + CONV_CHANNELS
SB_SPLITS = (SB_WIDTH, SB_WIDTH, SB_WIDTH, SB_WIDTH)
SB_IN = sum(SB_SPLITS)
N_EVEN = (DEPTH + 1) // 2
N_ODD = DEPTH // 2

kernel_name = "hybrid_swa_conformer_stickbreaking_trunk"


def _split(x, sizes):
    idx = [int(i) for i in np.cumsum(sizes)[:-1]]
    return jnp.split(x, idx, axis=-1)


def rms_norm(x, g):
    xf = x.astype(jnp.float32)
    y = xf * lax.rsqrt(jnp.mean(xf * xf, axis=-1, keepdims=True) + NORM_EPS)
    return (y * g.astype(jnp.float32)).astype(x.dtype)


def layer_norm(x, g, b):
    xf = x.astype(jnp.float32)
    mu = jnp.mean(xf, axis=-1, keepdims=True)
    xc = xf - mu
    y = xc * lax.rsqrt(jnp.mean(xc * xc, axis=-1, keepdims=True) + CONV_LN_EPS)
    return (y * g.astype(jnp.float32) + b.astype(jnp.float32)).astype(x.dtype)


def apply_rope(x, pos):
    half = x.shape[-1] // 2
    inv = ROPE_THETA ** (-jnp.arange(half, dtype=jnp.float32) / half)
    ang = pos.astype(jnp.float32)[:, None] * inv[None, :]
    cos = jnp.cos(ang)[None, :, None, :]
    sin = jnp.sin(ang)[None, :, None, :]
    xf = x.astype(jnp.float32)
    x1, x2 = xf[..., :half], xf[..., half:]
    return jnp.concatenate([x1 * cos - x2 * sin, x2 * cos + x1 * sin], axis=-1).astype(x.dtype)


def sliding_window_sink_attention(q, k, v, sinks):
    b, l = q.shape[0], q.shape[1]
    lp = l + META_PAD
    nb = lp // BLOCK
    padw = ((0, 0), (META_PAD, 0), (0, 0), (0, 0))
    qb = jnp.pad(q, padw).reshape(b, nb, BLOCK, SWA_KV_HEADS, SWA_GROUP, HEAD_DIM)
    kb = jnp.pad(k, padw).reshape(b, nb, BLOCK, SWA_KV_HEADS, HEAD_DIM)
    vb = jnp.pad(v, padw).reshape(b, nb, BLOCK, SWA_KV_HEADS, HEAD_DIM)

    def band(t):
        prev = jnp.concatenate([jnp.zeros_like(t[:, :1]), t[:, :-1]], axis=1)
        meta = jnp.broadcast_to(t[:, :1], t.shape)
        return jnp.concatenate([meta, prev, t], axis=2)

    kk, vv = band(kb), band(vb)
    scale = HEAD_DIM ** -0.5
    s = jnp.einsum('bnqgrd,bnkgd->bngrqk', qb, kk).astype(jnp.float32) * scale
    blk = jnp.arange(nb)[:, None, None]
    r = jnp.arange(BLOCK)
    qpos = blk * BLOCK + r[None, :, None]
    mpos = r[None, None, :]
    bpos = (blk - 1) * BLOCK + jnp.arange(2 * BLOCK)[None, None, :]
    meta_ok = (mpos >= META_PAD) & (qpos - mpos >= SWA_WINDOW)
    band_ok = (bpos >= META_PAD) & (qpos >= bpos) & (qpos - bpos < SWA_WINDOW)
    mask = jnp.concatenate([jnp.broadcast_to(meta_ok, (nb, BLOCK, BLOCK)), band_ok], axis=-1)
    s = jnp.where(mask[None, :, None, None], s, NEG_INF)
    sink = jnp.broadcast_to(sinks.astype(jnp.float32).reshape(1, 1, SWA_KV_HEADS, SWA_GROUP, 1, 1),
                            s.shape[:-1] + (1,))
    p = jax.nn.softmax(jnp.concatenate([s, sink], axis=-1), axis=-1)[..., :-1]
    o = jnp.einsum('bngrqk,bnkgd->bnqgrd', p.astype(v.dtype), vv)
    return o.reshape(b, lp, SWA_HEADS, HEAD_DIM)[:, META_PAD:]


def causal_depthwise_conv(u, w, bias):
    y = lax.conv_general_dilated(u, w[:, None, :].astype(u.dtype), window_strides=(1,),
                                 padding=((CONV_WIDTH - 1, 0),),
                                 dimension_numbers=('NWC', 'WIO', 'NWC'),
                                 feature_group_count=u.shape[-1])
    return y + bias


def stick_breaking_attention(q, k, v):
    b, l, h, d = q.shape
    lp = l + META_PAD
    nb = lp // BLOCK
    padw = ((0, 0), (META_PAD, 0), (0, 0), (0, 0))
    qp, kp, vp = jnp.pad(q, padw), jnp.pad(k, padw), jnp.pad(v, padw)
    scale = d ** -0.5
    outs = []
    for i in range(nb):
        kend = (i + 1) * BLOCK
        z = jnp.einsum('bqhd,bkhd->bhqk', qp[:, i * BLOCK:kend], kp[:, :kend]).astype(jnp.float32) * scale
        qpos = i * BLOCK + jnp.arange(BLOCK)[:, None]
        kpos = jnp.arange(kend)[None, :]
        valid = (kpos >= META_PAD) & (kpos < qpos)
        log_beta = jax.nn.log_sigmoid(z)
        log_1m = jnp.where(valid, jax.nn.log_sigmoid(-z), 0.0)
        after = lax.cumsum(log_1m, axis=3, reverse=True) - log_1m
        a = jnp.where(valid, jnp.exp(log_beta + after), 0.0)
        outs.append(jnp.einsum('bhqk,bkhd->bqhd', a.astype(v.dtype), vp[:, :kend]))
    return jnp.concatenate(outs, axis=1)[:, META_PAD:]


def swa_conv_mixer(h, pos, w_in, sinks, conv_w, conv_b, ln_g, ln_b, w_pw2, w_out):
    b, l, _ = h.shape
    q, k, v, g_a, glu_in, g_b = _split(h @ w_in, AB_SPLITS)
    q = apply_rope(q.reshape(b, l, SWA_HEADS, HEAD_DIM), pos)
    k = apply_rope(k.reshape(b, l, SWA_KV_HEADS, HEAD_DIM), pos)
    v = v.reshape(b, l, SWA_KV_HEADS, HEAD_DIM)
    a = sliding_window_sink_attention(q, k, v, sinks).reshape(b, l, SWA_WIDTH) * jax.nn.silu(g_a)
    u = glu_in[..., :CONV_CHANNELS] * jax.nn.sigmoid(glu_in[..., CONV_CHANNELS:])
    c = jax.nn.silu(layer_norm(causal_depthwise_conv(u, conv_w, conv_b), ln_g, ln_b))
    c = (c @ w_pw2) * jax.nn.silu(g_b)
    return jnp.concatenate([a, c], axis=-1) @ w_out


def stick_breaking_mixer(h, w_in, w_out):
    b, l, _ = h.shape
    q, k, v, g = _split(h @ w_in, SB_SPLITS)
    shp = (b, l, SB_HEADS, HEAD_DIM)
    o = stick_breaking_attention(q.reshape(shp), k.reshape(shp), v.reshape(shp))
    return (o.reshape(b, l, SB_WIDTH) * jax.nn.silu(g)) @ w_out


def setup_inputs(seed: int = 0) -> dict:
    key = jax.random.key(seed)
    ks = jax.random.split(key, 16)
    f32 = jnp.float32
    nrm = lambda k, s: jax.random.normal(k, s, dtype=f32)
    return {
        "x": nrm(ks[0], (BATCH, SEQ, D_MODEL)),
        "meta_tokens": nrm(ks[1], (N_META, D_MODEL)),
        "ab_pre_norm": 1.0 + 0.05 * nrm(ks[2], (N_EVEN, D_MODEL)),
        "ab_w_in": nrm(ks[3], (N_EVEN, D_MODEL, AB_IN)) * D_MODEL ** -0.5,
        "ab_sinks": nrm(ks[4], (N_EVEN, SWA_HEADS)),
        "ab_conv_w": nrm(ks[5], (N_EVEN, CONV_WIDTH, CONV_CHANNELS)) * CONV_WIDTH ** -0.5,
        "ab_conv_b": 0.02 * nrm(ks[6], (N_EVEN, CONV_CHANNELS)),
        "ab_conv_ln_g": 1.0 + 0.05 * nrm(ks[7], (N_EVEN, CONV_CHANNELS)),
        "ab_conv_ln_b": 0.02 * nrm(ks[8], (N_EVEN, CONV_CHANNELS)),
        "ab_w_pw2": nrm(ks[9], (N_EVEN, CONV_CHANNELS, CONV_CHANNELS)) * CONV_CHANNELS ** -0.5,
        "ab_w_out": nrm(ks[10], (N_EVEN, AB_MIX, D_MODEL)) * AB_MIX ** -0.5,
        "ab_post_norm": 1.0 + 0.05 * nrm(ks[11], (N_EVEN, D_MODEL)),
        "sb_pre_norm": 1.0 + 0.05 * nrm(ks[12], (N_ODD, D_MODEL)),
        "sb_w_in": nrm(ks[13], (N_ODD, D_MODEL, SB_IN)) * D_MODEL ** -0.5,
        "sb_w_out": nrm(ks[14], (N_ODD, SB_WIDTH, D_MODEL)) * SB_WIDTH ** -0.5,
        "sb_post_norm": 1.0 + 0.05 * nrm(ks[15], (N_ODD, D_MODEL)),
    }


def reference(x, meta_tokens, ab_pre_norm, ab_w_in, ab_sinks, ab_conv_w, ab_conv_b, ab_conv_ln_g,
              ab_conv_ln_b, ab_w_pw2, ab_w_out, ab_post_norm, sb_pre_norm, sb_w_in, sb_w_out,
              sb_post_norm):
    b = x.shape[0]
    meta = jnp.broadcast_to(meta_tokens[None].astype(x.dtype), (b, N_META, D_MODEL))
    h = jnp.concatenate([meta, x], axis=1)
    pos = jnp.arange(h.shape[1])
    for layer in range(DEPTH):
        i = layer // 2
        if layer % 2 == 0:
            y = swa_conv_mixer(rms_norm(h, ab_pre_norm[i]), pos, ab_w_in[i], ab_sinks[i],
                               ab_conv_w[i], ab_conv_b[i], ab_conv_ln_g[i], ab_conv_ln_b[i],
                               ab_w_pw2[i], ab_w_out[i])
            h = h + rms_norm(y, ab_post_norm[i])
        else:
            y = stick_breaking_mixer(rms_norm(h, sb_pre_norm[i]), sb_w_in[i], sb_w_out[i])
            h = h + rms_norm(y, sb_post_norm[i])
    return h[:, N_META:]
```

```python
import functools

import jax
import jax.numpy as jnp
from jax import lax
from jax.experimental import pallas as pl
from jax.experimental.pallas import tpu as pltpu

D_MODEL = 1024
N_META = 16
BLOCK = 128
META_PAD = BLOCK - N_META
HEAD_DIM = 64
ROPE_THETA = 10000.0
NORM_EPS = 1e-6
NEG_INF = -1e30
SWA_HEADS = 8
SWA_KV_HEADS = 2
SWA_GROUP = SWA_HEADS // SWA_KV_HEADS
SWA_WINDOW = 128
SWA_WIDTH = SWA_HEADS * HEAD_DIM
SWA_KV_WIDTH = SWA_KV_HEADS * HEAD_DIM
CONV_CHANNELS = 512
CONV_WIDTH = 31
CONV_LN_EPS = 1e-5
SB_HEADS = 16
SB_WIDTH = SB_HEADS * HEAD_DIM
AB_IN = 2 * SWA_WIDTH + 2 * SWA_KV_WIDTH + 3 * CONV_CHANNELS
SB_IN = 4 * SB_WIDTH

LANES = 128
ROW_TILE = 512
VMEM_LIMIT = 48 * 1024 * 1024

F32 = jnp.float32
BF16 = jnp.bfloat16


def _rms_normed(x, gain):
    return x * lax.rsqrt(jnp.mean(x * x, axis=-1, keepdims=True) + NORM_EPS) * gain


def _silu(x):
    return x * jax.nn.sigmoid(x)


def _params(*semantics):
    return pltpu.CompilerParams(dimension_semantics=semantics, vmem_limit_bytes=VMEM_LIMIT)


def _ab_in_kernel(h_ref, gain_ref, w_ref, cos_ref, sin_ref, qkv_ref, rest_ref):
    hn = _rms_normed(h_ref[...], gain_ref[...]).astype(BF16)

    def proj(lo, hi):
        return jnp.dot(hn, w_ref[:, lo:hi], preferred_element_type=F32)

    cos = cos_ref[...]
    sin = sin_ref[...]
    lane = lax.broadcasted_iota(jnp.int32, cos.shape, 1)
    first_half = (lane % HEAD_DIM) < (HEAD_DIM // 2)

    def rope(t):
        rot = jnp.where(first_half, pltpu.roll(t, LANES - HEAD_DIM // 2, 1), pltpu.roll(t, HEAD_DIM // 2, 1))
        return t * cos + rot * sin

    n_rope = (SWA_WIDTH + SWA_KV_WIDTH) // LANES
    for c in range(n_rope):
        lo = c * LANES
        qkv_ref[:, lo:lo + LANES] = rope(proj(lo, lo + LANES)).astype(BF16)
    v_lo = SWA_WIDTH + SWA_KV_WIDTH
    qkv_ref[:, v_lo:v_lo + SWA_KV_WIDTH] = proj(v_lo, v_lo + SWA_KV_WIDTH).astype(BF16)

    ga_lo = v_lo + SWA_KV_WIDTH
    rest_ref[:, 0:SWA_WIDTH] = _silu(proj(ga_lo, ga_lo + SWA_WIDTH))
    glu_lo = ga_lo + SWA_WIDTH
    glu_a = proj(glu_lo, glu_lo + CONV_CHANNELS)
    glu_b = proj(glu_lo + CONV_CHANNELS, glu_lo + 2 * CONV_CHANNELS)
    rest_ref[:, SWA_WIDTH:SWA_WIDTH + CONV_CHANNELS] = glu_a * jax.nn.sigmoid(glu_b)
    gb_lo = glu_lo + 2 * CONV_CHANNELS
    rest_ref[:, SWA_WIDTH + CONV_CHANNELS:] = _silu(proj(gb_lo, gb_lo + CONV_CHANNELS))


def _ab_in(h_flat, gain, w_bf, cos_t, sin_t):
    rows = h_flat.shape[0]
    qkv_w = SWA_WIDTH + 2 * SWA_KV_WIDTH
    rest_w = SWA_WIDTH + 2 * CONV_CHANNELS
    return pl.pallas_call(
        _ab_in_kernel,
        grid=(rows // ROW_TILE,),
        in_specs=[
            pl.BlockSpec((ROW_TILE, D_MODEL), lambda r: (r, 0)),
            pl.BlockSpec((1, D_MODEL), lambda r: (0, 0)),
            pl.BlockSpec((D_MODEL, AB_IN), lambda r: (0, 0)),
            pl.BlockSpec((ROW_TILE, LANES), lambda r: (r, 0)),
            pl.BlockSpec((ROW_TILE, LANES), lambda r: (r, 0)),
        ],
        out_specs=[
            pl.BlockSpec((ROW_TILE, qkv_w), lambda r: (r, 0)),
            pl.BlockSpec((ROW_TILE, rest_w), lambda r: (r, 0)),
        ],
        out_shape=[
            jax.ShapeDtypeStruct((rows, qkv_w), BF16),
            jax.ShapeDtypeStruct((rows, rest_w), F32),
        ],
        compiler_params=_params("parallel"),
        name="ab_in_proj",
    )(h_flat, gain, w_bf, cos_t, sin_t)


def _ab_mix_kernel(sinks_ref, qkv_ref, kv_prev_ref, kv_meta_ref, rest_ref, u_prev_ref, h_ref,
                   conv_w_ref, conv_b_ref, ln_g_ref, ln_b_ref, w_pw2_ref, w_out_ref, post_ref,
                   out_ref, ext_ref):
    i = pl.program_id(1)
    qkv = qkv_ref[0]
    kv_prev = kv_prev_ref[0]
    kv_meta = kv_meta_ref[0]

    k_cat = jnp.concatenate([kv_meta[:, :LANES], kv_prev[:, :LANES], qkv[:, SWA_WIDTH:SWA_WIDTH + LANES]], axis=0)
    v_cat = jnp.concatenate([kv_meta[:, LANES:], kv_prev[:, LANES:], qkv[:, SWA_WIDTH + LANES:]], axis=0)
    k_swap = jnp.concatenate([k_cat[:, HEAD_DIM:], k_cat[:, :HEAD_DIM]], axis=1)
    v_swap = jnp.concatenate([v_cat[:, HEAD_DIM:], v_cat[:, :HEAD_DIM]], axis=1)

    row = lax.broadcasted_iota(jnp.int32, (BLOCK, BLOCK), 0)
    col = lax.broadcasted_iota(jnp.int32, (BLOCK, BLOCK), 1)
    qpos = i * BLOCK + row
    meta_ok = (col >= META_PAD) & (qpos - col >= SWA_WINDOW)
    ppos = (i - 1) * BLOCK + col
    prev_ok = (ppos >= META_PAD) & (qpos - ppos < SWA_WINDOW)
    cpos = i * BLOCK + col
    cur_ok = (cpos >= META_PAD) & (qpos >= cpos)
    mask = jnp.concatenate([meta_ok, prev_ok, cur_ok], axis=1)

    lane = lax.broadcasted_iota(jnp.int32, (BLOCK, LANES), 1)
    low_half = lane < HEAD_DIM
    scale = HEAD_DIM ** -0.5
    zero_bf = jnp.zeros((), BF16)

    attn_chunks = []
    for pair in range(SWA_HEADS // 2):
        q_pair = qkv[:, pair * LANES:(pair + 1) * LANES]
        outs = []
        for parity in range(2):
            head = 2 * pair + parity
            group = head // SWA_GROUP
            q_h = jnp.where(low_half if parity == 0 else ~low_half, q_pair, zero_bf)
            k_use = k_cat if group == parity else k_swap
            v_use = v_cat if group == parity else v_swap
            s = lax.dot_general(q_h, k_use, (((1,), (1,)), ((), ())), preferred_element_type=F32) * scale
            s = jnp.where(mask, s, NEG_INF)
            sink = sinks_ref[head]
            m = jnp.maximum(jnp.max(s, axis=-1, keepdims=True), sink)
            p = jnp.exp(s - m)
            denom = jnp.sum(p, axis=-1, keepdims=True) + jnp.exp(sink - m)
            o = jnp.dot(p.astype(BF16), v_use, preferred_element_type=F32)
            outs.append(o / denom)
        attn_chunks.append(jnp.where(low_half, outs[0], outs[1]))
    attn = jnp.concatenate(attn_chunks, axis=1)

    rest = rest_ref[0]
    a_branch = (attn * rest[:, :SWA_WIDTH]).astype(BF16)

    u_prev = jnp.where(i > 0, u_prev_ref[0], 0.0)
    ext_ref[0:BLOCK, :] = u_prev
    ext_ref[BLOCK:2 * BLOCK, :] = rest[:, SWA_WIDTH:SWA_WIDTH + CONV_CHANNELS]
    conv = jnp.zeros((BLOCK, CONV_CHANNELS), F32) + conv_b_ref[...]
    for w in range(CONV_WIDTH):
        start = BLOCK - (CONV_WIDTH - 1) + w
        conv = conv + ext_ref[start:start + BLOCK, :] * conv_w_ref[w:w + 1, :]

    mu = jnp.mean(conv, axis=-1, keepdims=True)
    xc = conv - mu
    ln = xc * lax.rsqrt(jnp.mean(xc * xc, axis=-1, keepdims=True) + CONV_LN_EPS) * ln_g_ref[...] + ln_b_ref[...]
    c_act = _silu(ln).astype(BF16)
    c_branch = (jnp.dot(c_act, w_pw2_ref[...], preferred_element_type=F32) * rest[:, SWA_WIDTH + CONV_CHANNELS:]).astype(BF16)

    y = (jnp.dot(a_branch, w_out_ref[0:SWA_WIDTH, :], preferred_element_type=F32)
         + jnp.dot(c_branch, w_out_ref[SWA_WIDTH:, :], preferred_element_type=F32))
    out_ref[0] = h_ref[0] + _rms_normed(y, post_ref[...])


def _ab_mix(sinks, qkv, rest, h, conv_w, conv_b, ln_g, ln_b, w_pw2, w_out, post):
    batch, lp, _ = h.shape
    nb = lp // BLOCK
    qkv_w = qkv.shape[-1]
    rest_w = rest.shape[-1]
    kv_blk = (SWA_WIDTH // (2 * SWA_KV_WIDTH))
    prev = lambda b, i: (b, jnp.maximum(i - 1, 0), kv_blk)
    const2 = lambda b, i: (0, 0)
    return pl.pallas_call(
        _ab_mix_kernel,
        grid=(batch, nb),
        in_specs=[
            pl.BlockSpec(memory_space=pltpu.SMEM),
            pl.BlockSpec((1, BLOCK, qkv_w), lambda b, i: (b, i, 0)),
            pl.BlockSpec((1, BLOCK, 2 * SWA_KV_WIDTH), prev),
            pl.BlockSpec((1, BLOCK, 2 * SWA_KV_WIDTH), lambda b, i: (b, 0, kv_blk)),
            pl.BlockSpec((1, BLOCK, rest_w), lambda b, i: (b, i, 0)),
            pl.BlockSpec((1, BLOCK, CONV_CHANNELS), lambda b, i: (b, jnp.maximum(i - 1, 0), 1)),
            pl.BlockSpec((1, BLOCK, D_MODEL), lambda b, i: (b, i, 0)),
            pl.BlockSpec((CONV_WIDTH + 1, CONV_CHANNELS), const2),
            pl.BlockSpec((1, CONV_CHANNELS), const2),
            pl.BlockSpec((1, CONV_CHANNELS), const2),
            pl.BlockSpec((1, CONV_CHANNELS), const2),
            pl.BlockSpec((CONV_CHANNELS, CONV_CHANNELS), const2),
            pl.BlockSpec((D_MODEL, D_MODEL), const2),
            pl.BlockSpec((1, D_MODEL), const2),
        ],
        out_specs=pl.BlockSpec((1, BLOCK, D_MODEL), lambda b, i: (b, i, 0)),
        out_shape=jax.ShapeDtypeStruct(h.shape, F32),
        scratch_shapes=[pltpu.VMEM((2 * BLOCK, CONV_CHANNELS), F32)],
        compiler_params=_params("parallel", "parallel"),
        name="ab_mix",
    )(sinks, qkv, qkv, qkv, rest, rest, h, conv_w, conv_b, ln_g, ln_b, w_pw2, w_out, post)


def _sb_in_kernel(h_ref, gain_ref, w_ref, qkv_ref, gate_ref):
    hn = _rms_normed(h_ref[...], gain_ref[...]).astype(BF16)
    chunk = 4 * LANES
    for c in range(3 * SB_WIDTH // chunk):
        lo = c * chunk
        qkv_ref[:, lo:lo + chunk] = jnp.dot(hn, w_ref[:, lo:lo + chunk], preferred_element_type=F32).astype(BF16)
    for c in range(SB_WIDTH // chunk):
        lo = c * chunk
        g = jnp.dot(hn, w_ref[:, 3 * SB_WIDTH + lo:3 * SB_WIDTH + lo + chunk], preferred_element_type=F32)
        gate_ref[:, lo:lo + chunk] = _silu(g)


def _sb_in(h_flat, gain, w_bf):
    rows = h_flat.shape[0]
    return pl.pallas_call(
        _sb_in_kernel,
        grid=(rows // ROW_TILE,),
        in_specs=[
            pl.BlockSpec((ROW_TILE, D_MODEL), lambda r: (r, 0)),
            pl.BlockSpec((1, D_MODEL), lambda r: (0, 0)),
            pl.BlockSpec((D_MODEL, SB_IN), lambda r: (0, 0)),
        ],
        out_specs=[
            pl.BlockSpec((ROW_TILE, 3 * SB_WIDTH), lambda r: (r, 0)),
            pl.BlockSpec((ROW_TILE, SB_WIDTH), lambda r: (r, 0)),
        ],
        out_shape=[
            jax.ShapeDtypeStruct((rows, 3 * SB_WIDTH), BF16),
            jax.ShapeDtypeStruct((rows, SB_WIDTH), F32),
        ],
        compiler_params=_params("parallel"),
        name="sb_in_proj",
    )(h_flat, gain, w_bf)


def _sb_attn_kernel(q_ref, k_ref, v_ref, gate_ref, tri_ref, out_ref):
    i = pl.program_id(2) + 1
    q_pair = q_ref[0]
    lane = lax.broadcasted_iota(jnp.int32, (BLOCK, LANES), 1)
    row = lax.broadcasted_iota(jnp.int32, (BLOCK, LANES), 0)
    low_half = lane < HEAD_DIM
    qpos = i * BLOCK + row
    q_scaled = q_pair * jnp.asarray(HEAD_DIM ** -0.5, BF16)
    zero_bf = jnp.zeros((), BF16)
    tri = tri_ref[...]

    def one_head(q_h):
        def body(t, state):
            after_blocks, acc = state
            j = i - t
            off = pl.multiple_of(j * BLOCK, BLOCK)
            k_j = k_ref[0, pl.ds(off, BLOCK), :]
            v_j = v_ref[0, pl.ds(off, BLOCK), :]
            kpos = j * BLOCK + lane
            valid = (kpos >= META_PAD) & (kpos < qpos)
            z = lax.dot_general(q_h, k_j, (((1,), (1,)), ((), ())), preferred_element_type=F32)
            soft = jnp.log(1.0 + jnp.exp(-jnp.abs(z)))
            log_beta = jnp.minimum(z, 0.0) - soft
            log_1m = jnp.where(valid, jnp.minimum(-z, 0.0) - soft, 0.0)
            hi = log_1m.astype(BF16)
            lo = (log_1m - hi.astype(F32)).astype(BF16)
            sums = jnp.dot(jnp.concatenate([hi, lo], axis=0), tri, preferred_element_type=F32)
            sums = sums[:BLOCK] + sums[BLOCK:]
            after = after_blocks + sums[:, :LANES]
            a = jnp.where(valid, jnp.exp(log_beta + after), 0.0).astype(BF16)
            acc = acc + jnp.dot(a, v_j, preferred_element_type=F32)
            return after_blocks + sums[:, LANES:], acc

        zeros = jnp.zeros((BLOCK, LANES), F32)
        _, acc = lax.fori_loop(0, i + 1, body, (zeros, zeros))
        return acc

    o_lo = one_head(jnp.where(low_half, q_scaled, zero_bf))
    o_hi = one_head(jnp.where(low_half, zero_bf, q_scaled))
    out_ref[0] = (jnp.where(low_half, o_lo, o_hi) * gate_ref[0]).astype(BF16)


def _sb_attn(qkv, gate, tri, batch, lp):
    n_q = lp // BLOCK - 1
    pairs = SB_WIDTH // LANES
    return pl.pallas_call(
        _sb_attn_kernel,
        grid=(batch, pairs, n_q),
        in_specs=[
            pl.BlockSpec((1, BLOCK, LANES), lambda b, p, i: (b, i + 1, p)),
            pl.BlockSpec((1, lp, LANES), lambda b, p, i: (b, 0, pairs + p)),
            pl.BlockSpec((1, lp, LANES), lambda b, p, i: (b, 0, 2 * pairs + p)),
            pl.BlockSpec((1, BLOCK, LANES), lambda b, p, i: (b, i + 1, p)),
            pl.BlockSpec((BLOCK, 2 * LANES), lambda b, p, i: (0, 0)),
        ],
        out_specs=pl.BlockSpec((1, BLOCK, LANES), lambda b, p, i: (b, i, p)),
        out_shape=jax.ShapeDtypeStruct((batch, n_q * BLOCK, SB_WIDTH), BF16),
        compiler_params=_params("parallel", "parallel", "parallel"),
        name="sb_attn",
    )(qkv, qkv, qkv, gate, tri)


def _sb_out_kernel(o_ref, h_ref, w_ref, post_ref, out_ref):
    y = jnp.dot(o_ref[0], w_ref[...], preferred_element_type=F32)
    out_ref[0] = h_ref[0] + _rms_normed(y, post_ref[...])


def _sb_out(o_gated, h, w_bf, post):
    batch, rows, _ = o_gated.shape
    return pl.pallas_call(
        _sb_out_kernel,
        grid=(batch, rows // BLOCK),
        in_specs=[
            pl.BlockSpec((1, BLOCK, SB_WIDTH), lambda b, i: (b, i, 0)),
            pl.BlockSpec((1, BLOCK, D_MODEL), lambda b, i: (b, i + 1, 0)),
            pl.BlockSpec((SB_WIDTH, D_MODEL), lambda b, i: (0, 0)),
            pl.BlockSpec((1, D_MODEL), lambda b, i: (0, 0)),
        ],
        out_specs=pl.BlockSpec((1, BLOCK, D_MODEL), lambda b, i: (b, i, 0)),
        out_shape=jax.ShapeDtypeStruct((batch, rows, D_MODEL), F32),
        compiler_params=_params("parallel", "parallel"),
        name="sb_out_proj",
    )(o_gated, h, w_bf, post)


def _rope_tables(lp, batch):
    half = HEAD_DIM // 2
    pos = jnp.maximum(jnp.arange(lp) - META_PAD, 0)
    inv = ROPE_THETA ** (-jnp.arange(half, dtype=jnp.float32) / half)
    ang = pos.astype(jnp.float32)[:, None] * inv[None, :]
    cos = jnp.cos(ang)
    sin = jnp.sin(ang)
    cos_t = jnp.tile(jnp.concatenate([cos, cos], axis=-1), (batch, LANES // HEAD_DIM))
    sin_t = jnp.tile(jnp.concatenate([-sin, sin], axis=-1), (batch, LANES // HEAD_DIM))
    return cos_t, sin_t


def kernel(x, meta_tokens, ab_pre_norm, ab_w_in, ab_sinks, ab_conv_w, ab_conv_b, ab_conv_ln_g, ab_conv_ln_b, ab_w_pw2, ab_w_out, ab_post_norm, sb_pre_norm, sb_w_in, sb_w_out, sb_post_norm):
    batch, seq, _ = x.shape
    lp = META_PAD + N_META + seq
    assert lp % BLOCK == 0 and (batch * lp) % ROW_TILE == 0

    meta = jnp.broadcast_to(meta_tokens[None].astype(x.dtype), (batch, N_META, D_MODEL))
    h0 = jnp.concatenate([jnp.zeros((batch, META_PAD, D_MODEL), x.dtype), meta, x], axis=1)
    cos_t, sin_t = _rope_tables(lp, batch)

    qkv, rest = _ab_in(h0.reshape(batch * lp, D_MODEL), ab_pre_norm[0][None], ab_w_in[0].astype(BF16), cos_t, sin_t)
    conv_w = jnp.concatenate([ab_conv_w[0], jnp.zeros((1, CONV_CHANNELS), F32)], axis=0)
    h1 = _ab_mix(ab_sinks[0], qkv.reshape(batch, lp, -1), rest.reshape(batch, lp, -1), h0,
                 conv_w, ab_conv_b[0][None], ab_conv_ln_g[0][None], ab_conv_ln_b[0][None],
                 ab_w_pw2[0].astype(BF16), ab_w_out[0].astype(BF16), ab_post_norm[0][None])

    qkv1, gate1 = _sb_in(h1.reshape(batch * lp, D_MODEL), sb_pre_norm[0][None], sb_w_in[0].astype(BF16))
    r = jnp.arange(BLOCK)
    tri = jnp.concatenate([(r[:, None] > r[None, :]).astype(BF16), jnp.ones((BLOCK, LANES), BF16)], axis=1)
    o_gated = _sb_attn(qkv1.reshape(batch, lp, -1), gate1.reshape(batch, lp, -1), tri, batch, lp)
    return _sb_out(o_gated, h1, sb_w_out[0].astype(BF16), sb_post_norm[0][None])
```

```python
import functools

import jax
import jax.numpy as jnp
from jax import lax
from jax.experimental import pallas as pl
from jax.experimental.pallas import tpu as pltpu

D_MODEL = 1024
N_META = 16
BLOCK = 128
META_PAD = BLOCK - N_META
HEAD_DIM = 64
ROPE_THETA = 10000.0
NORM_EPS = 1e-6
NEG_INF = -1e30
SWA_HEADS = 8
SWA_KV_HEADS = 2
SWA_GROUP = SWA_HEADS // SWA_KV_HEADS
SWA_WINDOW = 128
SWA_WIDTH = SWA_HEADS * HEAD_DIM
SWA_KV_WIDTH = SWA_KV_HEADS * HEAD_DIM
CONV_CHANNELS = 512
CONV_WIDTH = 31
CONV_LN_EPS = 1e-5
SB_HEADS = 16
SB_WIDTH = SB_HEADS * HEAD_DIM
AB_IN = 2 * SWA_WIDTH + 2 * SWA_KV_WIDTH + 3 * CONV_CHANNELS
SB_IN = 4 * SB_WIDTH

LANES = 128
ROW_TILE = 512
VMEM_LIMIT = 48 * 1024 * 1024

F32 = jnp.float32
BF16 = jnp.bfloat16


def _rms_normed(x, gain):
    return x * lax.rsqrt(jnp.mean(x * x, axis=-1, keepdims=True) + NORM_EPS) * gain


def _silu(x):
    return x * jax.nn.sigmoid(x)


def _params(*semantics):
    return pltpu.CompilerParams(dimension_semantics=semantics, vmem_limit_bytes=VMEM_LIMIT)


def _ab_in_kernel(h_ref, gain_ref, w_ref, cos_ref, sin_ref, qkv_ref, rest_ref):
    hn = _rms_normed(h_ref[...], gain_ref[...]).astype(BF16)

    def proj(lo, hi):
        return jnp.dot(hn, w_ref[:, lo:hi], preferred_element_type=F32)

    cos = cos_ref[...]
    sin = sin_ref[...]
    lane = lax.broadcasted_iota(jnp.int32, cos.shape, 1)
    first_half = (lane % HEAD_DIM) < (HEAD_DIM // 2)

    def rope(t):
        rot = jnp.where(first_half, pltpu.roll(t, LANES - HEAD_DIM // 2, 1), pltpu.roll(t, HEAD_DIM // 2, 1))
        return t * cos + rot * sin

    n_rope = (SWA_WIDTH + SWA_KV_WIDTH) // LANES
    for c in range(n_rope):
        lo = c * LANES
        qkv_ref[:, lo:lo + LANES] = rope(proj(lo, lo + LANES)).astype(BF16)
    v_lo = SWA_WIDTH + SWA_KV_WIDTH
    qkv_ref[:, v_lo:v_lo + SWA_KV_WIDTH] = proj(v_lo, v_lo + SWA_KV_WIDTH).astype(BF16)

    ga_lo = v_lo + SWA_KV_WIDTH
    rest_ref[:, 0:SWA_WIDTH] = _silu(proj(ga_lo, ga_lo + SWA_WIDTH))
    glu_lo = ga_lo + SWA_WIDTH
    glu_a = proj(glu_lo, glu_lo + CONV_CHANNELS)
    glu_b = proj(glu_lo + CONV_CHANNELS, glu_lo + 2 * CONV_CHANNELS)
    rest_ref[:, SWA_WIDTH:SWA_WIDTH + CONV_CHANNELS] = glu_a * jax.nn.sigmoid(glu_b)
    gb_lo = glu_lo + 2 * CONV_CHANNELS
    rest_ref[:, SWA_WIDTH + CONV_CHANNELS:] = _silu(proj(gb_lo, gb_lo + CONV_CHANNELS))


def _ab_in(h_flat, gain, w_bf, cos_t, sin_t):
    rows = h_flat.shape[0]
    qkv_w = SWA_WIDTH + 2 * SWA_KV_WIDTH
    rest_w = SWA_WIDTH + 2 * CONV_CHANNELS
    return pl.pallas_call(
        _ab_in_kernel,
        grid=(rows // ROW_TILE,),
        in_specs=[
            pl.BlockSpec((ROW_TILE, D_MODEL), lambda r: (r, 0)),
            pl.BlockSpec((1, D_MODEL), lambda r: (0, 0)),
            pl.BlockSpec((D_MODEL, AB_IN), lambda r: (0, 0)),
            pl.BlockSpec((ROW_TILE, LANES), lambda r: (r, 0)),
            pl.BlockSpec((ROW_TILE, LANES), lambda r: (r, 0)),
        ],
        out_specs=[
            pl.BlockSpec((ROW_TILE, qkv_w), lambda r: (r, 0)),
            pl.BlockSpec((ROW_TILE, rest_w), lambda r: (r, 0)),
        ],
        out_shape=[
            jax.ShapeDtypeStruct((rows, qkv_w), BF16),
            jax.ShapeDtypeStruct((rows, rest_w), F32),
        ],
        compiler_params=_params("parallel"),
        name="ab_in_proj",
    )(h_flat, gain, w_bf, cos_t, sin_t)


def _ab_mix_kernel(sinks_ref, qkv_ref, kv_prev_ref, kv_meta_ref, rest_ref, u_prev_ref, h_ref,
                   conv_w_ref, conv_b_ref, ln_g_ref, ln_b_ref, w_pw2_ref, w_out_ref, post_ref,
                   out_ref, ext_ref):
    i = pl.program_id(1)
    qkv = qkv_ref[0]
    kv_prev = kv_prev_ref[0]
    kv_meta = kv_meta_ref[0]

    k_cat = jnp.concatenate([kv_meta[:, :LANES], kv_prev[:, :LANES], qkv[:, SWA_WIDTH:SWA_WIDTH + LANES]], axis=0)
    v_cat = jnp.concatenate([kv_meta[:, LANES:], kv_prev[:, LANES:], qkv[:, SWA_WIDTH + LANES:]], axis=0)
    k_swap = jnp.concatenate([k_cat[:, HEAD_DIM:], k_cat[:, :HEAD_DIM]], axis=1)
    v_swap = jnp.concatenate([v_cat[:, HEAD_DIM:], v_cat[:, :HEAD_DIM]], axis=1)

    row = lax.broadcasted_iota(jnp.int32, (BLOCK, BLOCK), 0)
    col = lax.broadcasted_iota(jnp.int32, (BLOCK, BLOCK), 1)
    qpos = i * BLOCK + row
    meta_ok = (col >= META_PAD) & (qpos - col >= SWA_WINDOW)
    ppos = (i - 1) * BLOCK + col
    prev_ok = (ppos >= META_PAD) & (qpos - ppos < SWA_WINDOW)
    cpos = i * BLOCK + col
    cur_ok = (cpos >= META_PAD) & (qpos >= cpos)
    mask = jnp.concatenate([meta_ok, prev_ok, cur_ok], axis=1)

    lane = lax.broadcasted_iota(jnp.int32, (BLOCK, LANES), 1)
    low_half = lane < HEAD_DIM
    scale = HEAD_DIM ** -0.5
    zero_bf = jnp.zeros((), BF16)

    attn_chunks = []
    for pair in range(SWA_HEADS // 2):
        q_pair = qkv[:, pair * LANES:(pair + 1) * LANES]
        outs = []
        for parity in range(2):
            head = 2 * pair + parity
            group = head // SWA_GROUP
            q_h = jnp.where(low_half if parity == 0 else ~low_half, q_pair, zero_bf)
            k_use = k_cat if group == parity else k_swap
            v_use = v_cat if group == parity else v_swap
            s = lax.dot_general(q_h, k_use, (((1,), (1,)), ((), ())), preferred_element_type=F32) * scale
            s = jnp.where(mask, s, NEG_INF)
            sink = sinks_ref[head]
            m = jnp.maximum(jnp.max(s, axis=-1, keepdims=True), sink)
            p = jnp.exp(s - m)
            denom = jnp.sum(p, axis=-1, keepdims=True) + jnp.exp(sink - m)
            o = jnp.dot(p.astype(BF16), v_use, preferred_element_type=F32)
            outs.append(o / denom)
        attn_chunks.append(jnp.where(low_half, outs[0], outs[1]))
    attn = jnp.concatenate(attn_chunks, axis=1)

    rest = rest_ref[0]
    a_branch = (attn * rest[:, :SWA_WIDTH]).astype(BF16)

    u_prev = jnp.where(i > 0, u_prev_ref[0], 0.0)
    ext_ref[0:BLOCK, :] = u_prev
    ext_ref[BLOCK:2 * BLOCK, :] = rest[:, SWA_WIDTH:SWA_WIDTH + CONV_CHANNELS]
    conv = jnp.zeros((BLOCK, CONV_CHANNELS), F32) + conv_b_ref[...]
    for w in range(CONV_WIDTH):
        start = BLOCK - (CONV_WIDTH - 1) + w
        conv = conv + ext_ref[start:start + BLOCK, :] * conv_w_ref[w:w + 1, :]

    mu = jnp.mean(conv, axis=-1, keepdims=True)
    xc = conv - mu
    ln = xc * lax.rsqrt(jnp.mean(xc * xc, axis=-1, keepdims=True) + CONV_LN_EPS) * ln_g_ref[...] + ln_b_ref[...]
    c_act = _silu(ln).astype(BF16)
    c_branch = (jnp.dot(c_act, w_pw2_ref[...], preferred_element_type=F32) * rest[:, SWA_WIDTH + CONV_CHANNELS:]).astype(BF16)

    y = (jnp.dot(a_branch, w_out_ref[0:SWA_WIDTH, :], preferred_element_type=F32)
         + jnp.dot(c_branch, w_out_ref[SWA_WIDTH:, :], preferred_element_type=F32))
    out_ref[0] = h_ref[0] + _rms_normed(y, post_ref[...])


def _ab_mix(sinks, qkv, rest, h, conv_w, conv_b, ln_g, ln_b, w_pw2, w_out, post):
    batch, lp, _ = h.shape
    nb = lp // BLOCK
    qkv_w = qkv.shape[-1]
    rest_w = rest.shape[-1]
    kv_blk = (SWA_WIDTH // (2 * SWA_KV_WIDTH))
    prev = lambda b, i: (b, jnp.maximum(i - 1, 0), kv_blk)
    const2 = lambda b, i: (0, 0)
    return pl.pallas_call(
        _ab_mix_kernel,
        grid=(batch, nb),
        in_specs=[
            pl.BlockSpec(memory_space=pltpu.SMEM),
            pl.BlockSpec((1, BLOCK, qkv_w), lambda b, i: (b, i, 0)),
            pl.BlockSpec((1, BLOCK, 2 * SWA_KV_WIDTH), prev),
            pl.BlockSpec((1, BLOCK, 2 * SWA_KV_WIDTH), lambda b, i: (b, 0, kv_blk)),
            pl.BlockSpec((1, BLOCK, rest_w), lambda b, i: (b, i, 0)),
            pl.BlockSpec((1, BLOCK, CONV_CHANNELS), lambda b, i: (b, jnp.maximum(i - 1, 0), 1)),
            pl.BlockSpec((1, BLOCK, D_MODEL), lambda b, i: (b, i, 0)),
            pl.BlockSpec((CONV_WIDTH + 1, CONV_CHANNELS), const2),
            pl.BlockSpec((1, CONV_CHANNELS), const2),
            pl.BlockSpec((1, CONV_CHANNELS), const2),
            pl.BlockSpec((1, CONV_CHANNELS), const2),
            pl.BlockSpec((CONV_CHANNELS, CONV_CHANNELS), const2),
            pl.BlockSpec((D_MODEL, D_MODEL), const2),
            pl.BlockSpec((1, D_MODEL), const2),
        ],
        out_specs=pl.BlockSpec((1, BLOCK, D_MODEL), lambda b, i: (b, i, 0)),
        out_shape=jax.ShapeDtypeStruct(h.shape, F32),
        scratch_shapes=[pltpu.VMEM((2 * BLOCK, CONV_CHANNELS), F32)],
        compiler_params=_params("parallel", "parallel"),
        name="ab_mix",
    )(sinks, qkv, qkv, qkv, rest, rest, h, conv_w, conv_b, ln_g, ln_b, w_pw2, w_out, post)


def _sb_in_kernel(h_ref, gain_ref, w_ref, qkv_ref, gate_ref):
    hn = _rms_normed(h_ref[...], gain_ref[...]).astype(BF16)
    chunk = 4 * LANES
    for c in range(3 * SB_WIDTH // chunk):
        lo = c * chunk
        qkv_ref[:, lo:lo + chunk] = jnp.dot(hn, w_ref[:, lo:lo + chunk], preferred_element_type=F32).astype(BF16)
    for c in range(SB_WIDTH // chunk):
        lo = c * chunk
        g = jnp.dot(hn, w_ref[:, 3 * SB_WIDTH + lo:3 * SB_WIDTH + lo + chunk], preferred_element_type=F32)
        gate_ref[:, lo:lo + chunk] = _silu(g)


def _sb_in(h_flat, gain, w_bf):
    rows = h_flat.shape[0]
    return pl.pallas_call(
        _sb_in_kernel,
        grid=(rows // ROW_TILE,),
        in_specs=[
            pl.BlockSpec((ROW_TILE, D_MODEL), lambda r: (r, 0)),
            pl.BlockSpec((1, D_MODEL), lambda r: (0, 0)),
            pl.BlockSpec((D_MODEL, SB_IN), lambda r: (0, 0)),
        ],
        out_specs=[
            pl.BlockSpec((ROW_TILE, 3 * SB_WIDTH), lambda r: (r, 0)),
            pl.BlockSpec((ROW_TILE, SB_WIDTH), lambda r: (r, 0)),
        ],
        out_shape=[
            jax.ShapeDtypeStruct((rows, 3 * SB_WIDTH), BF16),
            jax.ShapeDtypeStruct((rows, SB_WIDTH), F32),
        ],
        compiler_params=_params("parallel"),
        name="sb_in_proj",
    )(h_flat, gain, w_bf)


SB_PAIRS = SB_WIDTH // LANES
SB_MASS_CUTOFF = 104.0


def _sb_attn_kernel(q_ref, k_ref, v_ref, gate_ref, tri_ref, out_ref,
                    carry_ref, acc_ref, z_ref, lb_ref, split_ref, sums_ref, w_ref):
    i = pl.program_id(1) + 1
    lane = lax.broadcasted_iota(jnp.int32, (BLOCK, LANES), 1)
    row = lax.broadcasted_iota(jnp.int32, (BLOCK, LANES), 0)
    low_half = lane < HEAD_DIM
    qpos = i * BLOCK + row
    zero_bf = jnp.zeros((), BF16)
    q_scale = jnp.asarray(HEAD_DIM ** -0.5, BF16)

    carry_ref[...] = jnp.zeros_like(carry_ref)
    acc_ref[...] = jnp.zeros_like(acc_ref)

    def block_diag(x):
        return jnp.concatenate([jnp.where(low_half, x, zero_bf), jnp.where(low_half, zero_bf, x)], axis=0)

    def keep_going(state):
        t, least_mass = state
        return (t <= i) & (least_mass < SB_MASS_CUTOFF)

    def body(state):
        t, _ = state
        j = i - t
        off = pl.multiple_of(j * BLOCK, BLOCK)
        kpos = j * BLOCK + lane
        valid = (kpos >= META_PAD) & (kpos < qpos)
        for p in range(SB_PAIRS):
            cols = slice(p * LANES, (p + 1) * LANES)
            q_pair = q_ref[0, :, cols] * q_scale
            k_bd = block_diag(k_ref[0, pl.ds(off, BLOCK), cols])
            z_ref[p] = lax.dot_general(q_pair, k_bd, (((1,), (1,)), ((), ())), preferred_element_type=F32)
        for n in range(2 * SB_PAIRS):
            z = z_ref[n // 2, :, (n % 2) * LANES:(n % 2 + 1) * LANES]
            sp = jnp.maximum(z, 0.0) + jnp.log(1.0 + jnp.exp(-jnp.abs(z)))
            lb_ref[n] = z - sp
            sp = jnp.where(valid, sp, 0.0)
            hi = sp.astype(BF16)
            split_ref[n, :, 0:LANES] = hi
            split_ref[n, :, LANES:] = (sp - hi.astype(F32)).astype(BF16)
        for n in range(2 * SB_PAIRS):
            sums_ref[n] = jnp.dot(split_ref[n], tri_ref[...], preferred_element_type=F32)
        least = None
        for n in range(2 * SB_PAIRS):
            later_blocks = carry_ref[n]
            later = later_blocks + sums_ref[n, :, 0:LANES]
            w_ref[n // 2, :, (n % 2) * LANES:(n % 2 + 1) * LANES] = jnp.where(
                valid, jnp.exp(lb_ref[n] - later), 0.0).astype(BF16)
            new_carry = later_blocks + sums_ref[n, :, LANES:]
            carry_ref[n] = new_carry
            least = new_carry if least is None else jnp.minimum(least, new_carry)
        for p in range(SB_PAIRS):
            cols = slice(p * LANES, (p + 1) * LANES)
            v_bd = block_diag(v_ref[0, pl.ds(off, BLOCK), cols])
            acc_ref[p] += jnp.dot(w_ref[p], v_bd, preferred_element_type=F32)
        return t + 1, jnp.min(least)

    lax.while_loop(keep_going, body, (jnp.int32(0), jnp.float32(0.0)))
    for p in range(SB_PAIRS):
        cols = slice(p * LANES, (p + 1) * LANES)
        out_ref[0, :, cols] = (acc_ref[p] * gate_ref[0, :, cols]).astype(BF16)


def _sb_attn(qkv, gate, tri, batch, lp):
    n_q = lp // BLOCK - 1
    return pl.pallas_call(
        _sb_attn_kernel,
        grid=(batch, n_q),
        in_specs=[
            pl.BlockSpec((1, BLOCK, SB_WIDTH), lambda b, i: (b, i + 1, 0)),
            pl.BlockSpec((1, lp, SB_WIDTH), lambda b, i: (b, 0, 1)),
            pl.BlockSpec((1, lp, SB_WIDTH), lambda b, i: (b, 0, 2)),
            pl.BlockSpec((1, BLOCK, SB_WIDTH), lambda b, i: (b, i + 1, 0)),
            pl.BlockSpec((2 * BLOCK, 2 * LANES), lambda b, i: (0, 0)),
        ],
        out_specs=pl.BlockSpec((1, BLOCK, SB_WIDTH), lambda b, i: (b, i, 0)),
        out_shape=jax.ShapeDtypeStruct((batch, n_q * BLOCK, SB_WIDTH), BF16),
        scratch_shapes=[
            pltpu.VMEM((2 * SB_PAIRS, BLOCK, LANES), F32),
            pltpu.VMEM((SB_PAIRS, BLOCK, LANES), F32),
            pltpu.VMEM((SB_PAIRS, BLOCK, 2 * LANES), F32),
            pltpu.VMEM((2 * SB_PAIRS, BLOCK, LANES), F32),
            pltpu.VMEM((2 * SB_PAIRS, BLOCK, 2 * LANES), BF16),
            pltpu.VMEM((2 * SB_PAIRS, BLOCK, 2 * LANES), F32),
            pltpu.VMEM((SB_PAIRS, BLOCK, 2 * LANES), BF16),
        ],
        compiler_params=_params("parallel", "arbitrary"),
        name="sb_attn",
    )(qkv, qkv, qkv, gate, tri)


def _sb_out_kernel(o_ref, h_ref, w_ref, post_ref, out_ref):
    y = jnp.dot(o_ref[0], w_ref[...], preferred_element_type=F32)
    out_ref[0] = h_ref[0] + _rms_normed(y, post_ref[...])


def _sb_out(o_gated, h, w_bf, post):
    batch, rows, _ = o_gated.shape
    return pl.pallas_call(
        _sb_out_kernel,
        grid=(batch, rows // BLOCK),
        in_specs=[
            pl.BlockSpec((1, BLOCK, SB_WIDTH), lambda b, i: (b, i, 0)),
            pl.BlockSpec((1, BLOCK, D_MODEL), lambda b, i: (b, i + 1, 0)),
            pl.BlockSpec((SB_WIDTH, D_MODEL), lambda b, i: (0, 0)),
            pl.BlockSpec((1, D_MODEL), lambda b, i: (0, 0)),
        ],
        out_specs=pl.BlockSpec((1, BLOCK, D_MODEL), lambda b, i: (b, i, 0)),
        out_shape=jax.ShapeDtypeStruct((batch, rows, D_MODEL), F32),
        compiler_params=_params("parallel", "parallel"),
        name="sb_out_proj",
    )(o_gated, h, w_bf, post)


def _rope_tables(lp, batch):
    half = HEAD_DIM // 2
    pos = jnp.maximum(jnp.arange(lp) - META_PAD, 0)
    inv = ROPE_THETA ** (-jnp.arange(half, dtype=jnp.float32) / half)
    ang = pos.astype(jnp.float32)[:, None] * inv[None, :]
    cos = jnp.cos(ang)
    sin = jnp.sin(ang)
    cos_t = jnp.tile(jnp.concatenate([cos, cos], axis=-1), (batch, LANES // HEAD_DIM))
    sin_t = jnp.tile(jnp.concatenate([-sin, sin], axis=-1), (batch, LANES // HEAD_DIM))
    return cos_t, sin_t


def kernel(x, meta_tokens, ab_pre_norm, ab_w_in, ab_sinks, ab_conv_w, ab_conv_b, ab_conv_ln_g, ab_conv_ln_b, ab_w_pw2, ab_w_out, ab_post_norm, sb_pre_norm, sb_w_in, sb_w_out, sb_post_norm):
    batch, seq, _ = x.shape
    lp = META_PAD + N_META + seq
    assert lp % BLOCK == 0 and (batch * lp) % ROW_TILE == 0

    meta = jnp.broadcast_to(meta_tokens[None].astype(x.dtype), (batch, N_META, D_MODEL))
    h0 = jnp.concatenate([jnp.zeros((batch, META_PAD, D_MODEL), x.dtype), meta, x], axis=1)
    cos_t, sin_t = _rope_tables(lp, batch)

    qkv, rest = _ab_in(h0.reshape(batch * lp, D_MODEL), ab_pre_norm[0][None], ab_w_in[0].astype(BF16), cos_t, sin_t)
    conv_w = jnp.concatenate([ab_conv_w[0], jnp.zeros((1, CONV_CHANNELS), F32)], axis=0)
    h1 = _ab_mix(ab_sinks[0], qkv.reshape(batch, lp, -1), rest.reshape(batch, lp, -1), h0,
                 conv_w, ab_conv_b[0][None], ab_conv_ln_g[0][None], ab_conv_ln_b[0][None],
                 ab_w_pw2[0].astype(BF16), ab_w_out[0].astype(BF16), ab_post_norm[0][None])

    qkv1, gate1 = _sb_in(h1.reshape(batch * lp, D_MODEL), sb_pre_norm[0][None], sb_w_in[0].astype(BF16))
    r = jnp.arange(BLOCK)
    tri = jnp.concatenate([(r[:, None] > r[None, :]).astype(BF16), jnp.ones((BLOCK, LANES), BF16)], axis=1)
    tri = jnp.concatenate([tri, tri], axis=0)
    o_gated = _sb_attn(qkv1.reshape(batch, lp, -1), gate1.reshape(batch, lp, -1), tri, batch, lp)
    return _sb_out(o_gated, h1, sb_w_out[0].astype(BF16), sb_post_norm[0][None])
```

```python
import jax
import jax.numpy as jnp
from jax import lax
from jax.experimental import pallas as pl
from jax.experimental.pallas import tpu as pltpu

D_MODEL = 1024
N_META = 16
BLOCK = 128
META_PAD = BLOCK - N_META
HEAD_DIM = 64
ROPE_THETA = 10000.0
NORM_EPS = 1e-6
NEG_INF = -1e30
SWA_HEADS = 8
SWA_KV_HEADS = 2
SWA_GROUP = SWA_HEADS // SWA_KV_HEADS
SWA_WINDOW = 128
SWA_WIDTH = SWA_HEADS * HEAD_DIM
SWA_KV_WIDTH = SWA_KV_HEADS * HEAD_DIM
CONV_CHANNELS = 512
CONV_WIDTH = 31
CONV_LN_EPS = 1e-5
SB_HEADS = 16
SB_WIDTH = SB_HEADS * HEAD_DIM
AB_IN = 2 * SWA_WIDTH + 2 * SWA_KV_WIDTH + 3 * CONV_CHANNELS
AB_MIX = SWA_WIDTH + CONV_CHANNELS
SB_IN = 4 * SB_WIDTH

LANES = 128
SUBLANES = 8
ROW_TILE = 512
VMEM_LIMIT = 48 * 1024 * 1024
CONV_HALO = -(-(CONV_WIDTH - 1) // SUBLANES) * SUBLANES
CONV_WIN_START = BLOCK - CONV_HALO
CONV_WIN_ROWS = BLOCK + CONV_HALO
CONV_LEAD = CONV_HALO - (CONV_WIDTH - 1)

SB_PAIRS = SB_WIDTH // LANES
SB_MASS_CUTOFF = 104.0

F32 = jnp.float32
BF16 = jnp.bfloat16


def _rms_normed(x, gain):
    return x * lax.rsqrt(jnp.mean(x * x, axis=-1, keepdims=True) + NORM_EPS) * gain


def _silu(x):
    return x * jax.nn.sigmoid(x)


def _params(*semantics):
    return pltpu.CompilerParams(dimension_semantics=semantics, vmem_limit_bytes=VMEM_LIMIT)


def _ab_in_kernel(h_ref, gain_ref, w_ref, cos_ref, sin_ref, qkv_ref, rest_ref):
    hn = _rms_normed(h_ref[...], gain_ref[...]).astype(BF16)

    def proj(lo, hi):
        return jnp.dot(hn, w_ref[:, lo:hi], preferred_element_type=F32)

    cos = cos_ref[...]
    sin = sin_ref[...]
    lane = lax.broadcasted_iota(jnp.int32, cos.shape, 1)
    first_half = (lane % HEAD_DIM) < (HEAD_DIM // 2)

    def rope(t):
        rot = jnp.where(first_half, pltpu.roll(t, LANES - HEAD_DIM // 2, 1), pltpu.roll(t, HEAD_DIM // 2, 1))
        return t * cos + rot * sin

    for c in range((SWA_WIDTH + SWA_KV_WIDTH) // LANES):
        lo = c * LANES
        scale = HEAD_DIM ** -0.5 if lo < SWA_WIDTH else 1.0
        qkv_ref[:, lo:lo + LANES] = (rope(proj(lo, lo + LANES)) * scale).astype(BF16)
    v_lo = SWA_WIDTH + SWA_KV_WIDTH
    qkv_ref[:, v_lo:v_lo + SWA_KV_WIDTH] = proj(v_lo, v_lo + SWA_KV_WIDTH).astype(BF16)

    ga_lo = v_lo + SWA_KV_WIDTH
    rest_ref[:, 0:SWA_WIDTH] = _silu(proj(ga_lo, ga_lo + SWA_WIDTH))
    glu_lo = ga_lo + SWA_WIDTH
    glu_a = proj(glu_lo, glu_lo + CONV_CHANNELS)
    glu_b = proj(glu_lo + CONV_CHANNELS, glu_lo + 2 * CONV_CHANNELS)
    rest_ref[:, SWA_WIDTH:SWA_WIDTH + CONV_CHANNELS] = glu_a * jax.nn.sigmoid(glu_b)
    gb_lo = glu_lo + 2 * CONV_CHANNELS
    rest_ref[:, SWA_WIDTH + CONV_CHANNELS:] = _silu(proj(gb_lo, gb_lo + CONV_CHANNELS))


def _ab_in(h_flat, gain, w_bf, cos_t, sin_t):
    rows = h_flat.shape[0]
    qkv_w = SWA_WIDTH + 2 * SWA_KV_WIDTH
    rest_w = SWA_WIDTH + 2 * CONV_CHANNELS
    return pl.pallas_call(
        _ab_in_kernel,
        grid=(rows // ROW_TILE,),
        in_specs=[
            pl.BlockSpec((ROW_TILE, D_MODEL), lambda r: (r, 0)),
            pl.BlockSpec((1, D_MODEL), lambda r: (0, 0)),
            pl.BlockSpec((D_MODEL, AB_IN), lambda r: (0, 0)),
            pl.BlockSpec((ROW_TILE, LANES), lambda r: (r, 0)),
            pl.BlockSpec((ROW_TILE, LANES), lambda r: (r, 0)),
        ],
        out_specs=[
            pl.BlockSpec((ROW_TILE, qkv_w), lambda r: (r, 0)),
            pl.BlockSpec((ROW_TILE, rest_w), lambda r: (r, 0)),
        ],
        out_shape=[
            jax.ShapeDtypeStruct((rows, qkv_w), BF16),
            jax.ShapeDtypeStruct((rows, rest_w), F32),
        ],
        compiler_params=_params("parallel"),
        name="ab_in_proj",
    )(h_flat, gain, w_bf, cos_t, sin_t)


def _ab_mix_kernel(sinks_ref, qkv_ref, kv_prev_ref, kv_meta_ref, rest_ref, u_prev_ref,
                   conv_w_ref, conv_b_ref, ln_g_ref, ln_b_ref, w_pw2_ref,
                   out_ref, ext_ref, rot_ref, s_ref, p_ref):
    n_blocks = pl.num_programs(1)
    i = pl.program_id(1)
    ref_blk = jnp.where(i == n_blocks - 1, 0, i + 1)
    qkv = qkv_ref[0]
    kv_prev = kv_prev_ref[0]
    kv_meta = kv_meta_ref[0]

    k_cat = jnp.concatenate([kv_meta[:, :LANES], kv_prev[:, :LANES], qkv[:, SWA_WIDTH:SWA_WIDTH + LANES]], axis=0)
    v_cat = jnp.concatenate([kv_meta[:, LANES:], kv_prev[:, LANES:], qkv[:, SWA_WIDTH + LANES:]], axis=0)
    k_swap = jnp.concatenate([k_cat[:, HEAD_DIM:], k_cat[:, :HEAD_DIM]], axis=1)
    v_swap = jnp.concatenate([v_cat[:, HEAD_DIM:], v_cat[:, :HEAD_DIM]], axis=1)

    row = lax.broadcasted_iota(jnp.int32, (BLOCK, BLOCK), 0)
    col = lax.broadcasted_iota(jnp.int32, (BLOCK, BLOCK), 1)
    qpos = ref_blk * BLOCK + row
    meta_ok = (col >= META_PAD) & (qpos - col >= SWA_WINDOW)
    ppos = (ref_blk - 1) * BLOCK + col
    prev_ok = (ppos >= META_PAD) & (qpos - ppos < SWA_WINDOW)
    cpos = ref_blk * BLOCK + col
    cur_ok = (cpos >= META_PAD) & (qpos >= cpos)
    mask = jnp.concatenate([meta_ok, prev_ok, cur_ok], axis=1)

    lane = lax.broadcasted_iota(jnp.int32, (BLOCK, LANES), 1)
    low_half = lane < HEAD_DIM
    zero_bf = jnp.zeros((), BF16)

    for head in range(SWA_HEADS):
        parity = head % 2
        q_pair = qkv[:, (head // 2) * LANES:(head // 2 + 1) * LANES]
        q_h = jnp.where(low_half if parity == 0 else ~low_half, q_pair, zero_bf)
        k_use = k_cat if head // SWA_GROUP == parity else k_swap
        s_ref[head] = lax.dot_general(q_h, k_use, (((1,), (1,)), ((), ())), preferred_element_type=F32)
    inv_denoms = []
    for head in range(SWA_HEADS):
        s = jnp.where(mask, s_ref[head], NEG_INF)
        sink = sinks_ref[head]
        m = jnp.maximum(jnp.max(s, axis=-1, keepdims=True), sink)
        p = jnp.exp(s - m)
        inv_denoms.append(1.0 / (jnp.sum(p, axis=-1, keepdims=True) + jnp.exp(sink - m)))
        p_ref[head] = p.astype(BF16)
    rest = rest_ref[0]
    for pair in range(SWA_HEADS // 2):
        outs = []
        for parity in range(2):
            head = 2 * pair + parity
            v_use = v_cat if head // SWA_GROUP == parity else v_swap
            outs.append(jnp.dot(p_ref[head], v_use, preferred_element_type=F32) * inv_denoms[head])
        cols = slice(pair * LANES, (pair + 1) * LANES)
        out_ref[0, :, cols] = (jnp.where(low_half, outs[0], outs[1]) * rest[:, cols]).astype(BF16)

    ext_ref[0:BLOCK, :] = jnp.where(ref_blk > 0, u_prev_ref[0], 0.0)
    ext_ref[BLOCK:2 * BLOCK, :] = rest[:, SWA_WIDTH:SWA_WIDTH + CONV_CHANNELS]
    window = ext_ref[CONV_WIN_START:, :]
    for b in range(1, SUBLANES):
        rot_ref[b - 1] = pltpu.roll(window, CONV_WIN_ROWS - b, 0)
    conv_groups = []
    for cg in range(CONV_CHANNELS // LANES):
        cols = slice(cg * LANES, (cg + 1) * LANES)
        acc = jnp.zeros((BLOCK, LANES), F32) + conv_b_ref[:, cols]
        for w in range(CONV_WIDTH):
            a, b = divmod(CONV_LEAD + w, SUBLANES)
            if b == 0:
                taps = ext_ref[CONV_WIN_START + SUBLANES * a:CONV_WIN_START + SUBLANES * a + BLOCK, cols]
            else:
                taps = rot_ref[b - 1, SUBLANES * a:SUBLANES * a + BLOCK, cols]
            acc = acc + taps * conv_w_ref[w:w + 1, cols]
        conv_groups.append(acc)
    conv = jnp.concatenate(conv_groups, axis=1)

    mu = jnp.mean(conv, axis=-1, keepdims=True)
    xc = conv - mu
    ln = xc * lax.rsqrt(jnp.mean(xc * xc, axis=-1, keepdims=True) + CONV_LN_EPS) * ln_g_ref[...] + ln_b_ref[...]
    c_act = _silu(ln).astype(BF16)
    c_branch = jnp.dot(c_act, w_pw2_ref[...], preferred_element_type=F32) * rest[:, SWA_WIDTH + CONV_CHANNELS:]
    out_ref[0, :, SWA_WIDTH:] = c_branch.astype(BF16)


def _ab_mix(sinks, qkv, rest, conv_w, conv_b, ln_g, ln_b, w_pw2):
    batch, lp, qkv_w = qkv.shape
    nb = lp // BLOCK
    meta_blk = nb - 1
    rest_w = rest.shape[-1]
    kv_blk = (SWA_WIDTH // (2 * SWA_KV_WIDTH))
    prev_blk = lambda i: jnp.where(i == 0, meta_blk, i - 1)
    const2 = lambda b, i: (0, 0)
    return pl.pallas_call(
        _ab_mix_kernel,
        grid=(batch, nb),
        in_specs=[
            pl.BlockSpec(memory_space=pltpu.SMEM),
            pl.BlockSpec((1, BLOCK, qkv_w), lambda b, i: (b, i, 0)),
            pl.BlockSpec((1, BLOCK, 2 * SWA_KV_WIDTH), lambda b, i: (b, prev_blk(i), kv_blk)),
            pl.BlockSpec((1, BLOCK, 2 * SWA_KV_WIDTH), lambda b, i: (b, meta_blk, kv_blk)),
            pl.BlockSpec((1, BLOCK, rest_w), lambda b, i: (b, i, 0)),
            pl.BlockSpec((1, BLOCK, CONV_CHANNELS), lambda b, i: (b, prev_blk(i), 1)),
            pl.BlockSpec((CONV_WIDTH + 1, CONV_CHANNELS), const2),
            pl.BlockSpec((1, CONV_CHANNELS), const2),
            pl.BlockSpec((1, CONV_CHANNELS), const2),
            pl.BlockSpec((1, CONV_CHANNELS), const2),
            pl.BlockSpec((CONV_CHANNELS, CONV_CHANNELS), const2),
        ],
        out_specs=pl.BlockSpec((1, BLOCK, AB_MIX), lambda b, i: (b, i, 0)),
        out_shape=jax.ShapeDtypeStruct((batch, lp, AB_MIX), BF16),
        scratch_shapes=[
            pltpu.VMEM((2 * BLOCK, CONV_CHANNELS), F32),
            pltpu.VMEM((SUBLANES - 1, CONV_WIN_ROWS, CONV_CHANNELS), F32),
            pltpu.VMEM((SWA_HEADS, BLOCK, 3 * BLOCK), F32),
            pltpu.VMEM((SWA_HEADS, BLOCK, 3 * BLOCK), BF16),
        ],
        compiler_params=_params("parallel", "parallel"),
        name="ab_mix",
    )(sinks, qkv, qkv, qkv, rest, rest, conv_w, conv_b, ln_g, ln_b, w_pw2)


def _mid_kernel(mix_ref, h_ref, w_out_ref, post_ref, gain_ref, w_in_ref, h1_ref, qkv_ref, gate_ref):
    y = jnp.dot(mix_ref[...], w_out_ref[...], preferred_element_type=F32)
    h1 = h_ref[...] + _rms_normed(y, post_ref[...])
    h1_ref[...] = h1
    hn = _rms_normed(h1, gain_ref[...]).astype(BF16)
    chunk = 4 * LANES
    for c in range(3 * SB_WIDTH // chunk):
        lo = c * chunk
        scale = HEAD_DIM ** -0.5 if lo < SB_WIDTH else 1.0
        qkv_ref[:, lo:lo + chunk] = (jnp.dot(hn, w_in_ref[:, lo:lo + chunk], preferred_element_type=F32) * scale).astype(BF16)
    for c in range(SB_WIDTH // chunk):
        lo = 3 * SB_WIDTH + c * chunk
        gate_ref[:, c * chunk:(c + 1) * chunk] = _silu(jnp.dot(hn, w_in_ref[:, lo:lo + chunk], preferred_element_type=F32))


def _mid(mix_flat, h_flat, w_out_bf, post, gain, w_in_bf):
    rows = h_flat.shape[0]
    row_spec = lambda width: pl.BlockSpec((ROW_TILE, width), lambda r: (r, 0))
    const = lambda shape: pl.BlockSpec(shape, lambda r: (0, 0))
    return pl.pallas_call(
        _mid_kernel,
        grid=(rows // ROW_TILE,),
        in_specs=[
            row_spec(AB_MIX),
            row_spec(D_MODEL),
            const((AB_MIX, D_MODEL)),
            const((1, D_MODEL)),
            const((1, D_MODEL)),
            const((D_MODEL, SB_IN)),
        ],
        out_specs=[row_spec(D_MODEL), row_spec(3 * SB_WIDTH), row_spec(SB_WIDTH)],
        out_shape=[
            jax.ShapeDtypeStruct((rows, D_MODEL), F32),
            jax.ShapeDtypeStruct((rows, 3 * SB_WIDTH), BF16),
            jax.ShapeDtypeStruct((rows, SB_WIDTH), F32),
        ],
        compiler_params=_params("parallel"),
        name="ab_out_sb_in_proj",
    )(mix_flat, h_flat, w_out_bf, post, gain, w_in_bf)


def _sb_attn_kernel(q_ref, k_ref, v_ref, gate_ref, tri_ref, out_ref,
                    carry_ref, acc_ref, z_ref, lb_ref, split_ref, sums_ref, w_ref):
    meta_off = k_ref.shape[1] - BLOCK
    q_blk = pl.program_id(1) + 1
    lane = lax.broadcasted_iota(jnp.int32, (BLOCK, LANES), 1)
    row = lax.broadcasted_iota(jnp.int32, (BLOCK, LANES), 0)
    low_half = lane < HEAD_DIM
    qpos = q_blk * BLOCK + row
    zero_bf = jnp.zeros((), BF16)

    carry_ref[...] = jnp.zeros_like(carry_ref)
    acc_ref[...] = jnp.zeros_like(acc_ref)

    def block_diag(x):
        return jnp.concatenate([jnp.where(low_half, x, zero_bf), jnp.where(low_half, zero_bf, x)], axis=0)

    def keep_going(state):
        t, least_mass = state
        return (t <= q_blk) & (least_mass < SB_MASS_CUTOFF)

    def body(state):
        t, _ = state
        k_blk = q_blk - t
        off = pl.multiple_of(jnp.where(k_blk == 0, meta_off, (k_blk - 1) * BLOCK), BLOCK)
        kpos = k_blk * BLOCK + lane
        valid = (kpos >= META_PAD) & (kpos < qpos)
        for p in range(SB_PAIRS):
            cols = slice(p * LANES, (p + 1) * LANES)
            k_bd = block_diag(k_ref[0, pl.ds(off, BLOCK), cols])
            z_ref[p] = lax.dot_general(q_ref[0, :, cols], k_bd, (((1,), (1,)), ((), ())), preferred_element_type=F32)
        for n in range(2 * SB_PAIRS):
            z = z_ref[n // 2, :, (n % 2) * LANES:(n % 2 + 1) * LANES]
            sp = jnp.maximum(z, 0.0) + jnp.log(1.0 + jnp.exp(-jnp.abs(z)))
            lb_ref[n] = z - sp
            sp = jnp.where(valid, sp, 0.0)
            hi = sp.astype(BF16)
            split_ref[n, :, 0:LANES] = hi
            split_ref[n, :, LANES:] = (sp - hi.astype(F32)).astype(BF16)
        for n in range(2 * SB_PAIRS):
            sums_ref[n] = jnp.dot(split_ref[n], tri_ref[...], preferred_element_type=F32)
        least = None
        for n in range(2 * SB_PAIRS):
            later_blocks = carry_ref[n]
            later = later_blocks + sums_ref[n, :, 0:LANES]
            w_ref[n // 2, :, (n % 2) * LANES:(n % 2 + 1) * LANES] = jnp.where(
                valid, jnp.exp(lb_ref[n] - later), 0.0).astype(BF16)
            new_carry = later_blocks + sums_ref[n, :, LANES:]
            carry_ref[n] = new_carry
            least = new_carry if least is None else jnp.minimum(least, new_carry)
        for p in range(SB_PAIRS):
            cols = slice(p * LANES, (p + 1) * LANES)
            v_bd = block_diag(v_ref[0, pl.ds(off, BLOCK), cols])
            acc_ref[p] += jnp.dot(w_ref[p], v_bd, preferred_element_type=F32)
        return t + 1, jnp.min(least)

    lax.while_loop(keep_going, body, (jnp.int32(0), jnp.float32(0.0)))
    for p in range(SB_PAIRS):
        cols = slice(p * LANES, (p + 1) * LANES)
        out_ref[0, :, cols] = (acc_ref[p] * gate_ref[0, :, cols]).astype(BF16)


def _sb_attn(qkv, gate, tri, n_q):
    batch, lp, _ = qkv.shape
    return pl.pallas_call(
        _sb_attn_kernel,
        grid=(batch, n_q),
        in_specs=[
            pl.BlockSpec((1, BLOCK, SB_WIDTH), lambda b, i: (b, i, 0)),
            pl.BlockSpec((1, lp, SB_WIDTH), lambda b, i: (b, 0, 1)),
            pl.BlockSpec((1, lp, SB_WIDTH), lambda b, i: (b, 0, 2)),
            pl.BlockSpec((1, BLOCK, SB_WIDTH), lambda b, i: (b, i, 0)),
            pl.BlockSpec((2 * BLOCK, 2 * LANES), lambda b, i: (0, 0)),
        ],
        out_specs=pl.BlockSpec((1, BLOCK, SB_WIDTH), lambda b, i: (b, i, 0)),
        out_shape=jax.ShapeDtypeStruct((batch, n_q * BLOCK, SB_WIDTH), BF16),
        scratch_shapes=[
            pltpu.VMEM((2 * SB_PAIRS, BLOCK, LANES), F32),
            pltpu.VMEM((SB_PAIRS, BLOCK, LANES), F32),
            pltpu.VMEM((SB_PAIRS, BLOCK, 2 * LANES), F32),
            pltpu.VMEM((2 * SB_PAIRS, BLOCK, LANES), F32),
            pltpu.VMEM((2 * SB_PAIRS, BLOCK, 2 * LANES), BF16),
            pltpu.VMEM((2 * SB_PAIRS, BLOCK, 2 * LANES), F32),
            pltpu.VMEM((SB_PAIRS, BLOCK, 2 * LANES), BF16),
        ],
        compiler_params=_params("parallel", "arbitrary"),
        name="sb_attn",
    )(qkv, qkv, qkv, gate, tri)


def _sb_out_kernel(o_ref, h_ref, w_ref, post_ref, out_ref):
    y = jnp.dot(o_ref[0], w_ref[...], preferred_element_type=F32)
    out_ref[0] = h_ref[0] + _rms_normed(y, post_ref[...])


def _sb_out(o_gated, h, w_bf, post):
    batch, rows, _ = o_gated.shape
    return pl.pallas_call(
        _sb_out_kernel,
        grid=(batch, rows // ROW_TILE),
        in_specs=[
            pl.BlockSpec((1, ROW_TILE, SB_WIDTH), lambda b, i: (b, i, 0)),
            pl.BlockSpec((1, ROW_TILE, D_MODEL), lambda b, i: (b, i, 0)),
            pl.BlockSpec((SB_WIDTH, D_MODEL), lambda b, i: (0, 0)),
            pl.BlockSpec((1, D_MODEL), lambda b, i: (0, 0)),
        ],
        out_specs=pl.BlockSpec((1, ROW_TILE, D_MODEL), lambda b, i: (b, i, 0)),
        out_shape=jax.ShapeDtypeStruct((batch, rows, D_MODEL), F32),
        compiler_params=_params("parallel", "parallel"),
        name="sb_out_proj",
    )(o_gated, h, w_bf, post)


def _rope_tables(seq, batch):
    half = HEAD_DIM // 2
    pos = jnp.concatenate([N_META + jnp.arange(seq), jnp.zeros((META_PAD,), jnp.int32), jnp.arange(N_META)])
    inv = ROPE_THETA ** (-jnp.arange(half, dtype=jnp.float32) / half)
    ang = pos.astype(jnp.float32)[:, None] * inv[None, :]
    cos = jnp.cos(ang)
    sin = jnp.sin(ang)
    cos_t = jnp.tile(jnp.concatenate([cos, cos], axis=-1), (batch, LANES // HEAD_DIM))
    sin_t = jnp.tile(jnp.concatenate([-sin, sin], axis=-1), (batch, LANES // HEAD_DIM))
    return cos_t, sin_t


def kernel(x, meta_tokens, ab_pre_norm, ab_w_in, ab_sinks, ab_conv_w, ab_conv_b, ab_conv_ln_g, ab_conv_ln_b, ab_w_pw2, ab_w_out, ab_post_norm, sb_pre_norm, sb_w_in, sb_w_out, sb_post_norm):
    batch, seq, _ = x.shape
    lp = seq + BLOCK
    assert seq % ROW_TILE == 0 and (batch * lp) % ROW_TILE == 0

    meta = jnp.broadcast_to(meta_tokens[None].astype(x.dtype), (batch, N_META, D_MODEL))
    h0 = jnp.concatenate([x, jnp.zeros((batch, META_PAD, D_MODEL), x.dtype), meta], axis=1).reshape(batch * lp, D_MODEL)
    cos_t, sin_t = _rope_tables(seq, batch)

    qkv, rest = _ab_in(h0, ab_pre_norm[0][None], ab_w_in[0].astype(BF16), cos_t, sin_t)
    conv_w = jnp.concatenate([ab_conv_w[0], jnp.zeros((1, CONV_CHANNELS), F32)], axis=0)
    mix = _ab_mix(ab_sinks[0], qkv.reshape(batch, lp, -1), rest.reshape(batch, lp, -1),
                  conv_w, ab_conv_b[0][None], ab_conv_ln_g[0][None], ab_conv_ln_b[0][None],
                  ab_w_pw2[0].astype(BF16))

    h1, qkv1, gate1 = _mid(mix.reshape(batch * lp, AB_MIX), h0, ab_w_out[0].astype(BF16), ab_post_norm[0][None],
                           sb_pre_norm[0][None], sb_w_in[0].astype(BF16))

    r = jnp.arange(BLOCK)
    tri = jnp.concatenate([(r[:, None] > r[None, :]).astype(BF16), jnp.ones((BLOCK, LANES), BF16)], axis=1)
    tri = jnp.concatenate([tri, tri], axis=0)
    o_gated = _sb_attn(qkv1.reshape(batch, lp, -1), gate1.reshape(batch, lp, -1), tri, seq // BLOCK)
    return _sb_out(o_gated, h1.reshape(batch, lp, D_MODEL), sb_w_out[0].astype(BF16), sb_post_norm[0][None])
```

```python
import jax
import jax.numpy as jnp
from jax import lax
from jax.experimental import pallas as pl
from jax.experimental.pallas import tpu as pltpu

D_MODEL = 1024
N_META = 16
BLOCK = 128
META_PAD = BLOCK - N_META
HEAD_DIM = 64
ROPE_THETA = 10000.0
NORM_EPS = 1e-6
NEG_INF = -1e30
SWA_HEADS = 8
SWA_KV_HEADS = 2
SWA_GROUP = SWA_HEADS // SWA_KV_HEADS
SWA_WINDOW = 128
SWA_WIDTH = SWA_HEADS * HEAD_DIM
SWA_KV_WIDTH = SWA_KV_HEADS * HEAD_DIM
CONV_CHANNELS = 512
CONV_WIDTH = 31
CONV_LN_EPS = 1e-5
SB_HEADS = 16
SB_WIDTH = SB_HEADS * HEAD_DIM
AB_IN = 2 * SWA_WIDTH + 2 * SWA_KV_WIDTH + 3 * CONV_CHANNELS
AB_MIX = SWA_WIDTH + CONV_CHANNELS
SB_IN = 4 * SB_WIDTH

LANES = 128
SUBLANES = 8
ROW_TILE = 512
VMEM_LIMIT = 48 * 1024 * 1024
CONV_HALO = -(-(CONV_WIDTH - 1) // SUBLANES) * SUBLANES
CONV_WIN_START = BLOCK - CONV_HALO
CONV_WIN_ROWS = BLOCK + CONV_HALO
CONV_LEAD = CONV_HALO - (CONV_WIDTH - 1)

SB_PAIRS = SB_WIDTH // LANES
SB_MASS_CUTOFF = 104.0

F32 = jnp.float32
BF16 = jnp.bfloat16


def _rms_normed(x, gain):
    return x * lax.rsqrt(jnp.mean(x * x, axis=-1, keepdims=True) + NORM_EPS) * gain


def _silu(x):
    return x * jax.nn.sigmoid(x)


def _params(*semantics):
    return pltpu.CompilerParams(dimension_semantics=semantics, vmem_limit_bytes=VMEM_LIMIT)


def _ab_in_kernel(h_ref, gain_ref, w_ref, cos_ref, sin_ref, qkv_ref, rest_ref):
    hn = _rms_normed(h_ref[...], gain_ref[...]).astype(BF16)

    def proj(lo, hi):
        return jnp.dot(hn, w_ref[:, lo:hi], preferred_element_type=F32)

    cos = cos_ref[...]
    sin = sin_ref[...]
    lane = lax.broadcasted_iota(jnp.int32, cos.shape, 1)
    first_half = (lane % HEAD_DIM) < (HEAD_DIM // 2)

    def rope(t):
        rot = jnp.where(first_half, pltpu.roll(t, LANES - HEAD_DIM // 2, 1), pltpu.roll(t, HEAD_DIM // 2, 1))
        return t * cos + rot * sin

    q = proj(0, SWA_WIDTH)
    for c in range(SWA_WIDTH // LANES):
        cols = slice(c * LANES, (c + 1) * LANES)
        qkv_ref[:, cols] = (rope(q[:, cols]) * HEAD_DIM ** -0.5).astype(BF16)
    v_lo = SWA_WIDTH + SWA_KV_WIDTH
    kv = proj(SWA_WIDTH, v_lo + SWA_KV_WIDTH)
    qkv_ref[:, SWA_WIDTH:v_lo] = rope(kv[:, :SWA_KV_WIDTH]).astype(BF16)
    qkv_ref[:, v_lo:v_lo + SWA_KV_WIDTH] = kv[:, SWA_KV_WIDTH:].astype(BF16)

    ga_lo = v_lo + SWA_KV_WIDTH
    rest_ref[:, 0:SWA_WIDTH] = _silu(proj(ga_lo, ga_lo + SWA_WIDTH))
    glu_lo = ga_lo + SWA_WIDTH
    glu_a = proj(glu_lo, glu_lo + CONV_CHANNELS)
    glu_b = proj(glu_lo + CONV_CHANNELS, glu_lo + 2 * CONV_CHANNELS)
    rest_ref[:, SWA_WIDTH:SWA_WIDTH + CONV_CHANNELS] = glu_a * jax.nn.sigmoid(glu_b)
    gb_lo = glu_lo + 2 * CONV_CHANNELS
    rest_ref[:, SWA_WIDTH + CONV_CHANNELS:] = _silu(proj(gb_lo, gb_lo + CONV_CHANNELS))


def _ab_in(h_flat, gain, w_bf, cos_t, sin_t):
    rows = h_flat.shape[0]
    qkv_w = SWA_WIDTH + 2 * SWA_KV_WIDTH
    rest_w = SWA_WIDTH + 2 * CONV_CHANNELS
    return pl.pallas_call(
        _ab_in_kernel,
        grid=(rows // ROW_TILE,),
        in_specs=[
            pl.BlockSpec((ROW_TILE, D_MODEL), lambda r: (r, 0)),
            pl.BlockSpec((1, D_MODEL), lambda r: (0, 0)),
            pl.BlockSpec((D_MODEL, AB_IN), lambda r: (0, 0)),
            pl.BlockSpec((ROW_TILE, LANES), lambda r: (r, 0)),
            pl.BlockSpec((ROW_TILE, LANES), lambda r: (r, 0)),
        ],
        out_specs=[
            pl.BlockSpec((ROW_TILE, qkv_w), lambda r: (r, 0)),
            pl.BlockSpec((ROW_TILE, rest_w), lambda r: (r, 0)),
        ],
        out_shape=[
            jax.ShapeDtypeStruct((rows, qkv_w), BF16),
            jax.ShapeDtypeStruct((rows, rest_w), F32),
        ],
        compiler_params=_params("parallel"),
        name="ab_in_proj",
    )(h_flat, gain, w_bf, cos_t, sin_t)


def _ab_mix_kernel(sinks_ref, qkv_ref, kv_prev_ref, kv_meta_ref, rest_ref, u_prev_ref,
                   conv_w_ref, conv_b_ref, ln_g_ref, ln_b_ref, w_pw2_ref,
                   out_ref, ext_ref, rot_ref, s_ref, p_ref):
    n_blocks = pl.num_programs(1)
    i = pl.program_id(1)
    ref_blk = jnp.where(i == n_blocks - 1, 0, i + 1)
    qkv = qkv_ref[0]
    kv_prev = kv_prev_ref[0]
    kv_meta = kv_meta_ref[0]

    k_cat = jnp.concatenate([kv_meta[:, :LANES], kv_prev[:, :LANES], qkv[:, SWA_WIDTH:SWA_WIDTH + LANES]], axis=0)
    v_cat = jnp.concatenate([kv_meta[:, LANES:], kv_prev[:, LANES:], qkv[:, SWA_WIDTH + LANES:]], axis=0)
    k_swap = jnp.concatenate([k_cat[:, HEAD_DIM:], k_cat[:, :HEAD_DIM]], axis=1)
    v_swap = jnp.concatenate([v_cat[:, HEAD_DIM:], v_cat[:, :HEAD_DIM]], axis=1)

    row = lax.broadcasted_iota(jnp.int32, (BLOCK, BLOCK), 0)
    col = lax.broadcasted_iota(jnp.int32, (BLOCK, BLOCK), 1)
    qpos = ref_blk * BLOCK + row
    meta_ok = (col >= META_PAD) & (qpos - col >= SWA_WINDOW)
    ppos = (ref_blk - 1) * BLOCK + col
    prev_ok = (ppos >= META_PAD) & (qpos - ppos < SWA_WINDOW)
    cpos = ref_blk * BLOCK + col
    cur_ok = (cpos >= META_PAD) & (qpos >= cpos)
    mask = jnp.concatenate([meta_ok, prev_ok, cur_ok], axis=1)

    lane = lax.broadcasted_iota(jnp.int32, (BLOCK, LANES), 1)
    low_half = lane < HEAD_DIM
    zero_bf = jnp.zeros((), BF16)

    for head in range(SWA_HEADS):
        parity = head % 2
        q_pair = qkv[:, (head // 2) * LANES:(head // 2 + 1) * LANES]
        q_h = jnp.where(low_half if parity == 0 else ~low_half, q_pair, zero_bf)
        k_use = k_cat if head // SWA_GROUP == parity else k_swap
        s_ref[head] = lax.dot_general(q_h, k_use, (((1,), (1,)), ((), ())), preferred_element_type=F32)
    inv_denoms = []
    for head in range(SWA_HEADS):
        s = jnp.where(mask, s_ref[head], NEG_INF)
        sink = sinks_ref[head]
        m = jnp.maximum(jnp.max(s, axis=-1, keepdims=True), sink)
        p = jnp.exp(s - m)
        inv_denoms.append(1.0 / (jnp.sum(p, axis=-1, keepdims=True) + jnp.exp(sink - m)))
        p_ref[head] = p.astype(BF16)
    rest = rest_ref[0]
    for pair in range(SWA_HEADS // 2):
        outs = []
        for parity in range(2):
            head = 2 * pair + parity
            v_use = v_cat if head // SWA_GROUP == parity else v_swap
            outs.append(jnp.dot(p_ref[head], v_use, preferred_element_type=F32) * inv_denoms[head])
        cols = slice(pair * LANES, (pair + 1) * LANES)
        out_ref[0, :, cols] = (jnp.where(low_half, outs[0], outs[1]) * rest[:, cols]).astype(BF16)

    ext_ref[0:BLOCK, :] = jnp.where(ref_blk > 0, u_prev_ref[0], 0.0)
    ext_ref[BLOCK:2 * BLOCK, :] = rest[:, SWA_WIDTH:SWA_WIDTH + CONV_CHANNELS]
    window = ext_ref[CONV_WIN_START:, :]
    for b in range(1, SUBLANES):
        rot_ref[b - 1] = pltpu.roll(window, CONV_WIN_ROWS - b, 0)
    conv_groups = []
    for cg in range(CONV_CHANNELS // LANES):
        cols = slice(cg * LANES, (cg + 1) * LANES)
        acc = jnp.zeros((BLOCK, LANES), F32) + conv_b_ref[:, cols]
        for w in range(CONV_WIDTH):
            a, b = divmod(CONV_LEAD + w, SUBLANES)
            if b == 0:
                taps = ext_ref[CONV_WIN_START + SUBLANES * a:CONV_WIN_START + SUBLANES * a + BLOCK, cols]
            else:
                taps = rot_ref[b - 1, SUBLANES * a:SUBLANES * a + BLOCK, cols]
            acc = acc + taps * conv_w_ref[w:w + 1, cols]
        conv_groups.append(acc)
    conv = jnp.concatenate(conv_groups, axis=1)

    mu = jnp.mean(conv, axis=-1, keepdims=True)
    xc = conv - mu
    ln = xc * lax.rsqrt(jnp.mean(xc * xc, axis=-1, keepdims=True) + CONV_LN_EPS) * ln_g_ref[...] + ln_b_ref[...]
    c_act = _silu(ln).astype(BF16)
    c_branch = jnp.dot(c_act, w_pw2_ref[...], preferred_element_type=F32) * rest[:, SWA_WIDTH + CONV_CHANNELS:]
    out_ref[0, :, SWA_WIDTH:] = c_branch.astype(BF16)


def _ab_mix(sinks, qkv, rest, conv_w, conv_b, ln_g, ln_b, w_pw2):
    batch, lp, qkv_w = qkv.shape
    nb = lp // BLOCK
    meta_blk = nb - 1
    rest_w = rest.shape[-1]
    kv_blk = (SWA_WIDTH // (2 * SWA_KV_WIDTH))
    prev_blk = lambda i: jnp.where(i == 0, meta_blk, i - 1)
    const2 = lambda b, i: (0, 0)
    return pl.pallas_call(
        _ab_mix_kernel,
        grid=(batch, nb),
        in_specs=[
            pl.BlockSpec(memory_space=pltpu.SMEM),
            pl.BlockSpec((1, BLOCK, qkv_w), lambda b, i: (b, i, 0)),
            pl.BlockSpec((1, BLOCK, 2 * SWA_KV_WIDTH), lambda b, i: (b, prev_blk(i), kv_blk)),
            pl.BlockSpec((1, BLOCK, 2 * SWA_KV_WIDTH), lambda b, i: (b, meta_blk, kv_blk)),
            pl.BlockSpec((1, BLOCK, rest_w), lambda b, i: (b, i, 0)),
            pl.BlockSpec((1, BLOCK, CONV_CHANNELS), lambda b, i: (b, prev_blk(i), 1)),
            pl.BlockSpec((CONV_WIDTH + 1, CONV_CHANNELS), const2),
            pl.BlockSpec((1, CONV_CHANNELS), const2),
            pl.BlockSpec((1, CONV_CHANNELS), const2),
            pl.BlockSpec((1, CONV_CHANNELS), const2),
            pl.BlockSpec((CONV_CHANNELS, CONV_CHANNELS), const2),
        ],
        out_specs=pl.BlockSpec((1, BLOCK, AB_MIX), lambda b, i: (b, i, 0)),
        out_shape=jax.ShapeDtypeStruct((batch, lp, AB_MIX), BF16),
        scratch_shapes=[
            pltpu.VMEM((2 * BLOCK, CONV_CHANNELS), F32),
            pltpu.VMEM((SUBLANES - 1, CONV_WIN_ROWS, CONV_CHANNELS), F32),
            pltpu.VMEM((SWA_HEADS, BLOCK, 3 * BLOCK), F32),
            pltpu.VMEM((SWA_HEADS, BLOCK, 3 * BLOCK), BF16),
        ],
        compiler_params=_params("parallel", "parallel"),
        name="ab_mix",
    )(sinks, qkv, qkv, qkv, rest, rest, conv_w, conv_b, ln_g, ln_b, w_pw2)


def _mid_kernel(mix_ref, h_ref, w_out_ref, post_ref, gain_ref, w_in_ref, h1_ref, qkv_ref, gate_ref):
    y = jnp.dot(mix_ref[...], w_out_ref[...], preferred_element_type=F32)
    h1 = h_ref[...] + _rms_normed(y, post_ref[...])
    h1_ref[...] = h1
    hn = _rms_normed(h1, gain_ref[...]).astype(BF16)
    chunk = 4 * LANES
    for c in range(3 * SB_WIDTH // chunk):
        lo = c * chunk
        scale = HEAD_DIM ** -0.5 if lo < SB_WIDTH else 1.0
        qkv_ref[:, lo:lo + chunk] = (jnp.dot(hn, w_in_ref[:, lo:lo + chunk], preferred_element_type=F32) * scale).astype(BF16)
    for c in range(SB_WIDTH // chunk):
        lo = 3 * SB_WIDTH + c * chunk
        gate_ref[:, c * chunk:(c + 1) * chunk] = _silu(jnp.dot(hn, w_in_ref[:, lo:lo + chunk], preferred_element_type=F32))


def _mid(mix_flat, h_flat, w_out_bf, post, gain, w_in_bf):
    rows = h_flat.shape[0]
    row_spec = lambda width: pl.BlockSpec((ROW_TILE, width), lambda r: (r, 0))
    const = lambda shape: pl.BlockSpec(shape, lambda r: (0, 0))
    return pl.pallas_call(
        _mid_kernel,
        grid=(rows // ROW_TILE,),
        in_specs=[
            row_spec(AB_MIX),
            row_spec(D_MODEL),
            const((AB_MIX, D_MODEL)),
            const((1, D_MODEL)),
            const((1, D_MODEL)),
            const((D_MODEL, SB_IN)),
        ],
        out_specs=[row_spec(D_MODEL), row_spec(3 * SB_WIDTH), row_spec(SB_WIDTH)],
        out_shape=[
            jax.ShapeDtypeStruct((rows, D_MODEL), F32),
            jax.ShapeDtypeStruct((rows, 3 * SB_WIDTH), BF16),
            jax.ShapeDtypeStruct((rows, SB_WIDTH), F32),
        ],
        compiler_params=_params("parallel"),
        name="ab_out_sb_in_proj",
    )(mix_flat, h_flat, w_out_bf, post, gain, w_in_bf)


def _sb_attn_kernel(q_ref, k_ref, v_ref, gate_ref, tri_ref, out_ref,
                    carry_ref, acc_ref, z_ref, lb_ref, split_ref, sums_ref, w_ref):
    meta_off = k_ref.shape[1] - BLOCK
    q_blk = pl.program_id(1) + 1
    lane = lax.broadcasted_iota(jnp.int32, (BLOCK, LANES), 1)
    row = lax.broadcasted_iota(jnp.int32, (BLOCK, LANES), 0)
    low_half = lane < HEAD_DIM
    zero_bf = jnp.zeros((), BF16)

    def block_diag(x):
        return jnp.concatenate([jnp.where(low_half, x, zero_bf), jnp.where(low_half, zero_bf, x)], axis=0)

    def visit(off, valid, first):
        def scores(p):
            cols = slice(p * LANES, (p + 1) * LANES)
            k_bd = block_diag(k_ref[0, pl.ds(off, BLOCK), cols])
            z_ref[p] = lax.dot_general(q_ref[0, :, cols], k_bd, (((1,), (1,)), ((), ())), preferred_element_type=F32)

        def softplus_split(p):
            for n in (2 * p, 2 * p + 1):
                z = z_ref[p, :, (n % 2) * LANES:(n % 2 + 1) * LANES]
                sp = jnp.maximum(z, 0.0) + jnp.log(1.0 + jnp.exp(-jnp.abs(z)))
                lb_ref[n] = z - sp
                if valid is not None:
                    sp = jnp.where(valid, sp, 0.0)
                hi = sp.astype(BF16)
                split_ref[n, :, 0:LANES] = hi
                split_ref[n, :, LANES:] = (sp - hi.astype(F32)).astype(BF16)

        def suffix_sums(p):
            for n in (2 * p, 2 * p + 1):
                sums_ref[n] = jnp.dot(split_ref[n], tri_ref[...], preferred_element_type=F32)

        def carry_update(p, least):
            for n in (2 * p, 2 * p + 1):
                total = sums_ref[n, :, LANES:]
                if first:
                    new_carry = total
                else:
                    later_blocks = carry_ref[n]
                    new_carry = later_blocks + total
                    sums_ref[n, :, 0:LANES] = later_blocks + sums_ref[n, :, 0:LANES]
                carry_ref[n] = new_carry
                least = new_carry if least is None else jnp.minimum(least, new_carry)
            return least

        def weights(p):
            for n in (2 * p, 2 * p + 1):
                w = jnp.exp(lb_ref[n] - sums_ref[n, :, 0:LANES])
                if valid is not None:
                    w = jnp.where(valid, w, 0.0)
                w_ref[p, :, (n % 2) * LANES:(n % 2 + 1) * LANES] = w.astype(BF16)

        def accumulate(p):
            cols = slice(p * LANES, (p + 1) * LANES)
            o = jnp.dot(w_ref[p], block_diag(v_ref[0, pl.ds(off, BLOCK), cols]), preferred_element_type=F32)
            acc_ref[p] = o if first else acc_ref[p] + o

        pairs = range(SB_PAIRS)
        for p in pairs:
            scores(p)
        for p in pairs:
            softplus_split(p)
        for p in pairs:
            suffix_sums(p)
        least = None
        for p in pairs:
            least = carry_update(p, least)
        least_mass = jnp.min(least)
        for p in pairs:
            weights(p)
        for p in pairs:
            accumulate(p)
        return least_mass

    least0 = visit(pl.multiple_of((q_blk - 1) * BLOCK, BLOCK), lane < row, True)

    def keep_going(state):
        k_blk, least_mass = state
        return (k_blk >= 1) & (least_mass < SB_MASS_CUTOFF)

    def body(state):
        k_blk, _ = state
        return k_blk - 1, visit(pl.multiple_of((k_blk - 1) * BLOCK, BLOCK), None, False)

    _, least_mass = lax.while_loop(keep_going, body, (q_blk - 1, least0))

    @pl.when(least_mass < SB_MASS_CUTOFF)
    def _():
        visit(meta_off, lane >= META_PAD, False)

    for p in range(SB_PAIRS):
        cols = slice(p * LANES, (p + 1) * LANES)
        out_ref[0, :, cols] = (acc_ref[p] * gate_ref[0, :, cols]).astype(BF16)


def _sb_attn(qkv, gate, tri, n_q):
    batch, lp, _ = qkv.shape
    return pl.pallas_call(
        _sb_attn_kernel,
        grid=(batch, n_q),
        in_specs=[
            pl.BlockSpec((1, BLOCK, SB_WIDTH), lambda b, i: (b, i, 0)),
            pl.BlockSpec((1, lp, SB_WIDTH), lambda b, i: (b, 0, 1)),
            pl.BlockSpec((1, lp, SB_WIDTH), lambda b, i: (b, 0, 2)),
            pl.BlockSpec((1, BLOCK, SB_WIDTH), lambda b, i: (b, i, 0)),
            pl.BlockSpec((2 * BLOCK, 2 * LANES), lambda b, i: (0, 0)),
        ],
        out_specs=pl.BlockSpec((1, BLOCK, SB_WIDTH), lambda b, i: (b, i, 0)),
        out_shape=jax.ShapeDtypeStruct((batch, n_q * BLOCK, SB_WIDTH), BF16),
        scratch_shapes=[
            pltpu.VMEM((2 * SB_PAIRS, BLOCK, LANES), F32),
            pltpu.VMEM((SB_PAIRS, BLOCK, LANES), F32),
            pltpu.VMEM((SB_PAIRS, BLOCK, 2 * LANES), F32),
            pltpu.VMEM((2 * SB_PAIRS, BLOCK, LANES), F32),
            pltpu.VMEM((2 * SB_PAIRS, BLOCK, 2 * LANES), BF16),
            pltpu.VMEM((2 * SB_PAIRS, BLOCK, 2 * LANES), F32),
            pltpu.VMEM((SB_PAIRS, BLOCK, 2 * LANES), BF16),
        ],
        compiler_params=_params("parallel", "arbitrary"),
        name="sb_attn",
    )(qkv, qkv, qkv, gate, tri)


def _sb_out_kernel(o_ref, h_ref, w_ref, post_ref, out_ref):
    y = jnp.dot(o_ref[0], w_ref[...], preferred_element_type=F32)
    out_ref[0] = h_ref[0] + _rms_normed(y, post_ref[...])


def _sb_out(o_gated, h, w_bf, post):
    batch, rows, _ = o_gated.shape
    return pl.pallas_call(
        _sb_out_kernel,
        grid=(batch, rows // ROW_TILE),
        in_specs=[
            pl.BlockSpec((1, ROW_TILE, SB_WIDTH), lambda b, i: (b, i, 0)),
            pl.BlockSpec((1, ROW_TILE, D_MODEL), lambda b, i: (b, i, 0)),
            pl.BlockSpec((SB_WIDTH, D_MODEL), lambda b, i: (0, 0)),
            pl.BlockSpec((1, D_MODEL), lambda b, i: (0, 0)),
        ],
        out_specs=pl.BlockSpec((1, ROW_TILE, D_MODEL), lambda b, i: (b, i, 0)),
        out_shape=jax.ShapeDtypeStruct((batch, rows, D_MODEL), F32),
        compiler_params=_params("parallel", "parallel"),
        name="sb_out_proj",
    )(o_gated, h, w_bf, post)


def _rope_tables(seq, batch):
    half = HEAD_DIM // 2
    pos = jnp.concatenate([N_META + jnp.arange(seq), jnp.zeros((META_PAD,), jnp.int32), jnp.arange(N_META)])
    inv = ROPE_THETA ** (-jnp.arange(half, dtype=jnp.float32) / half)
    ang = pos.astype(jnp.float32)[:, None] * inv[None, :]
    cos = jnp.cos(ang)
    sin = jnp.sin(ang)
    cos_t = jnp.tile(jnp.concatenate([cos, cos], axis=-1), (batch, LANES // HEAD_DIM))
    sin_t = jnp.tile(jnp.concatenate([-sin, sin], axis=-1), (batch, LANES // HEAD_DIM))
    return cos_t, sin_t


def kernel(x, meta_tokens, ab_pre_norm, ab_w_in, ab_sinks, ab_conv_w, ab_conv_b, ab_conv_ln_g, ab_conv_ln_b, ab_w_pw2, ab_w_out, ab_post_norm, sb_pre_norm, sb_w_in, sb_w_out, sb_post_norm):
    batch, seq, _ = x.shape
    lp = seq + BLOCK
    assert seq % ROW_TILE == 0 and (batch * lp) % ROW_TILE == 0

    meta = jnp.broadcast_to(meta_tokens[None].astype(x.dtype), (batch, N_META, D_MODEL))
    h0 = jnp.concatenate([x, jnp.zeros((batch, META_PAD, D_MODEL), x.dtype), meta], axis=1).reshape(batch * lp, D_MODEL)
    cos_t, sin_t = _rope_tables(seq, batch)

    qkv, rest = _ab_in(h0, ab_pre_norm[0][None], ab_w_in[0].astype(BF16), cos_t, sin_t)
    conv_w = jnp.concatenate([ab_conv_w[0], jnp.zeros((1, CONV_CHANNELS), F32)], axis=0)
    mix = _ab_mix(ab_sinks[0], qkv.reshape(batch, lp, -1), rest.reshape(batch, lp, -1),
                  conv_w, ab_conv_b[0][None], ab_conv_ln_g[0][None], ab_conv_ln_b[0][None],
                  ab_w_pw2[0].astype(BF16))

    h1, qkv1, gate1 = _mid(mix.reshape(batch * lp, AB_MIX), h0, ab_w_out[0].astype(BF16), ab_post_norm[0][None],
                           sb_pre_norm[0][None], sb_w_in[0].astype(BF16))

    r = jnp.arange(BLOCK)
    tri = jnp.concatenate([(r[:, None] > r[None, :]).astype(BF16), jnp.ones((BLOCK, LANES), BF16)], axis=1)
    tri = jnp.concatenate([tri, tri], axis=0)
    o_gated = _sb_attn(qkv1.reshape(batch, lp, -1), gate1.reshape(batch, lp, -1), tri, seq // BLOCK)
    return _sb_out(o_gated, h1.reshape(batch, lp, D_MODEL), sb_w_out[0].astype(BF16), sb_post_norm[0][None])
```

```python
import jax
import jax.numpy as jnp
from jax import lax
from jax.experimental import pallas as pl
from jax.experimental.pallas import tpu as pltpu

D_MODEL = 1024
N_META = 16
BLOCK = 128
META_PAD = BLOCK - N_META
HEAD_DIM = 64
ROPE_THETA = 10000.0
NORM_EPS = 1e-6
NEG_INF = -1e30
SWA_HEADS = 8
SWA_KV_HEADS = 2
SWA_GROUP = SWA_HEADS // SWA_KV_HEADS
SWA_WINDOW = 128
SWA_WIDTH = SWA_HEADS * HEAD_DIM
SWA_KV_WIDTH = SWA_KV_HEADS * HEAD_DIM
CONV_CHANNELS = 512
CONV_WIDTH = 31
CONV_LN_EPS = 1e-5
SB_HEADS = 16
SB_WIDTH = SB_HEADS * HEAD_DIM
AB_IN = 2 * SWA_WIDTH + 2 * SWA_KV_WIDTH + 3 * CONV_CHANNELS
AB_MIX = SWA_WIDTH + CONV_CHANNELS
SB_IN = 4 * SB_WIDTH

LANES = 128
SUBLANES = 8
ROW_TILE = 512
VMEM_LIMIT = 48 * 1024 * 1024
CONV_HALO = -(-(CONV_WIDTH - 1) // SUBLANES) * SUBLANES
CONV_WIN_START = BLOCK - CONV_HALO
CONV_WIN_ROWS = BLOCK + CONV_HALO
CONV_LEAD = CONV_HALO - (CONV_WIDTH - 1)

SB_PAIRS = SB_WIDTH // LANES
SB_MASS_CUTOFF = 104.0
LOG2_E = 1.4426950408889634

F32 = jnp.float32
BF16 = jnp.bfloat16


def _rms_normed(x, gain):
    return x * lax.rsqrt(jnp.mean(x * x, axis=-1, keepdims=True) + NORM_EPS) * gain


def _silu(x):
    return x * jax.nn.sigmoid(x)


def _params(*semantics):
    return pltpu.CompilerParams(dimension_semantics=semantics, vmem_limit_bytes=VMEM_LIMIT)


def _ab_in_kernel(h_ref, gain_ref, w_ref, cos_ref, sin_ref, qkv_ref, rest_ref):
    hn = _rms_normed(h_ref[...], gain_ref[...]).astype(BF16)

    def proj(lo, hi):
        return jnp.dot(hn, w_ref[:, lo:hi], preferred_element_type=F32)

    cos = cos_ref[...]
    sin = sin_ref[...]
    lane = lax.broadcasted_iota(jnp.int32, cos.shape, 1)
    first_half = (lane % HEAD_DIM) < (HEAD_DIM // 2)

    def rope(t):
        rot = jnp.where(first_half, pltpu.roll(t, LANES - HEAD_DIM // 2, 1), pltpu.roll(t, HEAD_DIM // 2, 1))
        return t * cos + rot * sin

    q = proj(0, SWA_WIDTH)
    for c in range(SWA_WIDTH // LANES):
        cols = slice(c * LANES, (c + 1) * LANES)
        qkv_ref[:, cols] = (rope(q[:, cols]) * HEAD_DIM ** -0.5).astype(BF16)
    v_lo = SWA_WIDTH + SWA_KV_WIDTH
    kv = proj(SWA_WIDTH, v_lo + SWA_KV_WIDTH)
    qkv_ref[:, SWA_WIDTH:v_lo] = rope(kv[:, :SWA_KV_WIDTH]).astype(BF16)
    qkv_ref[:, v_lo:v_lo + SWA_KV_WIDTH] = kv[:, SWA_KV_WIDTH:].astype(BF16)

    ga_lo = v_lo + SWA_KV_WIDTH
    rest_ref[:, 0:SWA_WIDTH] = _silu(proj(ga_lo, ga_lo + SWA_WIDTH))
    glu_lo = ga_lo + SWA_WIDTH
    glu_a = proj(glu_lo, glu_lo + CONV_CHANNELS)
    glu_b = proj(glu_lo + CONV_CHANNELS, glu_lo + 2 * CONV_CHANNELS)
    rest_ref[:, SWA_WIDTH:SWA_WIDTH + CONV_CHANNELS] = glu_a * jax.nn.sigmoid(glu_b)
    gb_lo = glu_lo + 2 * CONV_CHANNELS
    rest_ref[:, SWA_WIDTH + CONV_CHANNELS:] = _silu(proj(gb_lo, gb_lo + CONV_CHANNELS))


def _ab_in(h_flat, gain, w_bf, cos_t, sin_t):
    rows = h_flat.shape[0]
    qkv_w = SWA_WIDTH + 2 * SWA_KV_WIDTH
    rest_w = SWA_WIDTH + 2 * CONV_CHANNELS
    return pl.pallas_call(
        _ab_in_kernel,
        grid=(rows // ROW_TILE,),
        in_specs=[
            pl.BlockSpec((ROW_TILE, D_MODEL), lambda r: (r, 0)),
            pl.BlockSpec((1, D_MODEL), lambda r: (0, 0)),
            pl.BlockSpec((D_MODEL, AB_IN), lambda r: (0, 0)),
            pl.BlockSpec((ROW_TILE, LANES), lambda r: (r, 0)),
            pl.BlockSpec((ROW_TILE, LANES), lambda r: (r, 0)),
        ],
        out_specs=[
            pl.BlockSpec((ROW_TILE, qkv_w), lambda r: (r, 0)),
            pl.BlockSpec((ROW_TILE, rest_w), lambda r: (r, 0)),
        ],
        out_shape=[
            jax.ShapeDtypeStruct((rows, qkv_w), BF16),
            jax.ShapeDtypeStruct((rows, rest_w), F32),
        ],
        compiler_params=_params("parallel"),
        name="ab_in_proj",
    )(h_flat, gain, w_bf, cos_t, sin_t)


def _ab_mix_kernel(sinks_ref, qkv_ref, kv_prev_ref, kv_meta_ref, rest_ref, u_prev_ref,
                   conv_w_ref, conv_b_ref, ln_g_ref, ln_b_ref, w_pw2_ref,
                   out_ref, ext_ref, rot_ref, s_ref, p_ref):
    n_blocks = pl.num_programs(1)
    i = pl.program_id(1)
    ref_blk = jnp.where(i == n_blocks - 1, 0, i + 1)
    qkv = qkv_ref[0]
    kv_prev = kv_prev_ref[0]
    kv_meta = kv_meta_ref[0]

    k_cat = jnp.concatenate([kv_meta[:, :LANES], kv_prev[:, :LANES], qkv[:, SWA_WIDTH:SWA_WIDTH + LANES]], axis=0)
    v_cat = jnp.concatenate([kv_meta[:, LANES:], kv_prev[:, LANES:], qkv[:, SWA_WIDTH + LANES:]], axis=0)
    k_swap = jnp.concatenate([k_cat[:, HEAD_DIM:], k_cat[:, :HEAD_DIM]], axis=1)
    v_swap = jnp.concatenate([v_cat[:, HEAD_DIM:], v_cat[:, :HEAD_DIM]], axis=1)

    row = lax.broadcasted_iota(jnp.int32, (BLOCK, BLOCK), 0)
    col = lax.broadcasted_iota(jnp.int32, (BLOCK, BLOCK), 1)
    qpos = ref_blk * BLOCK + row
    meta_ok = (col >= META_PAD) & (qpos - col >= SWA_WINDOW)
    ppos = (ref_blk - 1) * BLOCK + col
    prev_ok = (ppos >= META_PAD) & (qpos - ppos < SWA_WINDOW)
    cpos = ref_blk * BLOCK + col
    cur_ok = (cpos >= META_PAD) & (qpos >= cpos)
    mask = jnp.concatenate([meta_ok, prev_ok, cur_ok], axis=1)

    lane = lax.broadcasted_iota(jnp.int32, (BLOCK, LANES), 1)
    low_half = lane < HEAD_DIM
    zero_bf = jnp.zeros((), BF16)

    for head in range(SWA_HEADS):
        parity = head % 2
        q_pair = qkv[:, (head // 2) * LANES:(head // 2 + 1) * LANES]
        q_h = jnp.where(low_half if parity == 0 else ~low_half, q_pair, zero_bf)
        k_use = k_cat if head // SWA_GROUP == parity else k_swap
        s_ref[head] = lax.dot_general(q_h, k_use, (((1,), (1,)), ((), ())), preferred_element_type=F32)
    inv_denoms = []
    for head in range(SWA_HEADS):
        s = jnp.where(mask, s_ref[head], NEG_INF)
        sink = sinks_ref[head]
        m = jnp.maximum(jnp.max(s, axis=-1, keepdims=True), sink)
        p = jnp.exp(s - m)
        inv_denoms.append(1.0 / (jnp.sum(p, axis=-1, keepdims=True) + jnp.exp(sink - m)))
        p_ref[head] = p.astype(BF16)
    rest = rest_ref[0]
    for pair in range(SWA_HEADS // 2):
        outs = []
        for parity in range(2):
            head = 2 * pair + parity
            v_use = v_cat if head // SWA_GROUP == parity else v_swap
            outs.append(jnp.dot(p_ref[head], v_use, preferred_element_type=F32) * inv_denoms[head])
        cols = slice(pair * LANES, (pair + 1) * LANES)
        out_ref[0, :, cols] = (jnp.where(low_half, outs[0], outs[1]) * rest[:, cols]).astype(BF16)

    ext_ref[0:BLOCK, :] = jnp.where(ref_blk > 0, u_prev_ref[0], 0.0)
    ext_ref[BLOCK:2 * BLOCK, :] = rest[:, SWA_WIDTH:SWA_WIDTH + CONV_CHANNELS]
    window = ext_ref[CONV_WIN_START:, :]
    for b in range(1, SUBLANES):
        rot_ref[b - 1] = pltpu.roll(window, CONV_WIN_ROWS - b, 0)
    conv_groups = []
    for cg in range(CONV_CHANNELS // LANES):
        cols = slice(cg * LANES, (cg + 1) * LANES)
        acc = jnp.zeros((BLOCK, LANES), F32) + conv_b_ref[:, cols]
        for w in range(CONV_WIDTH):
            a, b = divmod(CONV_LEAD + w, SUBLANES)
            if b == 0:
                taps = ext_ref[CONV_WIN_START + SUBLANES * a:CONV_WIN_START + SUBLANES * a + BLOCK, cols]
            else:
                taps = rot_ref[b - 1, SUBLANES * a:SUBLANES * a + BLOCK, cols]
            acc = acc + taps * conv_w_ref[w:w + 1, cols]
        conv_groups.append(acc)
    conv = jnp.concatenate(conv_groups, axis=1)

    mu = jnp.mean(conv, axis=-1, keepdims=True)
    xc = conv - mu
    ln = xc * lax.rsqrt(jnp.mean(xc * xc, axis=-1, keepdims=True) + CONV_LN_EPS) * ln_g_ref[...] + ln_b_ref[...]
    c_act = _silu(ln).astype(BF16)
    c_branch = jnp.dot(c_act, w_pw2_ref[...], preferred_element_type=F32) * rest[:, SWA_WIDTH + CONV_CHANNELS:]
    out_ref[0, :, SWA_WIDTH:] = c_branch.astype(BF16)


def _ab_mix(sinks, qkv, rest, conv_w, conv_b, ln_g, ln_b, w_pw2):
    batch, lp, qkv_w = qkv.shape
    nb = lp // BLOCK
    meta_blk = nb - 1
    rest_w = rest.shape[-1]
    kv_blk = (SWA_WIDTH // (2 * SWA_KV_WIDTH))
    prev_blk = lambda i: jnp.where(i == 0, meta_blk, i - 1)
    const2 = lambda b, i: (0, 0)
    return pl.pallas_call(
        _ab_mix_kernel,
        grid=(batch, nb),
        in_specs=[
            pl.BlockSpec(memory_space=pltpu.SMEM),
            pl.BlockSpec((1, BLOCK, qkv_w), lambda b, i: (b, i, 0)),
            pl.BlockSpec((1, BLOCK, 2 * SWA_KV_WIDTH), lambda b, i: (b, prev_blk(i), kv_blk)),
            pl.BlockSpec((1, BLOCK, 2 * SWA_KV_WIDTH), lambda b, i: (b, meta_blk, kv_blk)),
            pl.BlockSpec((1, BLOCK, rest_w), lambda b, i: (b, i, 0)),
            pl.BlockSpec((1, BLOCK, CONV_CHANNELS), lambda b, i: (b, prev_blk(i), 1)),
            pl.BlockSpec((CONV_WIDTH + 1, CONV_CHANNELS), const2),
            pl.BlockSpec((1, CONV_CHANNELS), const2),
            pl.BlockSpec((1, CONV_CHANNELS), const2),
            pl.BlockSpec((1, CONV_CHANNELS), const2),
            pl.BlockSpec((CONV_CHANNELS, CONV_CHANNELS), const2),
        ],
        out_specs=pl.BlockSpec((1, BLOCK, AB_MIX), lambda b, i: (b, i, 0)),
        out_shape=jax.ShapeDtypeStruct((batch, lp, AB_MIX), BF16),
        scratch_shapes=[
            pltpu.VMEM((2 * BLOCK, CONV_CHANNELS), F32),
            pltpu.VMEM((SUBLANES - 1, CONV_WIN_ROWS, CONV_CHANNELS), F32),
            pltpu.VMEM((SWA_HEADS, BLOCK, 3 * BLOCK), F32),
            pltpu.VMEM((SWA_HEADS, BLOCK, 3 * BLOCK), BF16),
        ],
        compiler_params=_params("parallel", "parallel"),
        name="ab_mix",
    )(sinks, qkv, qkv, qkv, rest, rest, conv_w, conv_b, ln_g, ln_b, w_pw2)


def _mid_kernel(mix_ref, h_ref, w_out_ref, post_ref, gain_ref, w_in_ref, h1_ref, qkv_ref, gate_ref):
    y = jnp.dot(mix_ref[...], w_out_ref[...], preferred_element_type=F32)
    h1 = h_ref[...] + _rms_normed(y, post_ref[...])
    h1_ref[...] = h1
    hn = _rms_normed(h1, gain_ref[...]).astype(BF16)
    chunk = 4 * LANES
    even_head = (lax.broadcasted_iota(jnp.int32, (ROW_TILE, chunk), 1) % LANES) < HEAD_DIM

    def proj(lo):
        return jnp.dot(hn, w_in_ref[:, lo:lo + chunk], preferred_element_type=F32)

    for c in range(SB_WIDTH // chunk):
        lo = c * chunk
        qkv_ref[:, lo:lo + chunk] = (proj(lo) * HEAD_DIM ** -0.5).astype(BF16)
        qkv_ref[:, SB_WIDTH + lo:SB_WIDTH + lo + chunk] = proj(SB_WIDTH + lo).astype(BF16)
        v = proj(2 * SB_WIDTH + lo)
        qkv_ref[:, 2 * SB_WIDTH + lo:2 * SB_WIDTH + lo + chunk] = jnp.where(even_head, v, 0.0).astype(BF16)
        qkv_ref[:, 3 * SB_WIDTH + lo:3 * SB_WIDTH + lo + chunk] = jnp.where(even_head, 0.0, v).astype(BF16)
        gate_ref[:, lo:lo + chunk] = _silu(proj(3 * SB_WIDTH + lo))


def _mid(mix_flat, h_flat, w_out_bf, post, gain, w_in_bf):
    rows = h_flat.shape[0]
    row_spec = lambda width: pl.BlockSpec((ROW_TILE, width), lambda r: (r, 0))
    const = lambda shape: pl.BlockSpec(shape, lambda r: (0, 0))
    return pl.pallas_call(
        _mid_kernel,
        grid=(rows // ROW_TILE,),
        in_specs=[
            row_spec(AB_MIX),
            row_spec(D_MODEL),
            const((AB_MIX, D_MODEL)),
            const((1, D_MODEL)),
            const((1, D_MODEL)),
            const((D_MODEL, SB_IN)),
        ],
        out_specs=[row_spec(D_MODEL), row_spec(4 * SB_WIDTH), row_spec(SB_WIDTH)],
        out_shape=[
            jax.ShapeDtypeStruct((rows, D_MODEL), F32),
            jax.ShapeDtypeStruct((rows, 4 * SB_WIDTH), BF16),
            jax.ShapeDtypeStruct((rows, SB_WIDTH), F32),
        ],
        compiler_params=_params("parallel"),
        name="ab_out_sb_in_proj",
    )(mix_flat, h_flat, w_out_bf, post, gain, w_in_bf)


def _sb_attn_kernel(q_ref, k_ref, v_even_ref, v_odd_ref, gate_ref, tri_ref, out_ref,
                    carry_ref, acc_ref, qh_ref, z_ref, lb_ref, split_ref, sums_ref, w_ref, bias_ref):
    meta_off = k_ref.shape[1] - BLOCK
    q_blk = pl.program_id(1) + 1
    lane = lax.broadcasted_iota(jnp.int32, (BLOCK, LANES), 1)
    row = lax.broadcasted_iota(jnp.int32, (BLOCK, LANES), 0)
    low_half = lane < HEAD_DIM
    zero_bf = jnp.zeros((), BF16)

    qpos = q_blk * BLOCK + row
    carry_ref[...] = jnp.zeros_like(carry_ref)
    acc_ref[...] = jnp.zeros_like(acc_ref)
    for n in range(2 * SB_PAIRS):
        q_pair = q_ref[0, :, (n // 2) * LANES:(n // 2 + 1) * LANES]
        qh_ref[n] = jnp.where(low_half, q_pair, zero_bf) if n % 2 == 0 else jnp.where(low_half, zero_bf, q_pair)

    def row_offset(k_blk):
        return pl.multiple_of(jnp.where(k_blk == 0, meta_off, (k_blk - 1) * BLOCK), BLOCK)

    def visit(k_blk):
        off = row_offset(k_blk)
        kpos = k_blk * BLOCK + lane

        def scores(n):
            k_pair = k_ref[0, pl.ds(off, BLOCK), (n // 2) * LANES:(n // 2 + 1) * LANES]
            z_ref[n] = lax.dot_general(qh_ref[n], k_pair, (((1,), (1,)), ((), ())), preferred_element_type=F32)

        bias_ref[...] = jnp.where((kpos >= META_PAD) & (kpos < qpos), 0.0, NEG_INF)

        def softplus_split(n):
            z = z_ref[n] + bias_ref[...]
            sp = jnp.maximum(z, 0.0) + jnp.log(1.0 + jnp.exp2(jnp.abs(z) * -LOG2_E))
            lb_ref[n] = z - sp
            hi = sp.astype(BF16)
            split_ref[n, :, 0:LANES] = hi
            split_ref[n, :, LANES:] = (sp - hi.astype(F32)).astype(BF16)

        def suffix_sums(n):
            sums_ref[n] = jnp.dot(split_ref[n], tri_ref[...], preferred_element_type=F32)

        def carry_update(n, least):
            later_blocks = carry_ref[n]
            new_carry = later_blocks + sums_ref[n, :, LANES:]
            sums_ref[n, :, 0:LANES] = later_blocks + sums_ref[n, :, 0:LANES]
            carry_ref[n] = new_carry
            return new_carry if least is None else jnp.minimum(least, new_carry)

        def weights(n):
            w = jnp.exp(lb_ref[n] - sums_ref[n, :, 0:LANES])
            w_ref[n // 2, :, (n % 2) * LANES:(n % 2 + 1) * LANES] = w.astype(BF16)

        def accumulate(p):
            cols = slice(p * LANES, (p + 1) * LANES)
            v_bd = jnp.concatenate([v_even_ref[0, pl.ds(off, BLOCK), cols], v_odd_ref[0, pl.ds(off, BLOCK), cols]], axis=0)
            acc_ref[p] += jnp.dot(w_ref[p], v_bd, preferred_element_type=F32)

        pairs, heads = range(SB_PAIRS), range(2 * SB_PAIRS)
        for n in heads:
            scores(n)
        for n in heads:
            softplus_split(n)
        for n in heads:
            suffix_sums(n)
        least = None
        for n in heads:
            least = carry_update(n, least)
        least_mass = jnp.min(least)
        for n in heads:
            weights(n)
        for p in pairs:
            accumulate(p)
        return least_mass

    def keep_going(state):
        k_blk, least_mass = state
        return (k_blk >= 0) & (least_mass < SB_MASS_CUTOFF)

    def body(state):
        k_blk, _ = state
        return k_blk - 1, visit(k_blk)

    lax.while_loop(keep_going, body, (q_blk, jnp.float32(0.0)))

    for p in range(SB_PAIRS):
        cols = slice(p * LANES, (p + 1) * LANES)
        out_ref[0, :, cols] = (acc_ref[p] * gate_ref[0, :, cols]).astype(BF16)


def _sb_attn(qkv, gate, tri, n_q):
    batch, lp, _ = qkv.shape
    return pl.pallas_call(
        _sb_attn_kernel,
        grid=(batch, n_q),
        in_specs=[
            pl.BlockSpec((1, BLOCK, SB_WIDTH), lambda b, i: (b, i, 0)),
            pl.BlockSpec((1, lp, SB_WIDTH), lambda b, i: (b, 0, 1)),
            pl.BlockSpec((1, lp, SB_WIDTH), lambda b, i: (b, 0, 2)),
            pl.BlockSpec((1, lp, SB_WIDTH), lambda b, i: (b, 0, 3)),
            pl.BlockSpec((1, BLOCK, SB_WIDTH), lambda b, i: (b, i, 0)),
            pl.BlockSpec((2 * BLOCK, 2 * LANES), lambda b, i: (0, 0)),
        ],
        out_specs=pl.BlockSpec((1, BLOCK, SB_WIDTH), lambda b, i: (b, i, 0)),
        out_shape=jax.ShapeDtypeStruct((batch, n_q * BLOCK, SB_WIDTH), BF16),
        scratch_shapes=[
            pltpu.VMEM((2 * SB_PAIRS, BLOCK, LANES), F32),
            pltpu.VMEM((SB_PAIRS, BLOCK, LANES), F32),
            pltpu.VMEM((2 * SB_PAIRS, BLOCK, LANES), BF16),
            pltpu.VMEM((2 * SB_PAIRS, BLOCK, LANES), F32),
            pltpu.VMEM((2 * SB_PAIRS, BLOCK, LANES), F32),
            pltpu.VMEM((2 * SB_PAIRS, BLOCK, 2 * LANES), BF16),
            pltpu.VMEM((2 * SB_PAIRS, BLOCK, 2 * LANES), F32),
            pltpu.VMEM((SB_PAIRS, BLOCK, 2 * LANES), BF16),
            pltpu.VMEM((BLOCK, LANES), F32),
        ],
        compiler_params=_params("parallel", "arbitrary"),
        name="sb_attn",
    )(qkv, qkv, qkv, qkv, gate, tri)


def _sb_out_kernel(o_ref, h_ref, w_ref, post_ref, out_ref):
    y = jnp.dot(o_ref[0], w_ref[...], preferred_element_type=F32)
    out_ref[0] = h_ref[0] + _rms_normed(y, post_ref[...])


def _sb_out(o_gated, h, w_bf, post):
    batch, rows, _ = o_gated.shape
    return pl.pallas_call(
        _sb_out_kernel,
        grid=(batch, rows // ROW_TILE),
        in_specs=[
            pl.BlockSpec((1, ROW_TILE, SB_WIDTH), lambda b, i: (b, i, 0)),
            pl.BlockSpec((1, ROW_TILE, D_MODEL), lambda b, i: (b, i, 0)),
            pl.BlockSpec((SB_WIDTH, D_MODEL), lambda b, i: (0, 0)),
            pl.BlockSpec((1, D_MODEL), lambda b, i: (0, 0)),
        ],
        out_specs=pl.BlockSpec((1, ROW_TILE, D_MODEL), lambda b, i: (b, i, 0)),
        out_shape=jax.ShapeDtypeStruct((batch, rows, D_MODEL), F32),
        compiler_params=_params("parallel", "parallel"),
        name="sb_out_proj",
    )(o_gated, h, w_bf, post)


def _rope_tables(seq, batch):
    half = HEAD_DIM // 2
    pos = jnp.concatenate([N_META + jnp.arange(seq), jnp.zeros((META_PAD,), jnp.int32), jnp.arange(N_META)])
    inv = ROPE_THETA ** (-jnp.arange(half, dtype=jnp.float32) / half)
    ang = pos.astype(jnp.float32)[:, None] * inv[None, :]
    cos = jnp.cos(ang)
    sin = jnp.sin(ang)
    cos_t = jnp.tile(jnp.concatenate([cos, cos], axis=-1), (batch, LANES // HEAD_DIM))
    sin_t = jnp.tile(jnp.concatenate([-sin, sin], axis=-1), (batch, LANES // HEAD_DIM))
    return cos_t, sin_t


def kernel(x, meta_tokens, ab_pre_norm, ab_w_in, ab_sinks, ab_conv_w, ab_conv_b, ab_conv_ln_g, ab_conv_ln_b, ab_w_pw2, ab_w_out, ab_post_norm, sb_pre_norm, sb_w_in, sb_w_out, sb_post_norm):
    batch, seq, _ = x.shape
    lp = seq + BLOCK
    assert seq % ROW_TILE == 0 and (batch * lp) % ROW_TILE == 0

    meta = jnp.broadcast_to(meta_tokens[None].astype(x.dtype), (batch, N_META, D_MODEL))
    h0 = jnp.concatenate([x, jnp.zeros((batch, META_PAD, D_MODEL), x.dtype), meta], axis=1).reshape(batch * lp, D_MODEL)
    cos_t, sin_t = _rope_tables(seq, batch)

    qkv, rest = _ab_in(h0, ab_pre_norm[0][None], ab_w_in[0].astype(BF16), cos_t, sin_t)
    conv_w = jnp.concatenate([ab_conv_w[0], jnp.zeros((1, CONV_CHANNELS), F32)], axis=0)
    mix = _ab_mix(ab_sinks[0], qkv.reshape(batch, lp, -1), rest.reshape(batch, lp, -1),
                  conv_w, ab_conv_b[0][None], ab_conv_ln_g[0][None], ab_conv_ln_b[0][None],
                  ab_w_pw2[0].astype(BF16))

    h1, qkv1, gate1 = _mid(mix.reshape(batch * lp, AB_MIX), h0, ab_w_out[0].astype(BF16), ab_post_norm[0][None],
                           sb_pre_norm[0][None], sb_w_in[0].astype(BF16))

    r = jnp.arange(BLOCK)
    tri = jnp.concatenate([(r[:, None] > r[None, :]).astype(BF16), jnp.ones((BLOCK, LANES), BF16)], axis=1)
    tri = jnp.concatenate([tri, tri], axis=0)
    o_gated = _sb_attn(qkv1.reshape(batch, lp, -1), gate1.reshape(batch, lp, -1), tri, seq // BLOCK)
    return _sb_out(o_gated, h1.reshape(batch, lp, D_MODEL), sb_w_out[0].astype(BF16), sb_post_norm[0][None])
```

```python
import jax
import jax.numpy as jnp
from jax import lax
from jax.experimental import pallas as pl
from jax.experimental.pallas import tpu as pltpu

D_MODEL = 1024
N_META = 16
BLOCK = 128
META_PAD = BLOCK - N_META
HEAD_DIM = 64
ROPE_THETA = 10000.0
NORM_EPS = 1e-6
NEG_INF = -1e30
SWA_HEADS = 8
SWA_KV_HEADS = 2
SWA_GROUP = SWA_HEADS // SWA_KV_HEADS
SWA_WINDOW = 128
SWA_WIDTH = SWA_HEADS * HEAD_DIM
SWA_KV_WIDTH = SWA_KV_HEADS * HEAD_DIM
CONV_CHANNELS = 512
CONV_WIDTH = 31
CONV_LN_EPS = 1e-5
SB_HEADS = 16
SB_WIDTH = SB_HEADS * HEAD_DIM
AB_IN = 2 * SWA_WIDTH + 2 * SWA_KV_WIDTH + 3 * CONV_CHANNELS
AB_MIX = SWA_WIDTH + CONV_CHANNELS
SB_IN = 4 * SB_WIDTH

LANES = 128
SUBLANES = 8
ROW_TILE = 512
VMEM_LIMIT = 48 * 1024 * 1024
CONV_HALO = -(-(CONV_WIDTH - 1) // SUBLANES) * SUBLANES
CONV_WIN_START = BLOCK - CONV_HALO
CONV_WIN_ROWS = BLOCK + CONV_HALO
CONV_LEAD = CONV_HALO - (CONV_WIDTH - 1)

SB_PAIRS = SB_WIDTH // LANES
SB_Q_PER_STEP = 4
SB_MASS_CUTOFF = 104.0
LOG2_E = 1.4426950408889634

F32 = jnp.float32
BF16 = jnp.bfloat16


def _rms_normed(x, gain):
    return x * lax.rsqrt(jnp.mean(x * x, axis=-1, keepdims=True) + NORM_EPS) * gain


def _silu(x):
    return x * jax.nn.sigmoid(x)


def _params(*semantics):
    return pltpu.CompilerParams(dimension_semantics=semantics, vmem_limit_bytes=VMEM_LIMIT)


def _ab_in_kernel(h_ref, gain_ref, w_ref, cos_ref, sin_ref, qkv_ref, rest_ref):
    hn = _rms_normed(h_ref[...], gain_ref[...]).astype(BF16)

    def proj(lo, hi):
        return jnp.dot(hn, w_ref[:, lo:hi], preferred_element_type=F32)

    cos = cos_ref[...]
    sin = sin_ref[...]
    lane = lax.broadcasted_iota(jnp.int32, cos.shape, 1)
    first_half = (lane % HEAD_DIM) < (HEAD_DIM // 2)

    def rope(t):
        rot = jnp.where(first_half, pltpu.roll(t, LANES - HEAD_DIM // 2, 1), pltpu.roll(t, HEAD_DIM // 2, 1))
        return t * cos + rot * sin

    q = proj(0, SWA_WIDTH)
    for c in range(SWA_WIDTH // LANES):
        cols = slice(c * LANES, (c + 1) * LANES)
        qkv_ref[:, cols] = (rope(q[:, cols]) * HEAD_DIM ** -0.5).astype(BF16)
    v_lo = SWA_WIDTH + SWA_KV_WIDTH
    kv = proj(SWA_WIDTH, v_lo + SWA_KV_WIDTH)
    qkv_ref[:, SWA_WIDTH:v_lo] = rope(kv[:, :SWA_KV_WIDTH]).astype(BF16)
    qkv_ref[:, v_lo:v_lo + SWA_KV_WIDTH] = kv[:, SWA_KV_WIDTH:].astype(BF16)

    ga_lo = v_lo + SWA_KV_WIDTH
    rest_ref[:, 0:SWA_WIDTH] = _silu(proj(ga_lo, ga_lo + SWA_WIDTH))
    glu_lo = ga_lo + SWA_WIDTH
    glu_a = proj(glu_lo, glu_lo + CONV_CHANNELS)
    glu_b = proj(glu_lo + CONV_CHANNELS, glu_lo + 2 * CONV_CHANNELS)
    rest_ref[:, SWA_WIDTH:SWA_WIDTH + CONV_CHANNELS] = glu_a * jax.nn.sigmoid(glu_b)
    gb_lo = glu_lo + 2 * CONV_CHANNELS
    rest_ref[:, SWA_WIDTH + CONV_CHANNELS:] = _silu(proj(gb_lo, gb_lo + CONV_CHANNELS))


def _ab_in(h_flat, gain, w_bf, cos_t, sin_t):
    rows = h_flat.shape[0]
    qkv_w = SWA_WIDTH + 2 * SWA_KV_WIDTH
    rest_w = SWA_WIDTH + 2 * CONV_CHANNELS
    return pl.pallas_call(
        _ab_in_kernel,
        grid=(rows // ROW_TILE,),
        in_specs=[
            pl.BlockSpec((ROW_TILE, D_MODEL), lambda r: (r, 0)),
            pl.BlockSpec((1, D_MODEL), lambda r: (0, 0)),
            pl.BlockSpec((D_MODEL, AB_IN), lambda r: (0, 0)),
            pl.BlockSpec((ROW_TILE, LANES), lambda r: (r, 0)),
            pl.BlockSpec((ROW_TILE, LANES), lambda r: (r, 0)),
        ],
        out_specs=[
            pl.BlockSpec((ROW_TILE, qkv_w), lambda r: (r, 0)),
            pl.BlockSpec((ROW_TILE, rest_w), lambda r: (r, 0)),
        ],
        out_shape=[
            jax.ShapeDtypeStruct((rows, qkv_w), BF16),
            jax.ShapeDtypeStruct((rows, rest_w), F32),
        ],
        compiler_params=_params("parallel"),
        name="ab_in_proj",
    )(h_flat, gain, w_bf, cos_t, sin_t)


def _ab_mix_kernel(sinks_ref, qkv_ref, kv_prev_ref, kv_meta_ref, rest_ref, u_prev_ref,
                   conv_w_ref, conv_b_ref, ln_g_ref, ln_b_ref, w_pw2_ref,
                   out_ref, ext_ref, rot_ref, s_ref, p_ref):
    n_blocks = pl.num_programs(1)
    i = pl.program_id(1)
    ref_blk = jnp.where(i == n_blocks - 1, 0, i + 1)
    qkv = qkv_ref[0]
    kv_prev = kv_prev_ref[0]
    kv_meta = kv_meta_ref[0]

    k_cat = jnp.concatenate([kv_meta[:, :LANES], kv_prev[:, :LANES], qkv[:, SWA_WIDTH:SWA_WIDTH + LANES]], axis=0)
    v_cat = jnp.concatenate([kv_meta[:, LANES:], kv_prev[:, LANES:], qkv[:, SWA_WIDTH + LANES:]], axis=0)
    k_swap = jnp.concatenate([k_cat[:, HEAD_DIM:], k_cat[:, :HEAD_DIM]], axis=1)
    v_swap = jnp.concatenate([v_cat[:, HEAD_DIM:], v_cat[:, :HEAD_DIM]], axis=1)

    row = lax.broadcasted_iota(jnp.int32, (BLOCK, BLOCK), 0)
    col = lax.broadcasted_iota(jnp.int32, (BLOCK, BLOCK), 1)
    qpos = ref_blk * BLOCK + row
    meta_ok = (col >= META_PAD) & (qpos - col >= SWA_WINDOW)
    ppos = (ref_blk - 1) * BLOCK + col
    prev_ok = (ppos >= META_PAD) & (qpos - ppos < SWA_WINDOW)
    cpos = ref_blk * BLOCK + col
    cur_ok = (cpos >= META_PAD) & (qpos >= cpos)
    mask = jnp.concatenate([meta_ok, prev_ok, cur_ok], axis=1)

    lane = lax.broadcasted_iota(jnp.int32, (BLOCK, LANES), 1)
    low_half = lane < HEAD_DIM
    zero_bf = jnp.zeros((), BF16)

    for head in range(SWA_HEADS):
        parity = head % 2
        q_pair = qkv[:, (head // 2) * LANES:(head // 2 + 1) * LANES]
        q_h = jnp.where(low_half if parity == 0 else ~low_half, q_pair, zero_bf)
        k_use = k_cat if head // SWA_GROUP == parity else k_swap
        s_ref[head] = lax.dot_general(q_h, k_use, (((1,), (1,)), ((), ())), preferred_element_type=F32)
    inv_denoms = []
    for head in range(SWA_HEADS):
        s = jnp.where(mask, s_ref[head], NEG_INF)
        sink = sinks_ref[head]
        m = jnp.maximum(jnp.max(s, axis=-1, keepdims=True), sink)
        p = jnp.exp(s - m)
        inv_denoms.append(1.0 / (jnp.sum(p, axis=-1, keepdims=True) + jnp.exp(sink - m)))
        p_ref[head] = p.astype(BF16)
    rest = rest_ref[0]
    for pair in range(SWA_HEADS // 2):
        outs = []
        for parity in range(2):
            head = 2 * pair + parity
            v_use = v_cat if head // SWA_GROUP == parity else v_swap
            outs.append(jnp.dot(p_ref[head], v_use, preferred_element_type=F32) * inv_denoms[head])
        cols = slice(pair * LANES, (pair + 1) * LANES)
        out_ref[0, :, cols] = (jnp.where(low_half, outs[0], outs[1]) * rest[:, cols]).astype(BF16)

    ext_ref[0:BLOCK, :] = jnp.where(ref_blk > 0, u_prev_ref[0], 0.0)
    ext_ref[BLOCK:2 * BLOCK, :] = rest[:, SWA_WIDTH:SWA_WIDTH + CONV_CHANNELS]
    window = ext_ref[CONV_WIN_START:, :]
    for b in range(1, SUBLANES):
        rot_ref[b - 1] = pltpu.roll(window, CONV_WIN_ROWS - b, 0)
    conv_groups = []
    for cg in range(CONV_CHANNELS // LANES):
        cols = slice(cg * LANES, (cg + 1) * LANES)
        acc = jnp.zeros((BLOCK, LANES), F32) + conv_b_ref[:, cols]
        for w in range(CONV_WIDTH):
            a, b = divmod(CONV_LEAD + w, SUBLANES)
            if b == 0:
                taps = ext_ref[CONV_WIN_START + SUBLANES * a:CONV_WIN_START + SUBLANES * a + BLOCK, cols]
            else:
                taps = rot_ref[b - 1, SUBLANES * a:SUBLANES * a + BLOCK, cols]
            acc = acc + taps * conv_w_ref[w:w + 1, cols]
        conv_groups.append(acc)
    conv = jnp.concatenate(conv_groups, axis=1)

    mu = jnp.mean(conv, axis=-1, keepdims=True)
    xc = conv - mu
    ln = xc * lax.rsqrt(jnp.mean(xc * xc, axis=-1, keepdims=True) + CONV_LN_EPS) * ln_g_ref[...] + ln_b_ref[...]
    c_act = _silu(ln).astype(BF16)
    c_branch = jnp.dot(c_act, w_pw2_ref[...], preferred_element_type=F32) * rest[:, SWA_WIDTH + CONV_CHANNELS:]
    out_ref[0, :, SWA_WIDTH:] = c_branch.astype(BF16)


def _ab_mix(sinks, qkv, rest, conv_w, conv_b, ln_g, ln_b, w_pw2):
    batch, lp, qkv_w = qkv.shape
    nb = lp // BLOCK
    meta_blk = nb - 1
    rest_w = rest.shape[-1]
    kv_blk = (SWA_WIDTH // (2 * SWA_KV_WIDTH))
    prev_blk = lambda i: jnp.where(i == 0, meta_blk, i - 1)
    const2 = lambda b, i: (0, 0)
    return pl.pallas_call(
        _ab_mix_kernel,
        grid=(batch, nb),
        in_specs=[
            pl.BlockSpec(memory_space=pltpu.SMEM),
            pl.BlockSpec((1, BLOCK, qkv_w), lambda b, i: (b, i, 0)),
            pl.BlockSpec((1, BLOCK, 2 * SWA_KV_WIDTH), lambda b, i: (b, prev_blk(i), kv_blk)),
            pl.BlockSpec((1, BLOCK, 2 * SWA_KV_WIDTH), lambda b, i: (b, meta_blk, kv_blk)),
            pl.BlockSpec((1, BLOCK, rest_w), lambda b, i: (b, i, 0)),
            pl.BlockSpec((1, BLOCK, CONV_CHANNELS), lambda b, i: (b, prev_blk(i), 1)),
            pl.BlockSpec((CONV_WIDTH + 1, CONV_CHANNELS), const2),
            pl.BlockSpec((1, CONV_CHANNELS), const2),
            pl.BlockSpec((1, CONV_CHANNELS), const2),
            pl.BlockSpec((1, CONV_CHANNELS), const2),
            pl.BlockSpec((CONV_CHANNELS, CONV_CHANNELS), const2),
        ],
        out_specs=pl.BlockSpec((1, BLOCK, AB_MIX), lambda b, i: (b, i, 0)),
        out_shape=jax.ShapeDtypeStruct((batch, lp, AB_MIX), BF16),
        scratch_shapes=[
            pltpu.VMEM((2 * BLOCK, CONV_CHANNELS), F32),
            pltpu.VMEM((SUBLANES - 1, CONV_WIN_ROWS, CONV_CHANNELS), F32),
            pltpu.VMEM((SWA_HEADS, BLOCK, 3 * BLOCK), F32),
            pltpu.VMEM((SWA_HEADS, BLOCK, 3 * BLOCK), BF16),
        ],
        compiler_params=_params("parallel", "parallel"),
        name="ab_mix",
    )(sinks, qkv, qkv, qkv, rest, rest, conv_w, conv_b, ln_g, ln_b, w_pw2)


def _mid_kernel(mix_ref, h_ref, w_out_ref, post_ref, gain_ref, w_in_ref, h1_ref, qkv_ref, gate_ref):
    y = jnp.dot(mix_ref[...], w_out_ref[...], preferred_element_type=F32)
    h1 = h_ref[...] + _rms_normed(y, post_ref[...])
    h1_ref[...] = h1
    hn = _rms_normed(h1, gain_ref[...]).astype(BF16)
    chunk = 4 * LANES
    even_head = (lax.broadcasted_iota(jnp.int32, (ROW_TILE, chunk), 1) % LANES) < HEAD_DIM

    def proj(lo):
        return jnp.dot(hn, w_in_ref[:, lo:lo + chunk], preferred_element_type=F32)

    for c in range(SB_WIDTH // chunk):
        lo = c * chunk
        qkv_ref[:, lo:lo + chunk] = (proj(lo) * HEAD_DIM ** -0.5).astype(BF16)
        qkv_ref[:, SB_WIDTH + lo:SB_WIDTH + lo + chunk] = proj(SB_WIDTH + lo).astype(BF16)
        v = proj(2 * SB_WIDTH + lo)
        qkv_ref[:, 2 * SB_WIDTH + lo:2 * SB_WIDTH + lo + chunk] = jnp.where(even_head, v, 0.0).astype(BF16)
        qkv_ref[:, 3 * SB_WIDTH + lo:3 * SB_WIDTH + lo + chunk] = jnp.where(even_head, 0.0, v).astype(BF16)
        gate_ref[:, lo:lo + chunk] = _silu(proj(3 * SB_WIDTH + lo))


def _mid(mix_flat, h_flat, w_out_bf, post, gain, w_in_bf):
    rows = h_flat.shape[0]
    row_spec = lambda width: pl.BlockSpec((ROW_TILE, width), lambda r: (r, 0))
    const = lambda shape: pl.BlockSpec(shape, lambda r: (0, 0))
    return pl.pallas_call(
        _mid_kernel,
        grid=(rows // ROW_TILE,),
        in_specs=[
            row_spec(AB_MIX),
            row_spec(D_MODEL),
            const((AB_MIX, D_MODEL)),
            const((1, D_MODEL)),
            const((1, D_MODEL)),
            const((D_MODEL, SB_IN)),
        ],
        out_specs=[row_spec(D_MODEL), row_spec(4 * SB_WIDTH), row_spec(SB_WIDTH)],
        out_shape=[
            jax.ShapeDtypeStruct((rows, D_MODEL), F32),
            jax.ShapeDtypeStruct((rows, 4 * SB_WIDTH), BF16),
            jax.ShapeDtypeStruct((rows, SB_WIDTH), F32),
        ],
        compiler_params=_params("parallel"),
        name="ab_out_sb_in_proj",
    )(mix_flat, h_flat, w_out_bf, post, gain, w_in_bf)


def _sb_attn_kernel(q_ref, *refs):
    def one_query_block(sub, _):
        q_blk = pl.program_id(1) * SB_Q_PER_STEP + sub + 1
        _sb_query_block(q_blk, pl.ds(pl.multiple_of(sub * BLOCK, BLOCK), BLOCK), q_ref, *refs)
        return 0

    lax.fori_loop(0, SB_Q_PER_STEP, one_query_block, 0)


def _sb_query_block(q_blk, q_rows, q_ref, k_ref, v_even_ref, v_odd_ref, gate_ref, tri_ref, out_ref,
                    carry_ref, acc_ref, qh_ref, z_ref, lb_ref, split_ref, sums_ref, w_ref, bias_ref):
    meta_off = k_ref.shape[1] - BLOCK
    lane = lax.broadcasted_iota(jnp.int32, (BLOCK, LANES), 1)
    row = lax.broadcasted_iota(jnp.int32, (BLOCK, LANES), 0)
    low_half = lane < HEAD_DIM
    zero_bf = jnp.zeros((), BF16)

    qpos = q_blk * BLOCK + row
    carry_ref[...] = jnp.zeros_like(carry_ref)
    acc_ref[...] = jnp.zeros_like(acc_ref)
    for n in range(2 * SB_PAIRS):
        q_pair = q_ref[0, q_rows, (n // 2) * LANES:(n // 2 + 1) * LANES]
        qh_ref[n] = jnp.where(low_half, q_pair, zero_bf) if n % 2 == 0 else jnp.where(low_half, zero_bf, q_pair)

    def row_offset(k_blk):
        return pl.multiple_of(jnp.where(k_blk == 0, meta_off, (k_blk - 1) * BLOCK), BLOCK)

    def visit(k_blk):
        off = row_offset(k_blk)
        kpos = k_blk * BLOCK + lane

        def scores(n):
            k_pair = k_ref[0, pl.ds(off, BLOCK), (n // 2) * LANES:(n // 2 + 1) * LANES]
            z_ref[n] = lax.dot_general(qh_ref[n], k_pair, (((1,), (1,)), ((), ())), preferred_element_type=F32)

        bias_ref[...] = jnp.where((kpos >= META_PAD) & (kpos < qpos), 0.0, NEG_INF)

        def softplus_split(n):
            z = z_ref[n] + bias_ref[...]
            sp = jnp.maximum(z, 0.0) + jnp.log(1.0 + jnp.exp2(jnp.abs(z) * -LOG2_E))
            lb_ref[n] = z - sp
            hi = sp.astype(BF16)
            split_ref[n, :, 0:LANES] = hi
            split_ref[n, :, LANES:] = (sp - hi.astype(F32)).astype(BF16)

        def suffix_sums(n):
            sums_ref[n] = jnp.dot(split_ref[n], tri_ref[...], preferred_element_type=F32)

        def carry_update(n, least):
            later_blocks = carry_ref[n]
            new_carry = later_blocks + sums_ref[n, :, LANES:]
            sums_ref[n, :, 0:LANES] = later_blocks + sums_ref[n, :, 0:LANES]
            carry_ref[n] = new_carry
            return new_carry if least is None else jnp.minimum(least, new_carry)

        def weights(n):
            w = jnp.exp(lb_ref[n] - sums_ref[n, :, 0:LANES])
            w_ref[n // 2, :, (n % 2) * LANES:(n % 2 + 1) * LANES] = w.astype(BF16)

        def accumulate(p):
            cols = slice(p * LANES, (p + 1) * LANES)
            v_bd = jnp.concatenate([v_even_ref[0, pl.ds(off, BLOCK), cols], v_odd_ref[0, pl.ds(off, BLOCK), cols]], axis=0)
            acc_ref[p] += jnp.dot(w_ref[p], v_bd, preferred_element_type=F32)

        pairs, heads = range(SB_PAIRS), range(2 * SB_PAIRS)
        for n in heads:
            scores(n)
        for n in heads:
            softplus_split(n)
        for n in heads:
            suffix_sums(n)
        least = None
        for n in heads:
            least = carry_update(n, least)
        least_mass = jnp.min(least)
        for n in heads:
            weights(n)
        for p in pairs:
            accumulate(p)
        return least_mass

    def keep_going(state):
        k_blk, least_mass = state
        return (k_blk >= 0) & (least_mass < SB_MASS_CUTOFF)

    def body(state):
        k_blk, _ = state
        return k_blk - 1, visit(k_blk)

    lax.while_loop(keep_going, body, (q_blk, jnp.float32(0.0)))

    for p in range(SB_PAIRS):
        cols = slice(p * LANES, (p + 1) * LANES)
        out_ref[0, q_rows, cols] = (acc_ref[p] * gate_ref[0, q_rows, cols]).astype(BF16)


def _sb_attn(qkv, gate, tri, n_q):
    batch, lp, _ = qkv.shape
    q_tile = SB_Q_PER_STEP * BLOCK
    return pl.pallas_call(
        _sb_attn_kernel,
        grid=(batch, n_q // SB_Q_PER_STEP),
        in_specs=[
            pl.BlockSpec((1, q_tile, SB_WIDTH), lambda b, i: (b, i, 0)),
            pl.BlockSpec((1, lp, SB_WIDTH), lambda b, i: (b, 0, 1)),
            pl.BlockSpec((1, lp, SB_WIDTH), lambda b, i: (b, 0, 2)),
            pl.BlockSpec((1, lp, SB_WIDTH), lambda b, i: (b, 0, 3)),
            pl.BlockSpec((1, q_tile, SB_WIDTH), lambda b, i: (b, i, 0)),
            pl.BlockSpec((2 * BLOCK, 2 * LANES), lambda b, i: (0, 0)),
        ],
        out_specs=pl.BlockSpec((1, q_tile, SB_WIDTH), lambda b, i: (b, i, 0)),
        out_shape=jax.ShapeDtypeStruct((batch, n_q * BLOCK, SB_WIDTH), BF16),
        scratch_shapes=[
            pltpu.VMEM((2 * SB_PAIRS, BLOCK, LANES), F32),
            pltpu.VMEM((SB_PAIRS, BLOCK, LANES), F32),
            pltpu.VMEM((2 * SB_PAIRS, BLOCK, LANES), BF16),
            pltpu.VMEM((2 * SB_PAIRS, BLOCK, LANES), F32),
            pltpu.VMEM((2 * SB_PAIRS, BLOCK, LANES), F32),
            pltpu.VMEM((2 * SB_PAIRS, BLOCK, 2 * LANES), BF16),
            pltpu.VMEM((2 * SB_PAIRS, BLOCK, 2 * LANES), F32),
            pltpu.VMEM((SB_PAIRS, BLOCK, 2 * LANES), BF16),
            pltpu.VMEM((BLOCK, LANES), F32),
        ],
        compiler_params=_params("parallel", "arbitrary"),
        name="sb_attn",
    )(qkv, qkv, qkv, qkv, gate, tri)


def _sb_out_kernel(o_ref, h_ref, w_ref, post_ref, out_ref):
    y = jnp.dot(o_ref[0], w_ref[...], preferred_element_type=F32)
    out_ref[0] = h_ref[0] + _rms_normed(y, post_ref[...])


def _sb_out(o_gated, h, w_bf, post):
    batch, rows, _ = o_gated.shape
    return pl.pallas_call(
        _sb_out_kernel,
        grid=(batch, rows // ROW_TILE),
        in_specs=[
            pl.BlockSpec((1, ROW_TILE, SB_WIDTH), lambda b, i: (b, i, 0)),
            pl.BlockSpec((1, ROW_TILE, D_MODEL), lambda b, i: (b, i, 0)),
            pl.BlockSpec((SB_WIDTH, D_MODEL), lambda b, i: (0, 0)),
            pl.BlockSpec((1, D_MODEL), lambda b, i: (0, 0)),
        ],
        out_specs=pl.BlockSpec((1, ROW_TILE, D_MODEL), lambda b, i: (b, i, 0)),
        out_shape=jax.ShapeDtypeStruct((batch, rows, D_MODEL), F32),
        compiler_params=_params("parallel", "parallel"),
        name="sb_out_proj",
    )(o_gated, h, w_bf, post)


def _rope_tables(seq, batch):
    half = HEAD_DIM // 2
    pos = jnp.concatenate([N_META + jnp.arange(seq), jnp.zeros((META_PAD,), jnp.int32), jnp.arange(N_META)])
    inv = ROPE_THETA ** (-jnp.arange(half, dtype=jnp.float32) / half)
    ang = pos.astype(jnp.float32)[:, None] * inv[None, :]
    cos = jnp.cos(ang)
    sin = jnp.sin(ang)
    cos_t = jnp.tile(jnp.concatenate([cos, cos], axis=-1), (batch, LANES // HEAD_DIM))
    sin_t = jnp.tile(jnp.concatenate([-sin, sin], axis=-1), (batch, LANES // HEAD_DIM))
    return cos_t, sin_t


def kernel(x, meta_tokens, ab_pre_norm, ab_w_in, ab_sinks, ab_conv_w, ab_conv_b, ab_conv_ln_g, ab_conv_ln_b, ab_w_pw2, ab_w_out, ab_post_norm, sb_pre_norm, sb_w_in, sb_w_out, sb_post_norm):
    batch, seq, _ = x.shape
    lp = seq + BLOCK
    assert seq % ROW_TILE == 0 and (batch * lp) % ROW_TILE == 0

    meta = jnp.broadcast_to(meta_tokens[None].astype(x.dtype), (batch, N_META, D_MODEL))
    h0 = jnp.concatenate([x, jnp.zeros((batch, META_PAD, D_MODEL), x.dtype), meta], axis=1).reshape(batch * lp, D_MODEL)
    cos_t, sin_t = _rope_tables(seq, batch)

    qkv, rest = _ab_in(h0, ab_pre_norm[0][None], ab_w_in[0].astype(BF16), cos_t, sin_t)
    conv_w = jnp.concatenate([ab_conv_w[0], jnp.zeros((1, CONV_CHANNELS), F32)], axis=0)
    mix = _ab_mix(ab_sinks[0], qkv.reshape(batch, lp, -1), rest.reshape(batch, lp, -1),
                  conv_w, ab_conv_b[0][None], ab_conv_ln_g[0][None], ab_conv_ln_b[0][None],
                  ab_w_pw2[0].astype(BF16))

    h1, qkv1, gate1 = _mid(mix.reshape(batch * lp, AB_MIX), h0, ab_w_out[0].astype(BF16), ab_post_norm[0][None],
                           sb_pre_norm[0][None], sb_w_in[0].astype(BF16))

    r = jnp.arange(BLOCK)
    tri = jnp.concatenate([(r[:, None] > r[None, :]).astype(BF16), jnp.ones((BLOCK, LANES), BF16)], axis=1)
    tri = jnp.concatenate([tri, tri], axis=0)
    o_gated = _sb_attn(qkv1.reshape(batch, lp, -1), gate1.reshape(batch, lp, -1), tri, seq // BLOCK)
    return _sb_out(o_gated, h1.reshape(batch, lp, D_MODEL), sb_w_out[0].astype(BF16), sb_post_norm[0][None])
```

```python
import functools

import jax
import jax.numpy as jnp
from jax import lax
from jax.experimental import pallas as pl
from jax.experimental.pallas import tpu as pltpu

D_MODEL = 1024
N_META = 16
BLOCK = 128
META_PAD = BLOCK - N_META
HEAD_DIM = 64
ROPE_THETA = 10000.0
NORM_EPS = 1e-6
NEG_INF = -1e30
SWA_HEADS = 8
SWA_KV_HEADS = 2
SWA_GROUP = SWA_HEADS // SWA_KV_HEADS
SWA_WINDOW = 128
SWA_WIDTH = SWA_HEADS * HEAD_DIM
SWA_KV_WIDTH = SWA_KV_HEADS * HEAD_DIM
CONV_CHANNELS = 512
CONV_WIDTH = 31
CONV_LN_EPS = 1e-5
SB_HEADS = 16
SB_WIDTH = SB_HEADS * HEAD_DIM
AB_IN = 2 * SWA_WIDTH + 2 * SWA_KV_WIDTH + 3 * CONV_CHANNELS
AB_MIX = SWA_WIDTH + CONV_CHANNELS
SB_IN = 4 * SB_WIDTH

LANES = 128
SUBLANES = 8
ROW_TILE = 512
VMEM_LIMIT = 48 * 1024 * 1024
CONV_HALO = -(-(CONV_WIDTH - 1) // SUBLANES) * SUBLANES
CONV_WIN_START = BLOCK - CONV_HALO
CONV_WIN_ROWS = BLOCK + CONV_HALO
CONV_LEAD = CONV_HALO - (CONV_WIDTH - 1)

SB_PAIRS = SB_WIDTH // LANES
SB_Q_PER_STEP = 4
SB_MASS_CUTOFF = 104.0
LOG2_E = 1.4426950408889634

F32 = jnp.float32
BF16 = jnp.bfloat16


def _rms_normed(x, gain):
    return x * lax.rsqrt(jnp.mean(x * x, axis=-1, keepdims=True) + NORM_EPS) * gain


def _silu(x):
    return x * jax.nn.sigmoid(x)


def _params(*semantics):
    return pltpu.CompilerParams(dimension_semantics=semantics, vmem_limit_bytes=VMEM_LIMIT)


def _ab_in_kernel(h_ref, gain_ref, w_ref, cos_ref, sin_ref, qkv_ref, rest_ref):
    hn = _rms_normed(h_ref[...], gain_ref[...]).astype(BF16)

    def proj(lo, hi):
        return jnp.dot(hn, w_ref[:, lo:hi], preferred_element_type=F32)

    cos = cos_ref[...]
    sin = sin_ref[...]
    lane = lax.broadcasted_iota(jnp.int32, cos.shape, 1)
    first_half = (lane % HEAD_DIM) < (HEAD_DIM // 2)

    def rope(t):
        rot = jnp.where(first_half, pltpu.roll(t, LANES - HEAD_DIM // 2, 1), pltpu.roll(t, HEAD_DIM // 2, 1))
        return t * cos + rot * sin

    q = proj(0, SWA_WIDTH)
    for c in range(SWA_WIDTH // LANES):
        cols = slice(c * LANES, (c + 1) * LANES)
        qkv_ref[:, cols] = (rope(q[:, cols]) * HEAD_DIM ** -0.5).astype(BF16)
    v_lo = SWA_WIDTH + SWA_KV_WIDTH
    kv = proj(SWA_WIDTH, v_lo + SWA_KV_WIDTH)
    qkv_ref[:, SWA_WIDTH:v_lo] = rope(kv[:, :SWA_KV_WIDTH]).astype(BF16)
    qkv_ref[:, v_lo:v_lo + SWA_KV_WIDTH] = kv[:, SWA_KV_WIDTH:].astype(BF16)

    ga_lo = v_lo + SWA_KV_WIDTH
    rest_ref[:, 0:SWA_WIDTH] = _silu(proj(ga_lo, ga_lo + SWA_WIDTH))
    glu_lo = ga_lo + SWA_WIDTH
    glu_a = proj(glu_lo, glu_lo + CONV_CHANNELS)
    glu_b = proj(glu_lo + CONV_CHANNELS, glu_lo + 2 * CONV_CHANNELS)
    rest_ref[:, SWA_WIDTH:SWA_WIDTH + CONV_CHANNELS] = glu_a * jax.nn.sigmoid(glu_b)
    gb_lo = glu_lo + 2 * CONV_CHANNELS
    rest_ref[:, SWA_WIDTH + CONV_CHANNELS:] = _silu(proj(gb_lo, gb_lo + CONV_CHANNELS))


def _ab_in(h_flat, gain, w_bf, cos_t, sin_t, row_tile, table_block):
    rows = h_flat.shape[0]
    qkv_w = SWA_WIDTH + 2 * SWA_KV_WIDTH
    rest_w = SWA_WIDTH + 2 * CONV_CHANNELS
    return pl.pallas_call(
        _ab_in_kernel,
        grid=(rows // row_tile,),
        in_specs=[
            pl.BlockSpec((row_tile, D_MODEL), lambda r: (r, 0)),
            pl.BlockSpec((1, D_MODEL), lambda r: (0, 0)),
            pl.BlockSpec((D_MODEL, AB_IN), lambda r: (0, 0)),
            pl.BlockSpec((row_tile, LANES), lambda r: (table_block(r), 0)),
            pl.BlockSpec((row_tile, LANES), lambda r: (table_block(r), 0)),
        ],
        out_specs=[
            pl.BlockSpec((row_tile, qkv_w), lambda r: (r, 0)),
            pl.BlockSpec((row_tile, rest_w), lambda r: (r, 0)),
        ],
        out_shape=[
            jax.ShapeDtypeStruct((rows, qkv_w), BF16),
            jax.ShapeDtypeStruct((rows, rest_w), F32),
        ],
        compiler_params=_params("parallel"),
        name="ab_in_proj",
    )(h_flat, gain, w_bf, cos_t, sin_t)


def _ab_mix_kernel(first_ref_blk, sinks_ref, qkv_ref, kv_prev_ref, kv_meta_ref, rest_ref, u_prev_ref, u_meta_ref,
                   conv_w_ref, conv_b_ref, ln_g_ref, ln_b_ref, w_pw2_ref,
                   out_ref, ext_ref, rot_ref, s_ref, p_ref):
    ref_blk = pl.program_id(1) + first_ref_blk
    prev_is_meta = ref_blk == 1
    qkv = qkv_ref[0]
    kv_meta = kv_meta_ref[0]
    kv_prev = jnp.where(prev_is_meta, kv_meta, kv_prev_ref[0])

    k_cat = jnp.concatenate([kv_meta[:, :LANES], kv_prev[:, :LANES], qkv[:, SWA_WIDTH:SWA_WIDTH + LANES]], axis=0)
    v_cat = jnp.concatenate([kv_meta[:, LANES:], kv_prev[:, LANES:], qkv[:, SWA_WIDTH + LANES:]], axis=0)
    k_swap = jnp.concatenate([k_cat[:, HEAD_DIM:], k_cat[:, :HEAD_DIM]], axis=1)
    v_swap = jnp.concatenate([v_cat[:, HEAD_DIM:], v_cat[:, :HEAD_DIM]], axis=1)

    row = lax.broadcasted_iota(jnp.int32, (BLOCK, BLOCK), 0)
    col = lax.broadcasted_iota(jnp.int32, (BLOCK, BLOCK), 1)
    qpos = ref_blk * BLOCK + row
    meta_ok = (col >= META_PAD) & (qpos - col >= SWA_WINDOW)
    ppos = (ref_blk - 1) * BLOCK + col
    prev_ok = (ppos >= META_PAD) & (qpos - ppos < SWA_WINDOW)
    cpos = ref_blk * BLOCK + col
    cur_ok = (cpos >= META_PAD) & (qpos >= cpos)
    mask = jnp.concatenate([meta_ok, prev_ok, cur_ok], axis=1)

    lane = lax.broadcasted_iota(jnp.int32, (BLOCK, LANES), 1)
    low_half = lane < HEAD_DIM
    zero_bf = jnp.zeros((), BF16)

    for head in range(SWA_HEADS):
        parity = head % 2
        q_pair = qkv[:, (head // 2) * LANES:(head // 2 + 1) * LANES]
        q_h = jnp.where(low_half if parity == 0 else ~low_half, q_pair, zero_bf)
        k_use = k_cat if head // SWA_GROUP == parity else k_swap
        s_ref[head] = lax.dot_general(q_h, k_use, (((1,), (1,)), ((), ())), preferred_element_type=F32)
    inv_denoms = []
    for head in range(SWA_HEADS):
        s = jnp.where(mask, s_ref[head], NEG_INF)
        sink = sinks_ref[head]
        m = jnp.maximum(jnp.max(s, axis=-1, keepdims=True), sink)
        p = jnp.exp(s - m)
        inv_denoms.append(1.0 / (jnp.sum(p, axis=-1, keepdims=True) + jnp.exp(sink - m)))
        p_ref[head] = p.astype(BF16)
    rest = rest_ref[0]
    for pair in range(SWA_HEADS // 2):
        outs = []
        for parity in range(2):
            head = 2 * pair + parity
            v_use = v_cat if head // SWA_GROUP == parity else v_swap
            outs.append(jnp.dot(p_ref[head], v_use, preferred_element_type=F32) * inv_denoms[head])
        cols = slice(pair * LANES, (pair + 1) * LANES)
        out_ref[0, :, cols] = (jnp.where(low_half, outs[0], outs[1]) * rest[:, cols]).astype(BF16)

    u_prev = jnp.where(prev_is_meta, u_meta_ref[0], u_prev_ref[0])
    ext_ref[0:BLOCK, :] = jnp.where(ref_blk > 0, u_prev, 0.0)
    ext_ref[BLOCK:2 * BLOCK, :] = rest[:, SWA_WIDTH:SWA_WIDTH + CONV_CHANNELS]
    window = ext_ref[CONV_WIN_START:, :]
    for b in range(1, SUBLANES):
        rot_ref[b - 1] = pltpu.roll(window, CONV_WIN_ROWS - b, 0)
    conv_groups = []
    for cg in range(CONV_CHANNELS // LANES):
        cols = slice(cg * LANES, (cg + 1) * LANES)
        acc = jnp.zeros((BLOCK, LANES), F32) + conv_b_ref[:, cols]
        for w in range(CONV_WIDTH):
            a, b = divmod(CONV_LEAD + w, SUBLANES)
            if b == 0:
                taps = ext_ref[CONV_WIN_START + SUBLANES * a:CONV_WIN_START + SUBLANES * a + BLOCK, cols]
            else:
                taps = rot_ref[b - 1, SUBLANES * a:SUBLANES * a + BLOCK, cols]
            acc = acc + taps * conv_w_ref[w:w + 1, cols]
        conv_groups.append(acc)
    conv = jnp.concatenate(conv_groups, axis=1)

    mu = jnp.mean(conv, axis=-1, keepdims=True)
    xc = conv - mu
    ln = xc * lax.rsqrt(jnp.mean(xc * xc, axis=-1, keepdims=True) + CONV_LN_EPS) * ln_g_ref[...] + ln_b_ref[...]
    c_act = _silu(ln).astype(BF16)
    c_branch = jnp.dot(c_act, w_pw2_ref[...], preferred_element_type=F32) * rest[:, SWA_WIDTH + CONV_CHANNELS:]
    out_ref[0, :, SWA_WIDTH:] = c_branch.astype(BF16)


def _ab_mix(sinks, qkv, rest, qkv_meta, rest_meta, conv_w, conv_b, ln_g, ln_b, w_pw2, first_ref_blk):
    batch, rows, qkv_w = qkv.shape
    rest_w = rest.shape[-1]
    kv_blk = (SWA_WIDTH // (2 * SWA_KV_WIDTH))
    prev_blk = lambda i: jnp.maximum(i - 1, 0)
    const2 = lambda b, i: (0, 0)
    return pl.pallas_call(
        functools.partial(_ab_mix_kernel, first_ref_blk),
        grid=(batch, rows // BLOCK),
        in_specs=[
            pl.BlockSpec(memory_space=pltpu.SMEM),
            pl.BlockSpec((1, BLOCK, qkv_w), lambda b, i: (b, i, 0)),
            pl.BlockSpec((1, BLOCK, 2 * SWA_KV_WIDTH), lambda b, i: (b, prev_blk(i), kv_blk)),
            pl.BlockSpec((1, BLOCK, 2 * SWA_KV_WIDTH), lambda b, i: (0, 0, kv_blk)),
            pl.BlockSpec((1, BLOCK, rest_w), lambda b, i: (b, i, 0)),
            pl.BlockSpec((1, BLOCK, CONV_CHANNELS), lambda b, i: (b, prev_blk(i), 1)),
            pl.BlockSpec((1, BLOCK, CONV_CHANNELS), lambda b, i: (0, 0, 1)),
            pl.BlockSpec((CONV_WIDTH + 1, CONV_CHANNELS), const2),
            pl.BlockSpec((1, CONV_CHANNELS), const2),
            pl.BlockSpec((1, CONV_CHANNELS), const2),
            pl.BlockSpec((1, CONV_CHANNELS), const2),
            pl.BlockSpec((CONV_CHANNELS, CONV_CHANNELS), const2),
        ],
        out_specs=pl.BlockSpec((1, BLOCK, AB_MIX), lambda b, i: (b, i, 0)),
        out_shape=jax.ShapeDtypeStruct((batch, rows, AB_MIX), BF16),
        scratch_shapes=[
            pltpu.VMEM((2 * BLOCK, CONV_CHANNELS), F32),
            pltpu.VMEM((SUBLANES - 1, CONV_WIN_ROWS, CONV_CHANNELS), F32),
            pltpu.VMEM((SWA_HEADS, BLOCK, 3 * BLOCK), F32),
            pltpu.VMEM((SWA_HEADS, BLOCK, 3 * BLOCK), BF16),
        ],
        compiler_params=_params("parallel", "parallel"),
        name="ab_mix",
    )(sinks, qkv, qkv, qkv_meta, rest, rest, rest_meta, conv_w, conv_b, ln_g, ln_b, w_pw2)


def _mid_kernel(mix_ref, h_ref, w_out_ref, post_ref, gain_ref, w_in_ref, h1_ref, qkv_ref, gate_ref):
    y = jnp.dot(mix_ref[...], w_out_ref[...], preferred_element_type=F32)
    h1 = h_ref[...] + _rms_normed(y, post_ref[...])
    h1_ref[...] = h1
    hn = _rms_normed(h1, gain_ref[...]).astype(BF16)
    chunk = 4 * LANES
    even_head = (lax.broadcasted_iota(jnp.int32, (hn.shape[0], chunk), 1) % LANES) < HEAD_DIM

    def proj(lo):
        return jnp.dot(hn, w_in_ref[:, lo:lo + chunk], preferred_element_type=F32)

    for c in range(SB_WIDTH // chunk):
        lo = c * chunk
        qkv_ref[0, :, lo:lo + chunk] = (proj(lo) * HEAD_DIM ** -0.5).astype(BF16)
        qkv_ref[0, :, SB_WIDTH + lo:SB_WIDTH + lo + chunk] = proj(SB_WIDTH + lo).astype(BF16)
        v = proj(2 * SB_WIDTH + lo)
        qkv_ref[0, :, 2 * SB_WIDTH + lo:2 * SB_WIDTH + lo + chunk] = jnp.where(even_head, v, 0.0).astype(BF16)
        qkv_ref[0, :, 3 * SB_WIDTH + lo:3 * SB_WIDTH + lo + chunk] = jnp.where(even_head, 0.0, v).astype(BF16)
        gate_ref[:, lo:lo + chunk] = _silu(proj(3 * SB_WIDTH + lo))


def _mid(mix_flat, h_flat, w_out_bf, post, gain, w_in_bf, row_tile, qkv_rows, tiles_per_seq):
    rows = h_flat.shape[0]
    row_spec = lambda width: pl.BlockSpec((row_tile, width), lambda r: (r, 0))
    const = lambda shape: pl.BlockSpec(shape, lambda r: (0, 0))
    qkv_spec = pl.BlockSpec((1, row_tile, 4 * SB_WIDTH), lambda r: (r // tiles_per_seq, r % tiles_per_seq, 0))
    return pl.pallas_call(
        _mid_kernel,
        grid=(rows // row_tile,),
        in_specs=[
            row_spec(AB_MIX),
            row_spec(D_MODEL),
            const((AB_MIX, D_MODEL)),
            const((1, D_MODEL)),
            const((1, D_MODEL)),
            const((D_MODEL, SB_IN)),
        ],
        out_specs=[row_spec(D_MODEL), qkv_spec, row_spec(SB_WIDTH)],
        out_shape=[
            jax.ShapeDtypeStruct((rows, D_MODEL), F32),
            jax.ShapeDtypeStruct((rows // (tiles_per_seq * row_tile), qkv_rows, 4 * SB_WIDTH), BF16),
            jax.ShapeDtypeStruct((rows, SB_WIDTH), F32),
        ],
        compiler_params=_params("parallel"),
        name="ab_out_sb_in_proj",
    )(mix_flat, h_flat, w_out_bf, post, gain, w_in_bf)


def _meta_fill_kernel(qkv_hbm_ref, meta_ref, out_ref):
    del qkv_hbm_ref
    out_ref[...] = meta_ref[...]


def _meta_fill(qkv, qkv_meta):
    batch, lp, width = qkv.shape
    return pl.pallas_call(
        _meta_fill_kernel,
        grid=(batch,),
        in_specs=[
            pl.BlockSpec(memory_space=pl.ANY),
            pl.BlockSpec((1, BLOCK, width), lambda b: (0, 0, 0)),
        ],
        out_specs=pl.BlockSpec((1, BLOCK, width), lambda b: (b, lp // BLOCK - 1, 0)),
        out_shape=jax.ShapeDtypeStruct(qkv.shape, qkv.dtype),
        input_output_aliases={0: 0},
        compiler_params=_params("arbitrary"),
        name="sb_meta_kv_fill",
    )(qkv, qkv_meta)


def _sb_attn_kernel(q_ref, *refs):
    def one_query_block(sub, _):
        q_blk = pl.program_id(1) * SB_Q_PER_STEP + sub + 1
        _sb_query_block(q_blk, pl.ds(pl.multiple_of(sub * BLOCK, BLOCK), BLOCK), q_ref, *refs)
        return 0

    lax.fori_loop(0, SB_Q_PER_STEP, one_query_block, 0)


def _sb_query_block(q_blk, q_rows, q_ref, k_ref, v_even_ref, v_odd_ref, gate_ref, tri_ref, out_ref,
                    carry_ref, acc_ref, qh_ref, z_ref, lb_ref, split_ref, sums_ref, w_ref, bias_ref):
    meta_off = k_ref.shape[1] - BLOCK
    lane = lax.broadcasted_iota(jnp.int32, (BLOCK, LANES), 1)
    row = lax.broadcasted_iota(jnp.int32, (BLOCK, LANES), 0)
    low_half = lane < HEAD_DIM
    zero_bf = jnp.zeros((), BF16)

    qpos = q_blk * BLOCK + row
    carry_ref[...] = jnp.zeros_like(carry_ref)
    acc_ref[...] = jnp.zeros_like(acc_ref)
    for n in range(2 * SB_PAIRS):
        q_pair = q_ref[0, q_rows, (n // 2) * LANES:(n // 2 + 1) * LANES]
        qh_ref[n] = jnp.where(low_half, q_pair, zero_bf) if n % 2 == 0 else jnp.where(low_half, zero_bf, q_pair)

    def row_offset(k_blk):
        return pl.multiple_of(jnp.where(k_blk == 0, meta_off, (k_blk - 1) * BLOCK), BLOCK)

    def visit(k_blk):
        off = row_offset(k_blk)
        kpos = k_blk * BLOCK + lane

        def scores(n):
            k_pair = k_ref[0, pl.ds(off, BLOCK), (n // 2) * LANES:(n // 2 + 1) * LANES]
            z_ref[n] = lax.dot_general(qh_ref[n], k_pair, (((1,), (1,)), ((), ())), preferred_element_type=F32)

        bias_ref[...] = jnp.where((kpos >= META_PAD) & (kpos < qpos), 0.0, NEG_INF)

        def softplus_split(n):
            z = z_ref[n] + bias_ref[...]
            sp = jnp.maximum(z, 0.0) + jnp.log(1.0 + jnp.exp2(jnp.abs(z) * -LOG2_E))
            lb_ref[n] = z - sp
            hi = sp.astype(BF16)
            split_ref[n, :, 0:LANES] = hi
            split_ref[n, :, LANES:] = (sp - hi.astype(F32)).astype(BF16)

        def suffix_sums(n):
            sums_ref[n] = jnp.dot(split_ref[n], tri_ref[...], preferred_element_type=F32)

        def carry_update(n, least):
            later_blocks = carry_ref[n]
            new_carry = later_blocks + sums_ref[n, :, LANES:]
            sums_ref[n, :, 0:LANES] = later_blocks + sums_ref[n, :, 0:LANES]
            carry_ref[n] = new_carry
            return new_carry if least is None else jnp.minimum(least, new_carry)

        def weights(n):
            w = jnp.exp(lb_ref[n] - sums_ref[n, :, 0:LANES])
            w_ref[n // 2, :, (n % 2) * LANES:(n % 2 + 1) * LANES] = w.astype(BF16)

        def accumulate(p):
            cols = slice(p * LANES, (p + 1) * LANES)
            v_bd = jnp.concatenate([v_even_ref[0, pl.ds(off, BLOCK), cols], v_odd_ref[0, pl.ds(off, BLOCK), cols]], axis=0)
            acc_ref[p] += jnp.dot(w_ref[p], v_bd, preferred_element_type=F32)

        pairs, heads = range(SB_PAIRS), range(2 * SB_PAIRS)
        for n in heads:
            scores(n)
        for n in heads:
            softplus_split(n)
        for n in heads:
            suffix_sums(n)
        least = None
        for n in heads:
            least = carry_update(n, least)
        least_mass = jnp.min(least)
        for n in heads:
            weights(n)
        for p in pairs:
            accumulate(p)
        return least_mass

    def keep_going(state):
        k_blk, least_mass = state
        return (k_blk >= 0) & (least_mass < SB_MASS_CUTOFF)

    def body(state):
        k_blk, _ = state
        return k_blk - 1, visit(k_blk)

    lax.while_loop(keep_going, body, (q_blk, jnp.float32(0.0)))

    for p in range(SB_PAIRS):
        cols = slice(p * LANES, (p + 1) * LANES)
        out_ref[0, q_rows, cols] = (acc_ref[p] * gate_ref[0, q_rows, cols]).astype(BF16)


def _sb_attn(qkv, gate, tri, n_q):
    batch, lp, _ = qkv.shape
    q_tile = SB_Q_PER_STEP * BLOCK
    return pl.pallas_call(
        _sb_attn_kernel,
        grid=(batch, n_q // SB_Q_PER_STEP),
        in_specs=[
            pl.BlockSpec((1, q_tile, SB_WIDTH), lambda b, i: (b, i, 0)),
            pl.BlockSpec((1, lp, SB_WIDTH), lambda b, i: (b, 0, 1)),
            pl.BlockSpec((1, lp, SB_WIDTH), lambda b, i: (b, 0, 2)),
            pl.BlockSpec((1, lp, SB_WIDTH), lambda b, i: (b, 0, 3)),
            pl.BlockSpec((1, q_tile, SB_WIDTH), lambda b, i: (b, i, 0)),
            pl.BlockSpec((2 * BLOCK, 2 * LANES), lambda b, i: (0, 0)),
        ],
        out_specs=pl.BlockSpec((1, q_tile, SB_WIDTH), lambda b, i: (b, i, 0)),
        out_shape=jax.ShapeDtypeStruct((batch, n_q * BLOCK, SB_WIDTH), BF16),
        scratch_shapes=[
            pltpu.VMEM((2 * SB_PAIRS, BLOCK, LANES), F32),
            pltpu.VMEM((SB_PAIRS, BLOCK, LANES), F32),
            pltpu.VMEM((2 * SB_PAIRS, BLOCK, LANES), BF16),
            pltpu.VMEM((2 * SB_PAIRS, BLOCK, LANES), F32),
            pltpu.VMEM((2 * SB_PAIRS, BLOCK, LANES), F32),
            pltpu.VMEM((2 * SB_PAIRS, BLOCK, 2 * LANES), BF16),
            pltpu.VMEM((2 * SB_PAIRS, BLOCK, 2 * LANES), F32),
            pltpu.VMEM((SB_PAIRS, BLOCK, 2 * LANES), BF16),
            pltpu.VMEM((BLOCK, LANES), F32),
        ],
        compiler_params=_params("parallel", "arbitrary"),
        name="sb_attn",
    )(qkv, qkv, qkv, qkv, gate, tri)


def _sb_out_kernel(o_ref, h_ref, w_ref, post_ref, out_ref):
    y = jnp.dot(o_ref[0], w_ref[...], preferred_element_type=F32)
    out_ref[0] = h_ref[0] + _rms_normed(y, post_ref[...])


def _sb_out(o_gated, h, w_bf, post):
    batch, rows, _ = o_gated.shape
    return pl.pallas_call(
        _sb_out_kernel,
        grid=(batch, rows // ROW_TILE),
        in_specs=[
            pl.BlockSpec((1, ROW_TILE, SB_WIDTH), lambda b, i: (b, i, 0)),
            pl.BlockSpec((1, ROW_TILE, D_MODEL), lambda b, i: (b, i, 0)),
            pl.BlockSpec((SB_WIDTH, D_MODEL), lambda b, i: (0, 0)),
            pl.BlockSpec((1, D_MODEL), lambda b, i: (0, 0)),
        ],
        out_specs=pl.BlockSpec((1, ROW_TILE, D_MODEL), lambda b, i: (b, i, 0)),
        out_shape=jax.ShapeDtypeStruct((batch, rows, D_MODEL), F32),
        compiler_params=_params("parallel", "parallel"),
        name="sb_out_proj",
    )(o_gated, h, w_bf, post)


def _rope_tables(seq):
    half = HEAD_DIM // 2
    pos = jnp.concatenate([N_META + jnp.arange(seq), jnp.zeros((META_PAD,), jnp.int32), jnp.arange(N_META)])
    inv = ROPE_THETA ** (-jnp.arange(half, dtype=jnp.float32) / half)
    ang = pos.astype(jnp.float32)[:, None] * inv[None, :]
    cos = jnp.cos(ang)
    sin = jnp.sin(ang)
    cos_t = jnp.tile(jnp.concatenate([cos, cos], axis=-1), (1, LANES // HEAD_DIM))
    sin_t = jnp.tile(jnp.concatenate([-sin, sin], axis=-1), (1, LANES // HEAD_DIM))
    return cos_t, sin_t


def kernel(x, meta_tokens, ab_pre_norm, ab_w_in, ab_sinks, ab_conv_w, ab_conv_b, ab_conv_ln_g, ab_conv_ln_b, ab_w_pw2, ab_w_out, ab_post_norm, sb_pre_norm, sb_w_in, sb_w_out, sb_post_norm):
    batch, seq, _ = x.shape
    assert seq % ROW_TILE == 0
    tiles_per_seq = seq // ROW_TILE
    n_blocks = seq // BLOCK

    x_flat = x.reshape(batch * seq, D_MODEL)
    meta_blk = jnp.concatenate([jnp.zeros((META_PAD, D_MODEL), x.dtype), meta_tokens.astype(x.dtype)], axis=0)
    cos_t, sin_t = _rope_tables(seq)
    pre0, post0, pre1 = ab_pre_norm[0][None], ab_post_norm[0][None], sb_pre_norm[0][None]
    w_in0, w_pw2, w_out0 = ab_w_in[0].astype(BF16), ab_w_pw2[0].astype(BF16), ab_w_out[0].astype(BF16)
    w_in1, w_out1 = sb_w_in[0].astype(BF16), sb_w_out[0].astype(BF16)

    qkv, rest = _ab_in(x_flat, pre0, w_in0, cos_t, sin_t, ROW_TILE, lambda r: r % tiles_per_seq)
    qkv_m, rest_m = _ab_in(meta_blk, pre0, w_in0, cos_t, sin_t, BLOCK, lambda r: n_blocks)
    qkv_m, rest_m = qkv_m[None], rest_m[None]
    conv_w = jnp.concatenate([ab_conv_w[0], jnp.zeros((1, CONV_CHANNELS), F32)], axis=0)
    mix_args = (conv_w, ab_conv_b[0][None], ab_conv_ln_g[0][None], ab_conv_ln_b[0][None], w_pw2)
    mix = _ab_mix(ab_sinks[0], qkv.reshape(batch, seq, -1), rest.reshape(batch, seq, -1), qkv_m, rest_m, *mix_args, 1)
    mix_m = _ab_mix(ab_sinks[0], qkv_m, rest_m, qkv_m, rest_m, *mix_args, 0)

    mid_args = (w_out0, post0, pre1, w_in1)
    h1, qkv1, gate1 = _mid(mix.reshape(batch * seq, AB_MIX), x_flat, *mid_args, ROW_TILE, seq + BLOCK, tiles_per_seq)
    _, qkv1_m, _ = _mid(mix_m[0], meta_blk, *mid_args, BLOCK, BLOCK, 1)
    qkv1 = _meta_fill(qkv1, qkv1_m)

    r = jnp.arange(BLOCK)
    tri = jnp.concatenate([(r[:, None] > r[None, :]).astype(BF16), jnp.ones((BLOCK, LANES), BF16)], axis=1)
    tri = jnp.concatenate([tri, tri], axis=0)
    o_gated = _sb_attn(qkv1, gate1.reshape(batch, seq, -1), tri, n_blocks)
    return _sb_out(o_gated, h1.reshape(batch, seq, D_MODEL), w_out1, sb_post_norm[0][None])
```

```python
import functools

import jax
import jax.numpy as jnp
from jax import lax
from jax.experimental import pallas as pl
from jax.experimental.pallas import tpu as pltpu

D_MODEL = 1024
N_META = 16
BLOCK = 128
META_PAD = BLOCK - N_META
HEAD_DIM = 64
ROPE_THETA = 10000.0
NORM_EPS = 1e-6
NEG_INF = -1e30
SWA_HEADS = 8
SWA_KV_HEADS = 2
SWA_GROUP = SWA_HEADS // SWA_KV_HEADS
SWA_WINDOW = 128
SWA_WIDTH = SWA_HEADS * HEAD_DIM
SWA_KV_WIDTH = SWA_KV_HEADS * HEAD_DIM
CONV_CHANNELS = 512
CONV_WIDTH = 31
CONV_LN_EPS = 1e-5
SB_HEADS = 16
SB_WIDTH = SB_HEADS * HEAD_DIM
AB_IN = 2 * SWA_WIDTH + 2 * SWA_KV_WIDTH + 3 * CONV_CHANNELS
AB_MIX = SWA_WIDTH + CONV_CHANNELS
SB_IN = 4 * SB_WIDTH

LANES = 128
SUBLANES = 8
ROW_TILE = 512
VMEM_LIMIT = 48 * 1024 * 1024
CONV_HALO = -(-(CONV_WIDTH - 1) // SUBLANES) * SUBLANES
CONV_WIN_START = BLOCK - CONV_HALO
CONV_WIN_ROWS = BLOCK + CONV_HALO
CONV_LEAD = CONV_HALO - (CONV_WIDTH - 1)

SB_PAIRS = SB_WIDTH // LANES
SB_Q_PER_STEP = 4
SB_MASS_CUTOFF = 104.0
LOG2_E = 1.4426950408889634

F32 = jnp.float32
BF16 = jnp.bfloat16


def _rms_normed(x, gain):
    return x * lax.rsqrt(jnp.mean(x * x, axis=-1, keepdims=True) + NORM_EPS) * gain


def _silu(x):
    return x * jax.nn.sigmoid(x)


def _params(*semantics):
    return pltpu.CompilerParams(dimension_semantics=semantics, vmem_limit_bytes=VMEM_LIMIT)


def _ab_in_kernel(h_ref, gain_ref, w_ref, cos_ref, sin_ref, qkv_ref, rest_ref):
    hn = _rms_normed(h_ref[...], gain_ref[...]).astype(BF16)

    def proj(lo, hi):
        return jnp.dot(hn, w_ref[:, lo:hi], preferred_element_type=F32)

    cos = cos_ref[...]
    sin = sin_ref[...]
    lane = lax.broadcasted_iota(jnp.int32, cos.shape, 1)
    first_half = (lane % HEAD_DIM) < (HEAD_DIM // 2)

    def rope(t):
        rot = jnp.where(first_half, pltpu.roll(t, LANES - HEAD_DIM // 2, 1), pltpu.roll(t, HEAD_DIM // 2, 1))
        return t * cos + rot * sin

    q = proj(0, SWA_WIDTH)
    for c in range(SWA_WIDTH // LANES):
        cols = slice(c * LANES, (c + 1) * LANES)
        qkv_ref[:, cols] = (rope(q[:, cols]) * HEAD_DIM ** -0.5).astype(BF16)
    v_lo = SWA_WIDTH + SWA_KV_WIDTH
    kv = proj(SWA_WIDTH, v_lo + SWA_KV_WIDTH)
    qkv_ref[:, SWA_WIDTH:v_lo] = rope(kv[:, :SWA_KV_WIDTH]).astype(BF16)
    qkv_ref[:, v_lo:v_lo + SWA_KV_WIDTH] = kv[:, SWA_KV_WIDTH:].astype(BF16)

    ga_lo = v_lo + SWA_KV_WIDTH
    rest_ref[:, 0:SWA_WIDTH] = _silu(proj(ga_lo, ga_lo + SWA_WIDTH))
    glu_lo = ga_lo + SWA_WIDTH
    glu_a = proj(glu_lo, glu_lo + CONV_CHANNELS)
    glu_b = proj(glu_lo + CONV_CHANNELS, glu_lo + 2 * CONV_CHANNELS)
    rest_ref[:, SWA_WIDTH:SWA_WIDTH + CONV_CHANNELS] = glu_a * jax.nn.sigmoid(glu_b)
    gb_lo = glu_lo + 2 * CONV_CHANNELS
    rest_ref[:, SWA_WIDTH + CONV_CHANNELS:] = _silu(proj(gb_lo, gb_lo + CONV_CHANNELS))


def _ab_in(h_flat, gain, w_bf, cos_t, sin_t, row_tile, table_block):
    rows = h_flat.shape[0]
    qkv_w = SWA_WIDTH + 2 * SWA_KV_WIDTH
    rest_w = SWA_WIDTH + 2 * CONV_CHANNELS
    return pl.pallas_call(
        _ab_in_kernel,
        grid=(rows // row_tile,),
        in_specs=[
            pl.BlockSpec((row_tile, D_MODEL), lambda r: (r, 0)),
            pl.BlockSpec((1, D_MODEL), lambda r: (0, 0)),
            pl.BlockSpec((D_MODEL, AB_IN), lambda r: (0, 0)),
            pl.BlockSpec((row_tile, LANES), lambda r: (table_block(r), 0)),
            pl.BlockSpec((row_tile, LANES), lambda r: (table_block(r), 0)),
        ],
        out_specs=[
            pl.BlockSpec((row_tile, qkv_w), lambda r: (r, 0)),
            pl.BlockSpec((row_tile, rest_w), lambda r: (r, 0)),
        ],
        out_shape=[
            jax.ShapeDtypeStruct((rows, qkv_w), BF16),
            jax.ShapeDtypeStruct((rows, rest_w), F32),
        ],
        compiler_params=_params("parallel"),
        name="ab_in_proj",
    )(h_flat, gain, w_bf, cos_t, sin_t)


def _ab_mix_kernel(first_ref_blk, sinks_ref, qkv_ref, kv_prev_ref, kv_meta_ref, rest_ref, u_prev_ref, u_meta_ref,
                   conv_w_ref, conv_b_ref, ln_g_ref, ln_b_ref, w_pw2_ref,
                   out_ref, ext_ref, rot_ref, s_ref, p_ref):
    ref_blk = pl.program_id(1) + first_ref_blk
    prev_is_meta = ref_blk == 1
    qkv = qkv_ref[0]
    kv_meta = kv_meta_ref[0]
    kv_prev = jnp.where(prev_is_meta, kv_meta, kv_prev_ref[0])

    k_cat = jnp.concatenate([kv_meta[:, :LANES], kv_prev[:, :LANES], qkv[:, SWA_WIDTH:SWA_WIDTH + LANES]], axis=0)
    v_cat = jnp.concatenate([kv_meta[:, LANES:], kv_prev[:, LANES:], qkv[:, SWA_WIDTH + LANES:]], axis=0)
    k_swap = jnp.concatenate([k_cat[:, HEAD_DIM:], k_cat[:, :HEAD_DIM]], axis=1)
    v_swap = jnp.concatenate([v_cat[:, HEAD_DIM:], v_cat[:, :HEAD_DIM]], axis=1)

    row = lax.broadcasted_iota(jnp.int32, (BLOCK, BLOCK), 0)
    col = lax.broadcasted_iota(jnp.int32, (BLOCK, BLOCK), 1)
    qpos = ref_blk * BLOCK + row
    meta_ok = (col >= META_PAD) & (qpos - col >= SWA_WINDOW)
    ppos = (ref_blk - 1) * BLOCK + col
    prev_ok = (ppos >= META_PAD) & (qpos - ppos < SWA_WINDOW)
    cpos = ref_blk * BLOCK + col
    cur_ok = (cpos >= META_PAD) & (qpos >= cpos)
    mask = jnp.concatenate([meta_ok, prev_ok, cur_ok], axis=1)

    lane = lax.broadcasted_iota(jnp.int32, (BLOCK, LANES), 1)
    low_half = lane < HEAD_DIM
    zero_bf = jnp.zeros((), BF16)

    for head in range(SWA_HEADS):
        parity = head % 2
        q_pair = qkv[:, (head // 2) * LANES:(head // 2 + 1) * LANES]
        q_h = jnp.where(low_half if parity == 0 else ~low_half, q_pair, zero_bf)
        k_use = k_cat if head // SWA_GROUP == parity else k_swap
        s_ref[head] = lax.dot_general(q_h, k_use, (((1,), (1,)), ((), ())), preferred_element_type=F32)
    inv_denoms = []
    for head in range(SWA_HEADS):
        s = jnp.where(mask, s_ref[head], NEG_INF)
        sink = sinks_ref[head]
        m = jnp.maximum(jnp.max(s, axis=-1, keepdims=True), sink)
        p = jnp.exp(s - m)
        inv_denoms.append(1.0 / (jnp.sum(p, axis=-1, keepdims=True) + jnp.exp(sink - m)))
        p_ref[head] = p.astype(BF16)
    rest = rest_ref[0]
    for pair in range(SWA_HEADS // 2):
        outs = []
        for parity in range(2):
            head = 2 * pair + parity
            v_use = v_cat if head // SWA_GROUP == parity else v_swap
            outs.append(jnp.dot(p_ref[head], v_use, preferred_element_type=F32) * inv_denoms[head])
        cols = slice(pair * LANES, (pair + 1) * LANES)
        out_ref[0, :, cols] = (jnp.where(low_half, outs[0], outs[1]) * rest[:, cols]).astype(BF16)

    u_prev = jnp.where(prev_is_meta, u_meta_ref[0], u_prev_ref[0])
    ext_ref[0:BLOCK, :] = jnp.where(ref_blk > 0, u_prev, 0.0)
    ext_ref[BLOCK:2 * BLOCK, :] = rest[:, SWA_WIDTH:SWA_WIDTH + CONV_CHANNELS]
    window = ext_ref[CONV_WIN_START:, :]
    for b in range(1, SUBLANES):
        rot_ref[b - 1] = pltpu.roll(window, CONV_WIN_ROWS - b, 0)
    conv_groups = []
    for cg in range(CONV_CHANNELS // LANES):
        cols = slice(cg * LANES, (cg + 1) * LANES)
        acc = jnp.zeros((BLOCK, LANES), F32) + conv_b_ref[:, cols]
        for w in range(CONV_WIDTH):
            a, b = divmod(CONV_LEAD + w, SUBLANES)
            if b == 0:
                taps = ext_ref[CONV_WIN_START + SUBLANES * a:CONV_WIN_START + SUBLANES * a + BLOCK, cols]
            else:
                taps = rot_ref[b - 1, SUBLANES * a:SUBLANES * a + BLOCK, cols]
            acc = acc + taps * conv_w_ref[w:w + 1, cols]
        conv_groups.append(acc)
    conv = jnp.concatenate(conv_groups, axis=1)

    mu = jnp.mean(conv, axis=-1, keepdims=True)
    xc = conv - mu
    ln = xc * lax.rsqrt(jnp.mean(xc * xc, axis=-1, keepdims=True) + CONV_LN_EPS) * ln_g_ref[...] + ln_b_ref[...]
    c_act = _silu(ln).astype(BF16)
    c_branch = jnp.dot(c_act, w_pw2_ref[...], preferred_element_type=F32) * rest[:, SWA_WIDTH + CONV_CHANNELS:]
    out_ref[0, :, SWA_WIDTH:] = c_branch.astype(BF16)


def _ab_mix(sinks, qkv, rest, qkv_meta, rest_meta, conv_w, conv_b, ln_g, ln_b, w_pw2, first_ref_blk):
    batch, rows, qkv_w = qkv.shape
    rest_w = rest.shape[-1]
    kv_blk = (SWA_WIDTH // (2 * SWA_KV_WIDTH))
    prev_blk = lambda i: jnp.maximum(i - 1, 0)
    const2 = lambda b, i: (0, 0)
    return pl.pallas_call(
        functools.partial(_ab_mix_kernel, first_ref_blk),
        grid=(batch, rows // BLOCK),
        in_specs=[
            pl.BlockSpec(memory_space=pltpu.SMEM),
            pl.BlockSpec((1, BLOCK, qkv_w), lambda b, i: (b, i, 0)),
            pl.BlockSpec((1, BLOCK, 2 * SWA_KV_WIDTH), lambda b, i: (b, prev_blk(i), kv_blk)),
            pl.BlockSpec((1, BLOCK, 2 * SWA_KV_WIDTH), lambda b, i: (0, 0, kv_blk)),
            pl.BlockSpec((1, BLOCK, rest_w), lambda b, i: (b, i, 0)),
            pl.BlockSpec((1, BLOCK, CONV_CHANNELS), lambda b, i: (b, prev_blk(i), 1)),
            pl.BlockSpec((1, BLOCK, CONV_CHANNELS), lambda b, i: (0, 0, 1)),
            pl.BlockSpec((CONV_WIDTH + 1, CONV_CHANNELS), const2),
            pl.BlockSpec((1, CONV_CHANNELS), const2),
            pl.BlockSpec((1, CONV_CHANNELS), const2),
            pl.BlockSpec((1, CONV_CHANNELS), const2),
            pl.BlockSpec((CONV_CHANNELS, CONV_CHANNELS), const2),
        ],
        out_specs=pl.BlockSpec((1, BLOCK, AB_MIX), lambda b, i: (b, i, 0)),
        out_shape=jax.ShapeDtypeStruct((batch, rows, AB_MIX), BF16),
        scratch_shapes=[
            pltpu.VMEM((2 * BLOCK, CONV_CHANNELS), F32),
            pltpu.VMEM((SUBLANES - 1, CONV_WIN_ROWS, CONV_CHANNELS), F32),
            pltpu.VMEM((SWA_HEADS, BLOCK, 3 * BLOCK), F32),
            pltpu.VMEM((SWA_HEADS, BLOCK, 3 * BLOCK), BF16),
        ],
        compiler_params=_params("parallel", "parallel"),
        name="ab_mix",
    )(sinks, qkv, qkv, qkv_meta, rest, rest, rest_meta, conv_w, conv_b, ln_g, ln_b, w_pw2)


def _mid_kernel(mix_ref, h_ref, w_out_ref, post_ref, gain_ref, w_in_ref, h1_ref, qkv_ref, gate_ref):
    y = jnp.dot(mix_ref[...], w_out_ref[...], preferred_element_type=F32)
    h1 = h_ref[...] + _rms_normed(y, post_ref[...])
    h1_ref[...] = h1
    hn = _rms_normed(h1, gain_ref[...]).astype(BF16)
    chunk = 4 * LANES
    even_head = (lax.broadcasted_iota(jnp.int32, (hn.shape[0], chunk), 1) % LANES) < HEAD_DIM

    def proj(lo):
        return jnp.dot(hn, w_in_ref[:, lo:lo + chunk], preferred_element_type=F32)

    for c in range(SB_WIDTH // chunk):
        lo = c * chunk
        qkv_ref[0, :, lo:lo + chunk] = (proj(lo) * HEAD_DIM ** -0.5).astype(BF16)
        qkv_ref[0, :, SB_WIDTH + lo:SB_WIDTH + lo + chunk] = proj(SB_WIDTH + lo).astype(BF16)
        v = proj(2 * SB_WIDTH + lo)
        qkv_ref[0, :, 2 * SB_WIDTH + lo:2 * SB_WIDTH + lo + chunk] = jnp.where(even_head, v, 0.0).astype(BF16)
        qkv_ref[0, :, 3 * SB_WIDTH + lo:3 * SB_WIDTH + lo + chunk] = jnp.where(even_head, 0.0, v).astype(BF16)
        gate_ref[:, lo:lo + chunk] = _silu(proj(3 * SB_WIDTH + lo))


def _mid(mix_flat, h_flat, w_out_bf, post, gain, w_in_bf, row_tile, qkv_rows, tiles_per_seq):
    rows = h_flat.shape[0]
    row_spec = lambda width: pl.BlockSpec((row_tile, width), lambda r: (r, 0))
    const = lambda shape: pl.BlockSpec(shape, lambda r: (0, 0))
    qkv_spec = pl.BlockSpec((1, row_tile, 4 * SB_WIDTH), lambda r: (r // tiles_per_seq, r % tiles_per_seq, 0))
    return pl.pallas_call(
        _mid_kernel,
        grid=(rows // row_tile,),
        in_specs=[
            row_spec(AB_MIX),
            row_spec(D_MODEL),
            const((AB_MIX, D_MODEL)),
            const((1, D_MODEL)),
            const((1, D_MODEL)),
            const((D_MODEL, SB_IN)),
        ],
        out_specs=[row_spec(D_MODEL), qkv_spec, row_spec(SB_WIDTH)],
        out_shape=[
            jax.ShapeDtypeStruct((rows, D_MODEL), F32),
            jax.ShapeDtypeStruct((rows // (tiles_per_seq * row_tile), qkv_rows, 4 * SB_WIDTH), BF16),
            jax.ShapeDtypeStruct((rows, SB_WIDTH), F32),
        ],
        compiler_params=_params("parallel"),
        name="ab_out_sb_in_proj",
    )(mix_flat, h_flat, w_out_bf, post, gain, w_in_bf)


def _meta_fill_kernel(qkv_hbm_ref, meta_ref, out_ref):
    del qkv_hbm_ref
    out_ref[...] = meta_ref[...]


def _meta_fill(qkv, qkv_meta):
    batch, lp, width = qkv.shape
    return pl.pallas_call(
        _meta_fill_kernel,
        grid=(batch,),
        in_specs=[
            pl.BlockSpec(memory_space=pl.ANY),
            pl.BlockSpec((1, BLOCK, width), lambda b: (0, 0, 0)),
        ],
        out_specs=pl.BlockSpec((1, BLOCK, width), lambda b: (b, lp // BLOCK - 1, 0)),
        out_shape=jax.ShapeDtypeStruct(qkv.shape, qkv.dtype),
        input_output_aliases={0: 0},
        compiler_params=_params("arbitrary"),
        name="sb_meta_kv_fill",
    )(qkv, qkv_meta)


def _sb_attn_kernel(q_ref, *refs):
    def one_query_block(sub, _):
        q_blk = pl.program_id(1) * SB_Q_PER_STEP + sub + 1
        _sb_query_block(q_blk, pl.ds(pl.multiple_of(sub * BLOCK, BLOCK), BLOCK), q_ref, *refs)
        return 0

    lax.fori_loop(0, SB_Q_PER_STEP, one_query_block, 0)


def _sb_query_block(q_blk, q_rows, q_ref, k_ref, v_even_ref, v_odd_ref, gate_ref, tri_ref, out_ref,
                    carry_ref, acc_ref, qh_ref, z_ref, split_ref, sums_ref, w_ref, bias_ref):
    meta_off = k_ref.shape[1] - BLOCK
    lane = lax.broadcasted_iota(jnp.int32, (BLOCK, LANES), 1)
    row = lax.broadcasted_iota(jnp.int32, (BLOCK, LANES), 0)
    low_half = lane < HEAD_DIM
    zero_bf = jnp.zeros((), BF16)

    qpos = q_blk * BLOCK + row
    carry_ref[...] = jnp.zeros_like(carry_ref)
    acc_ref[...] = jnp.zeros_like(acc_ref)
    for n in range(2 * SB_PAIRS):
        q_pair = q_ref[0, q_rows, (n // 2) * LANES:(n // 2 + 1) * LANES]
        qh_ref[n] = jnp.where(low_half, q_pair, zero_bf) if n % 2 == 0 else jnp.where(low_half, zero_bf, q_pair)

    def row_offset(k_blk):
        return pl.multiple_of(jnp.where(k_blk == 0, meta_off, (k_blk - 1) * BLOCK), BLOCK)

    def visit(k_blk):
        off = row_offset(k_blk)
        kpos = k_blk * BLOCK + lane

        def scores(n):
            k_pair = k_ref[0, pl.ds(off, BLOCK), (n // 2) * LANES:(n // 2 + 1) * LANES]
            z_ref[n] = lax.dot_general(qh_ref[n], k_pair, (((1,), (1,)), ((), ())), preferred_element_type=F32)

        bias_ref[...] = jnp.where((kpos >= META_PAD) & (kpos < qpos), 0.0, NEG_INF)

        def softplus_split(n):
            z = z_ref[n] + bias_ref[...]
            z_ref[n] = z
            sp = jnp.maximum(z, 0.0) + jnp.log(1.0 + jnp.exp2(jnp.abs(z) * -LOG2_E))
            hi = sp.astype(BF16)
            split_ref[n, :, 0:LANES] = hi
            split_ref[n, :, LANES:] = (sp - hi.astype(F32)).astype(BF16)

        def suffix_sums(n):
            sums_ref[n] = jnp.dot(split_ref[n], tri_ref[...], preferred_element_type=F32)

        def carry_update(n, least):
            later_blocks = carry_ref[n]
            new_carry = later_blocks + sums_ref[n, :, LANES:]
            sums_ref[n, :, 0:LANES] = later_blocks + sums_ref[n, :, 0:LANES]
            carry_ref[n] = new_carry
            return new_carry if least is None else jnp.minimum(least, new_carry)

        def weights(n):
            w = jnp.exp(z_ref[n] - sums_ref[n, :, 0:LANES])
            w_ref[n // 2, :, (n % 2) * LANES:(n % 2 + 1) * LANES] = w.astype(BF16)

        def accumulate(p):
            cols = slice(p * LANES, (p + 1) * LANES)
            v_bd = jnp.concatenate([v_even_ref[0, pl.ds(off, BLOCK), cols], v_odd_ref[0, pl.ds(off, BLOCK), cols]], axis=0)
            acc_ref[p] += jnp.dot(w_ref[p], v_bd, preferred_element_type=F32)

        pairs, heads = range(SB_PAIRS), range(2 * SB_PAIRS)
        for n in heads:
            scores(n)
        for n in heads:
            softplus_split(n)
        for n in heads:
            suffix_sums(n)
        least = None
        for n in heads:
            least = carry_update(n, least)
        least_mass = jnp.min(least)
        for n in heads:
            weights(n)
        for p in pairs:
            accumulate(p)
        return least_mass

    def keep_going(state):
        k_blk, least_mass = state
        return (k_blk >= 0) & (least_mass < SB_MASS_CUTOFF)

    def body(state):
        k_blk, _ = state
        return k_blk - 1, visit(k_blk)

    lax.while_loop(keep_going, body, (q_blk, jnp.float32(0.0)))

    for p in range(SB_PAIRS):
        cols = slice(p * LANES, (p + 1) * LANES)
        out_ref[0, q_rows, cols] = (acc_ref[p] * gate_ref[0, q_rows, cols]).astype(BF16)


def _sb_attn(qkv, gate, tri, n_q):
    batch, lp, _ = qkv.shape
    q_tile = SB_Q_PER_STEP * BLOCK
    return pl.pallas_call(
        _sb_attn_kernel,
        grid=(batch, n_q // SB_Q_PER_STEP),
        in_specs=[
            pl.BlockSpec((1, q_tile, SB_WIDTH), lambda b, i: (b, i, 0)),
            pl.BlockSpec((1, lp, SB_WIDTH), lambda b, i: (b, 0, 1)),
            pl.BlockSpec((1, lp, SB_WIDTH), lambda b, i: (b, 0, 2)),
            pl.BlockSpec((1, lp, SB_WIDTH), lambda b, i: (b, 0, 3)),
            pl.BlockSpec((1, q_tile, SB_WIDTH), lambda b, i: (b, i, 0)),
            pl.BlockSpec((2 * BLOCK, 2 * LANES), lambda b, i: (0, 0)),
        ],
        out_specs=pl.BlockSpec((1, q_tile, SB_WIDTH), lambda b, i: (b, i, 0)),
        out_shape=jax.ShapeDtypeStruct((batch, n_q * BLOCK, SB_WIDTH), BF16),
        scratch_shapes=[
            pltpu.VMEM((2 * SB_PAIRS, BLOCK, LANES), F32),
            pltpu.VMEM((SB_PAIRS, BLOCK, LANES), F32),
            pltpu.VMEM((2 * SB_PAIRS, BLOCK, LANES), BF16),
            pltpu.VMEM((2 * SB_PAIRS, BLOCK, LANES), F32),
            pltpu.VMEM((2 * SB_PAIRS, BLOCK, 2 * LANES), BF16),
            pltpu.VMEM((2 * SB_PAIRS, BLOCK, 2 * LANES), F32),
            pltpu.VMEM((SB_PAIRS, BLOCK, 2 * LANES), BF16),
            pltpu.VMEM((BLOCK, LANES), F32),
        ],
        compiler_params=_params("parallel", "arbitrary"),
        name="sb_attn",
    )(qkv, qkv, qkv, qkv, gate, tri)


def _sb_out_kernel(o_ref, h_ref, w_ref, post_ref, out_ref):
    y = jnp.dot(o_ref[0], w_ref[...], preferred_element_type=F32)
    out_ref[0] = h_ref[0] + _rms_normed(y, post_ref[...])


def _sb_out(o_gated, h, w_bf, post):
    batch, rows, _ = o_gated.shape
    return pl.pallas_call(
        _sb_out_kernel,
        grid=(batch, rows // ROW_TILE),
        in_specs=[
            pl.BlockSpec((1, ROW_TILE, SB_WIDTH), lambda b, i: (b, i, 0)),
            pl.BlockSpec((1, ROW_TILE, D_MODEL), lambda b, i: (b, i, 0)),
            pl.BlockSpec((SB_WIDTH, D_MODEL), lambda b, i: (0, 0)),
            pl.BlockSpec((1, D_MODEL), lambda b, i: (0, 0)),
        ],
        out_specs=pl.BlockSpec((1, ROW_TILE, D_MODEL), lambda b, i: (b, i, 0)),
        out_shape=jax.ShapeDtypeStruct((batch, rows, D_MODEL), F32),
        compiler_params=_params("parallel", "parallel"),
        name="sb_out_proj",
    )(o_gated, h, w_bf, post)


def _rope_tables(seq):
    half = HEAD_DIM // 2
    pos = jnp.concatenate([N_META + jnp.arange(seq), jnp.zeros((META_PAD,), jnp.int32), jnp.arange(N_META)])
    inv = ROPE_THETA ** (-jnp.arange(half, dtype=jnp.float32) / half)
    ang = pos.astype(jnp.float32)[:, None] * inv[None, :]
    cos = jnp.cos(ang)
    sin = jnp.sin(ang)
    cos_t = jnp.tile(jnp.concatenate([cos, cos], axis=-1), (1, LANES // HEAD_DIM))
    sin_t = jnp.tile(jnp.concatenate([-sin, sin], axis=-1), (1, LANES // HEAD_DIM))
    return cos_t, sin_t


def kernel(x, meta_tokens, ab_pre_norm, ab_w_in, ab_sinks, ab_conv_w, ab_conv_b, ab_conv_ln_g, ab_conv_ln_b, ab_w_pw2, ab_w_out, ab_post_norm, sb_pre_norm, sb_w_in, sb_w_out, sb_post_norm):
    batch, seq, _ = x.shape
    assert seq % ROW_TILE == 0
    tiles_per_seq = seq // ROW_TILE
    n_blocks = seq // BLOCK

    x_flat = x.reshape(batch * seq, D_MODEL)
    meta_blk = jnp.concatenate([jnp.zeros((META_PAD, D_MODEL), x.dtype), meta_tokens.astype(x.dtype)], axis=0)
    cos_t, sin_t = _rope_tables(seq)
    pre0, post0, pre1 = ab_pre_norm[0][None], ab_post_norm[0][None], sb_pre_norm[0][None]
    w_in0, w_pw2, w_out0 = ab_w_in[0].astype(BF16), ab_w_pw2[0].astype(BF16), ab_w_out[0].astype(BF16)
    w_in1, w_out1 = sb_w_in[0].astype(BF16), sb_w_out[0].astype(BF16)

    qkv, rest = _ab_in(x_flat, pre0, w_in0, cos_t, sin_t, ROW_TILE, lambda r: r % tiles_per_seq)
    qkv_m, rest_m = _ab_in(meta_blk, pre0, w_in0, cos_t, sin_t, BLOCK, lambda r: n_blocks)
    qkv_m, rest_m = qkv_m[None], rest_m[None]
    conv_w = jnp.concatenate([ab_conv_w[0], jnp.zeros((1, CONV_CHANNELS), F32)], axis=0)
    mix_args = (conv_w, ab_conv_b[0][None], ab_conv_ln_g[0][None], ab_conv_ln_b[0][None], w_pw2)
    mix = _ab_mix(ab_sinks[0], qkv.reshape(batch, seq, -1), rest.reshape(batch, seq, -1), qkv_m, rest_m, *mix_args, 1)
    mix_m = _ab_mix(ab_sinks[0], qkv_m, rest_m, qkv_m, rest_m, *mix_args, 0)

    mid_args = (w_out0, post0, pre1, w_in1)
    h1, qkv1, gate1 = _mid(mix.reshape(batch * seq, AB_MIX), x_flat, *mid_args, ROW_TILE, seq + BLOCK, tiles_per_seq)
    _, qkv1_m, _ = _mid(mix_m[0], meta_blk, *mid_args, BLOCK, BLOCK, 1)
    qkv1 = _meta_fill(qkv1, qkv1_m)

    r = jnp.arange(BLOCK)
    tri = jnp.concatenate([(r[:, None] >= r[None, :]).astype(BF16), jnp.ones((BLOCK, LANES), BF16)], axis=1)
    tri = jnp.concatenate([tri, tri], axis=0)
    o_gated = _sb_attn(qkv1, gate1.reshape(batch, seq, -1), tri, n_blocks)
    return _sb_out(o_gated, h1.reshape(batch, seq, D_MODEL), w_out1, sb_post_norm[0][None])
```

```python
import functools

import jax
import jax.numpy as jnp
from jax import lax
from jax.experimental import pallas as pl
from jax.experimental.pallas import tpu as pltpu

D_MODEL = 1024
N_META = 16
BLOCK = 128
META_PAD = BLOCK - N_META
HEAD_DIM = 64
ROPE_THETA = 10000.0
NORM_EPS = 1e-6
NEG_INF = -1e30
SWA_HEADS = 8
SWA_KV_HEADS = 2
SWA_GROUP = SWA_HEADS // SWA_KV_HEADS
SWA_WINDOW = 128
SWA_WIDTH = SWA_HEADS * HEAD_DIM
SWA_KV_WIDTH = SWA_KV_HEADS * HEAD_DIM
CONV_CHANNELS = 512
CONV_WIDTH = 31
CONV_LN_EPS = 1e-5
SB_HEADS = 16
SB_WIDTH = SB_HEADS * HEAD_DIM
AB_IN = 2 * SWA_WIDTH + 2 * SWA_KV_WIDTH + 3 * CONV_CHANNELS
AB_MIX = SWA_WIDTH + CONV_CHANNELS
SB_IN = 4 * SB_WIDTH

LANES = 128
SUBLANES = 8
ROW_TILE = 512
VMEM_LIMIT = 48 * 1024 * 1024
CONV_HALO = -(-(CONV_WIDTH - 1) // SUBLANES) * SUBLANES
CONV_WIN_START = BLOCK - CONV_HALO
CONV_WIN_ROWS = BLOCK + CONV_HALO
CONV_LEAD = CONV_HALO - (CONV_WIDTH - 1)

SB_PAIRS = SB_WIDTH // LANES
SB_Q_PER_STEP = 4
SB_MASS_CUTOFF = 104.0
LOG2_E = 1.4426950408889634

F32 = jnp.float32
BF16 = jnp.bfloat16


def _rms_normed(x, gain):
    return x * lax.rsqrt(jnp.mean(x * x, axis=-1, keepdims=True) + NORM_EPS) * gain


def _silu(x):
    return x * jax.nn.sigmoid(x)


def _params(*semantics):
    return pltpu.CompilerParams(dimension_semantics=semantics, vmem_limit_bytes=VMEM_LIMIT)


def _ab_in_kernel(h_ref, gain_ref, w_ref, cos_ref, sin_ref, qkv_ref, rest_ref):
    hn = _rms_normed(h_ref[...], gain_ref[...]).astype(BF16)

    def proj(lo, hi):
        return jnp.dot(hn, w_ref[:, lo:hi], preferred_element_type=F32)

    cos = cos_ref[...]
    sin = sin_ref[...]
    lane = lax.broadcasted_iota(jnp.int32, cos.shape, 1)
    first_half = (lane % HEAD_DIM) < (HEAD_DIM // 2)

    def rope(t):
        rot = jnp.where(first_half, pltpu.roll(t, LANES - HEAD_DIM // 2, 1), pltpu.roll(t, HEAD_DIM // 2, 1))
        return t * cos + rot * sin

    q = proj(0, SWA_WIDTH)
    for c in range(SWA_WIDTH // LANES):
        cols = slice(c * LANES, (c + 1) * LANES)
        qkv_ref[:, cols] = (rope(q[:, cols]) * HEAD_DIM ** -0.5).astype(BF16)
    v_lo = SWA_WIDTH + SWA_KV_WIDTH
    kv = proj(SWA_WIDTH, v_lo + SWA_KV_WIDTH)
    qkv_ref[:, SWA_WIDTH:v_lo] = rope(kv[:, :SWA_KV_WIDTH]).astype(BF16)
    qkv_ref[:, v_lo:v_lo + SWA_KV_WIDTH] = kv[:, SWA_KV_WIDTH:].astype(BF16)

    ga_lo = v_lo + SWA_KV_WIDTH
    rest_ref[:, 0:SWA_WIDTH] = _silu(proj(ga_lo, ga_lo + SWA_WIDTH))
    glu_lo = ga_lo + SWA_WIDTH
    glu_a = proj(glu_lo, glu_lo + CONV_CHANNELS)
    glu_b = proj(glu_lo + CONV_CHANNELS, glu_lo + 2 * CONV_CHANNELS)
    rest_ref[:, SWA_WIDTH:SWA_WIDTH + CONV_CHANNELS] = glu_a * jax.nn.sigmoid(glu_b)
    gb_lo = glu_lo + 2 * CONV_CHANNELS
    rest_ref[:, SWA_WIDTH + CONV_CHANNELS:] = _silu(proj(gb_lo, gb_lo + CONV_CHANNELS))


def _ab_in(h_flat, gain, w_bf, cos_t, sin_t, row_tile, table_block):
    rows = h_flat.shape[0]
    qkv_w = SWA_WIDTH + 2 * SWA_KV_WIDTH
    rest_w = SWA_WIDTH + 2 * CONV_CHANNELS
    return pl.pallas_call(
        _ab_in_kernel,
        grid=(rows // row_tile,),
        in_specs=[
            pl.BlockSpec((row_tile, D_MODEL), lambda r: (r, 0)),
            pl.BlockSpec((1, D_MODEL), lambda r: (0, 0)),
            pl.BlockSpec((D_MODEL, AB_IN), lambda r: (0, 0)),
            pl.BlockSpec((row_tile, LANES), lambda r: (table_block(r), 0)),
            pl.BlockSpec((row_tile, LANES), lambda r: (table_block(r), 0)),
        ],
        out_specs=[
            pl.BlockSpec((row_tile, qkv_w), lambda r: (r, 0)),
            pl.BlockSpec((row_tile, rest_w), lambda r: (r, 0)),
        ],
        out_shape=[
            jax.ShapeDtypeStruct((rows, qkv_w), BF16),
            jax.ShapeDtypeStruct((rows, rest_w), F32),
        ],
        compiler_params=_params("parallel"),
        name="ab_in_proj",
    )(h_flat, gain, w_bf, cos_t, sin_t)


def _ab_mix_kernel(first_ref_blk, sinks_ref, qkv_ref, kv_prev_ref, kv_meta_ref, rest_ref, u_prev_ref, u_meta_ref,
                   conv_w_ref, conv_b_ref, ln_g_ref, ln_b_ref, w_pw2_ref,
                   out_ref, ext_ref, rot_ref, s_ref, p_ref):
    ref_blk = pl.program_id(1) + first_ref_blk
    prev_is_meta = ref_blk == 1
    qkv = qkv_ref[0]
    kv_meta = kv_meta_ref[0]
    kv_prev = jnp.where(prev_is_meta, kv_meta, kv_prev_ref[0])

    k_cat = jnp.concatenate([kv_meta[:, :LANES], kv_prev[:, :LANES], qkv[:, SWA_WIDTH:SWA_WIDTH + LANES]], axis=0)
    v_cat = jnp.concatenate([kv_meta[:, LANES:], kv_prev[:, LANES:], qkv[:, SWA_WIDTH + LANES:]], axis=0)
    k_swap = jnp.concatenate([k_cat[:, HEAD_DIM:], k_cat[:, :HEAD_DIM]], axis=1)
    v_swap = jnp.concatenate([v_cat[:, HEAD_DIM:], v_cat[:, :HEAD_DIM]], axis=1)

    row = lax.broadcasted_iota(jnp.int32, (BLOCK, BLOCK), 0)
    col = lax.broadcasted_iota(jnp.int32, (BLOCK, BLOCK), 1)
    qpos = ref_blk * BLOCK + row
    meta_ok = (col >= META_PAD) & (qpos - col >= SWA_WINDOW)
    ppos = (ref_blk - 1) * BLOCK + col
    prev_ok = (ppos >= META_PAD) & (qpos - ppos < SWA_WINDOW)
    cpos = ref_blk * BLOCK + col
    cur_ok = (cpos >= META_PAD) & (qpos >= cpos)
    mask = jnp.concatenate([meta_ok, prev_ok, cur_ok], axis=1)

    lane = lax.broadcasted_iota(jnp.int32, (BLOCK, LANES), 1)
    low_half = lane < HEAD_DIM
    zero_bf = jnp.zeros((), BF16)

    for head in range(SWA_HEADS):
        parity = head % 2
        q_pair = qkv[:, (head // 2) * LANES:(head // 2 + 1) * LANES]
        q_h = jnp.where(low_half if parity == 0 else ~low_half, q_pair, zero_bf)
        k_use = k_cat if head // SWA_GROUP == parity else k_swap
        s_ref[head] = lax.dot_general(q_h, k_use, (((1,), (1,)), ((), ())), preferred_element_type=F32)
    inv_denoms = []
    for head in range(SWA_HEADS):
        s = jnp.where(mask, s_ref[head], NEG_INF)
        sink = sinks_ref[head]
        m = jnp.maximum(jnp.max(s, axis=-1, keepdims=True), sink)
        p = jnp.exp(s - m)
        inv_denoms.append(1.0 / (jnp.sum(p, axis=-1, keepdims=True) + jnp.exp(sink - m)))
        p_ref[head] = p.astype(BF16)
    rest = rest_ref[0]
    for pair in range(SWA_HEADS // 2):
        outs = []
        for parity in range(2):
            head = 2 * pair + parity
            v_use = v_cat if head // SWA_GROUP == parity else v_swap
            outs.append(jnp.dot(p_ref[head], v_use, preferred_element_type=F32) * inv_denoms[head])
        cols = slice(pair * LANES, (pair + 1) * LANES)
        out_ref[0, :, cols] = (jnp.where(low_half, outs[0], outs[1]) * rest[:, cols]).astype(BF16)

    u_prev = jnp.where(prev_is_meta, u_meta_ref[0], u_prev_ref[0])
    ext_ref[0:BLOCK, :] = jnp.where(ref_blk > 0, u_prev, 0.0)
    ext_ref[BLOCK:2 * BLOCK, :] = rest[:, SWA_WIDTH:SWA_WIDTH + CONV_CHANNELS]
    window = ext_ref[CONV_WIN_START:, :]
    for b in range(1, SUBLANES):
        rot_ref[b - 1] = pltpu.roll(window, CONV_WIN_ROWS - b, 0)
    conv_groups = []
    for cg in range(CONV_CHANNELS // LANES):
        cols = slice(cg * LANES, (cg + 1) * LANES)
        acc = jnp.zeros((BLOCK, LANES), F32) + conv_b_ref[:, cols]
        for w in range(CONV_WIDTH):
            a, b = divmod(CONV_LEAD + w, SUBLANES)
            if b == 0:
                taps = ext_ref[CONV_WIN_START + SUBLANES * a:CONV_WIN_START + SUBLANES * a + BLOCK, cols]
            else:
                taps = rot_ref[b - 1, SUBLANES * a:SUBLANES * a + BLOCK, cols]
            acc = acc + taps * conv_w_ref[w:w + 1, cols]
        conv_groups.append(acc)
    conv = jnp.concatenate(conv_groups, axis=1)

    mu = jnp.mean(conv, axis=-1, keepdims=True)
    xc = conv - mu
    ln = xc * lax.rsqrt(jnp.mean(xc * xc, axis=-1, keepdims=True) + CONV_LN_EPS) * ln_g_ref[...] + ln_b_ref[...]
    c_act = _silu(ln).astype(BF16)
    c_branch = jnp.dot(c_act, w_pw2_ref[...], preferred_element_type=F32) * rest[:, SWA_WIDTH + CONV_CHANNELS:]
    out_ref[0, :, SWA_WIDTH:] = c_branch.astype(BF16)


def _ab_mix(sinks, qkv, rest, qkv_meta, rest_meta, conv_w, conv_b, ln_g, ln_b, w_pw2, first_ref_blk):
    batch, rows, qkv_w = qkv.shape
    rest_w = rest.shape[-1]
    kv_blk = (SWA_WIDTH // (2 * SWA_KV_WIDTH))
    prev_blk = lambda i: jnp.maximum(i - 1, 0)
    const2 = lambda b, i: (0, 0)
    return pl.pallas_call(
        functools.partial(_ab_mix_kernel, first_ref_blk),
        grid=(batch, rows // BLOCK),
        in_specs=[
            pl.BlockSpec(memory_space=pltpu.SMEM),
            pl.BlockSpec((1, BLOCK, qkv_w), lambda b, i: (b, i, 0)),
            pl.BlockSpec((1, BLOCK, 2 * SWA_KV_WIDTH), lambda b, i: (b, prev_blk(i), kv_blk)),
            pl.BlockSpec((1, BLOCK, 2 * SWA_KV_WIDTH), lambda b, i: (0, 0, kv_blk)),
            pl.BlockSpec((1, BLOCK, rest_w), lambda b, i: (b, i, 0)),
            pl.BlockSpec((1, BLOCK, CONV_CHANNELS), lambda b, i: (b, prev_blk(i), 1)),
            pl.BlockSpec((1, BLOCK, CONV_CHANNELS), lambda b, i: (0, 0, 1)),
            pl.BlockSpec((CONV_WIDTH + 1, CONV_CHANNELS), const2),
            pl.BlockSpec((1, CONV_CHANNELS), const2),
            pl.BlockSpec((1, CONV_CHANNELS), const2),
            pl.BlockSpec((1, CONV_CHANNELS), const2),
            pl.BlockSpec((CONV_CHANNELS, CONV_CHANNELS), const2),
        ],
        out_specs=pl.BlockSpec((1, BLOCK, AB_MIX), lambda b, i: (b, i, 0)),
        out_shape=jax.ShapeDtypeStruct((batch, rows, AB_MIX), BF16),
        scratch_shapes=[
            pltpu.VMEM((2 * BLOCK, CONV_CHANNELS), F32),
            pltpu.VMEM((SUBLANES - 1, CONV_WIN_ROWS, CONV_CHANNELS), F32),
            pltpu.VMEM((SWA_HEADS, BLOCK, 3 * BLOCK), F32),
            pltpu.VMEM((SWA_HEADS, BLOCK, 3 * BLOCK), BF16),
        ],
        compiler_params=_params("parallel", "parallel"),
        name="ab_mix",
    )(sinks, qkv, qkv, qkv_meta, rest, rest, rest_meta, conv_w, conv_b, ln_g, ln_b, w_pw2)


def _mid_kernel(mix_ref, h_ref, w_out_ref, post_ref, gain_ref, w_in_ref, h1_ref, qkv_ref, gate_ref):
    y = jnp.dot(mix_ref[...], w_out_ref[...], preferred_element_type=F32)
    h1 = h_ref[...] + _rms_normed(y, post_ref[...])
    h1_ref[...] = h1
    hn = _rms_normed(h1, gain_ref[...]).astype(BF16)
    chunk = 4 * LANES
    even_head = (lax.broadcasted_iota(jnp.int32, (hn.shape[0], chunk), 1) % LANES) < HEAD_DIM

    def proj(lo):
        return jnp.dot(hn, w_in_ref[:, lo:lo + chunk], preferred_element_type=F32)

    for c in range(SB_WIDTH // chunk):
        lo = c * chunk
        qkv_ref[0, :, lo:lo + chunk] = (proj(lo) * HEAD_DIM ** -0.5).astype(BF16)
        qkv_ref[0, :, SB_WIDTH + lo:SB_WIDTH + lo + chunk] = proj(SB_WIDTH + lo).astype(BF16)
        v = proj(2 * SB_WIDTH + lo)
        qkv_ref[0, :, 2 * SB_WIDTH + lo:2 * SB_WIDTH + lo + chunk] = jnp.where(even_head, v, 0.0).astype(BF16)
        qkv_ref[0, :, 3 * SB_WIDTH + lo:3 * SB_WIDTH + lo + chunk] = jnp.where(even_head, 0.0, v).astype(BF16)
        gate_ref[:, lo:lo + chunk] = _silu(proj(3 * SB_WIDTH + lo))


def _mid_tail_kernel(n_tiles, mix_ref, h_ref, w_out_ref, post_ref, gain_ref, w_in_ref, tail_ref,
                     h1_ref, qkv_ref, gate_ref):
    @pl.when(pl.program_id(0) < n_tiles)
    def _():
        _mid_kernel(mix_ref, h_ref, w_out_ref, post_ref, gain_ref, w_in_ref, h1_ref, qkv_ref, gate_ref)

    @pl.when(pl.program_id(0) >= n_tiles)
    def _():
        qkv_ref[0, 0:tail_ref.shape[1], :] = tail_ref[0]


def _mid(mix_flat, h_flat, w_out_bf, post, gain, w_in_bf, row_tile, tiles_per_seq, qkv_tail=None):
    rows = h_flat.shape[0]
    n_tiles = rows // row_tile
    n_seq = n_tiles // tiles_per_seq
    seq_rows = tiles_per_seq * row_tile
    const = lambda shape: pl.BlockSpec(shape, lambda r: (0,) * len(shape))
    weights = [const((AB_MIX, D_MODEL)), const((1, D_MODEL)), const((1, D_MODEL)), const((D_MODEL, SB_IN))]
    if qkv_tail is None:
        tile = lambda r: r
        qkv_index = lambda r: (r // tiles_per_seq, r % tiles_per_seq, 0)
        body, steps, tail_specs, tail_args, tail_rows = _mid_kernel, n_tiles, [], (), 0
    else:
        tile = lambda r: jnp.minimum(r, n_tiles - 1)
        qkv_index = lambda r: (jnp.where(r < n_tiles, r // tiles_per_seq, r - n_tiles),
                               jnp.where(r < n_tiles, r % tiles_per_seq, tiles_per_seq), 0)
        body, steps = functools.partial(_mid_tail_kernel, n_tiles), n_tiles + n_seq
        tail_specs, tail_args, tail_rows = [const(qkv_tail.shape)], (qkv_tail,), qkv_tail.shape[1]
    row_spec = lambda width: pl.BlockSpec((row_tile, width), lambda r: (tile(r), 0))
    return pl.pallas_call(
        body,
        grid=(steps,),
        in_specs=[row_spec(AB_MIX), row_spec(D_MODEL)] + weights + tail_specs,
        out_specs=[row_spec(D_MODEL), pl.BlockSpec((1, row_tile, 4 * SB_WIDTH), qkv_index), row_spec(SB_WIDTH)],
        out_shape=[
            jax.ShapeDtypeStruct((rows, D_MODEL), F32),
            jax.ShapeDtypeStruct((n_seq, seq_rows + tail_rows, 4 * SB_WIDTH), BF16),
            jax.ShapeDtypeStruct((rows, SB_WIDTH), F32),
        ],
        compiler_params=_params("arbitrary"),
        name="ab_out_sb_in_proj",
    )(mix_flat, h_flat, w_out_bf, post, gain, w_in_bf, *tail_args)


def _sb_attn_kernel(q_ref, *refs):
    def one_query_block(sub, _):
        q_blk = pl.program_id(1) * SB_Q_PER_STEP + sub + 1
        _sb_query_block(q_blk, pl.ds(pl.multiple_of(sub * BLOCK, BLOCK), BLOCK), q_ref, *refs)
        return 0

    lax.fori_loop(0, SB_Q_PER_STEP, one_query_block, 0)


def _sb_query_block(q_blk, q_rows, q_ref, k_ref, v_even_ref, v_odd_ref, gate_ref, tri_ref, out_ref,
                    carry_ref, acc_ref, qh_ref, z_ref, split_ref, sums_ref, w_ref, bias_ref):
    meta_off = k_ref.shape[1] - BLOCK
    lane = lax.broadcasted_iota(jnp.int32, (BLOCK, LANES), 1)
    row = lax.broadcasted_iota(jnp.int32, (BLOCK, LANES), 0)
    low_half = lane < HEAD_DIM
    zero_bf = jnp.zeros((), BF16)

    qpos = q_blk * BLOCK + row
    carry_ref[...] = jnp.zeros_like(carry_ref)
    acc_ref[...] = jnp.zeros_like(acc_ref)
    for n in range(2 * SB_PAIRS):
        q_pair = q_ref[0, q_rows, (n // 2) * LANES:(n // 2 + 1) * LANES]
        qh_ref[n] = jnp.where(low_half, q_pair, zero_bf) if n % 2 == 0 else jnp.where(low_half, zero_bf, q_pair)

    def visit(k_blk):
        off = pl.multiple_of(jnp.where(k_blk == 0, meta_off, (k_blk - 1) * BLOCK), BLOCK)
        kpos = k_blk * BLOCK + lane

        def scores(n):
            k_pair = k_ref[0, pl.ds(off, BLOCK), (n // 2) * LANES:(n // 2 + 1) * LANES]
            z_ref[n] = lax.dot_general(qh_ref[n], k_pair, (((1,), (1,)), ((), ())), preferred_element_type=F32)

        bias_ref[...] = jnp.where((kpos >= META_PAD) & (kpos < qpos), 0.0, NEG_INF)

        def softplus_split(n):
            z = z_ref[n] + bias_ref[...]
            z_ref[n] = z - carry_ref[n]
            sp = jnp.maximum(z, 0.0) + jnp.log(1.0 + jnp.exp2(jnp.abs(z) * -LOG2_E))
            hi = sp.astype(BF16)
            split_ref[n, :, 0:LANES] = hi
            split_ref[n, :, LANES:] = (sp - hi.astype(F32)).astype(BF16)

        def suffix_sums(n):
            sums_ref[n] = jnp.dot(split_ref[n], tri_ref[...], preferred_element_type=F32)

        def carry_update(n, least):
            new_carry = carry_ref[n] + sums_ref[n, :, LANES:]
            carry_ref[n] = new_carry
            return new_carry if least is None else jnp.minimum(least, new_carry)

        def weights(n):
            w = jnp.exp(z_ref[n] - sums_ref[n, :, 0:LANES])
            w_ref[n // 2, :, (n % 2) * LANES:(n % 2 + 1) * LANES] = w.astype(BF16)

        def accumulate(p):
            cols = slice(p * LANES, (p + 1) * LANES)
            v_bd = jnp.concatenate([v_even_ref[0, pl.ds(off, BLOCK), cols], v_odd_ref[0, pl.ds(off, BLOCK), cols]], axis=0)
            acc_ref[p] += jnp.dot(w_ref[p], v_bd, preferred_element_type=F32)

        pairs, heads = range(SB_PAIRS), range(2 * SB_PAIRS)
        for n in heads:
            scores(n)
        for n in heads:
            softplus_split(n)
        for n in heads:
            suffix_sums(n)
        least = None
        for n in heads:
            least = carry_update(n, least)
        least_mass = jnp.min(least)
        for n in heads:
            weights(n)
        for p in pairs:
            accumulate(p)
        return least_mass

    def keep_going(state):
        k_blk, least_mass = state
        return (k_blk >= 0) & (least_mass < SB_MASS_CUTOFF)

    def body(state):
        k_blk, _ = state
        return k_blk - 1, visit(k_blk)

    lax.while_loop(keep_going, body, (q_blk, jnp.float32(0.0)))

    for p in range(SB_PAIRS):
        cols = slice(p * LANES, (p + 1) * LANES)
        out_ref[0, q_rows, cols] = (acc_ref[p] * gate_ref[0, q_rows, cols]).astype(BF16)


def _sb_attn(qkv, gate, tri):
    batch, lp, _ = qkv.shape
    seq = lp - BLOCK
    q_tile = SB_Q_PER_STEP * BLOCK
    seq_cols = lambda col: pl.BlockSpec((1, lp, SB_WIDTH), lambda b, i: (b, 0, col))
    return pl.pallas_call(
        _sb_attn_kernel,
        grid=(batch, seq // q_tile),
        in_specs=[
            pl.BlockSpec((1, q_tile, SB_WIDTH), lambda b, i: (b, i, 0)),
            seq_cols(1), seq_cols(2), seq_cols(3),
            pl.BlockSpec((1, q_tile, SB_WIDTH), lambda b, i: (b, i, 0)),
            pl.BlockSpec((2 * BLOCK, 2 * LANES), lambda b, i: (0, 0)),
        ],
        out_specs=pl.BlockSpec((1, q_tile, SB_WIDTH), lambda b, i: (b, i, 0)),
        out_shape=jax.ShapeDtypeStruct((batch, seq, SB_WIDTH), BF16),
        scratch_shapes=[
            pltpu.VMEM((2 * SB_PAIRS, BLOCK, LANES), F32),
            pltpu.VMEM((SB_PAIRS, BLOCK, LANES), F32),
            pltpu.VMEM((2 * SB_PAIRS, BLOCK, LANES), BF16),
            pltpu.VMEM((2 * SB_PAIRS, BLOCK, LANES), F32),
            pltpu.VMEM((2 * SB_PAIRS, BLOCK, 2 * LANES), BF16),
            pltpu.VMEM((2 * SB_PAIRS, BLOCK, 2 * LANES), F32),
            pltpu.VMEM((SB_PAIRS, BLOCK, 2 * LANES), BF16),
            pltpu.VMEM((BLOCK, LANES), F32),
        ],
        compiler_params=_params("parallel", "arbitrary"),
        name="sb_attn",
    )(qkv, qkv, qkv, qkv, gate, tri)


def _sb_out_kernel(o_ref, h_ref, w_ref, post_ref, out_ref):
    y = jnp.dot(o_ref[0], w_ref[...], preferred_element_type=F32)
    out_ref[0] = h_ref[0] + _rms_normed(y, post_ref[...])


def _sb_out(o_gated, h, w_bf, post):
    batch, rows, _ = o_gated.shape
    return pl.pallas_call(
        _sb_out_kernel,
        grid=(batch, rows // ROW_TILE),
        in_specs=[
            pl.BlockSpec((1, ROW_TILE, SB_WIDTH), lambda b, i: (b, i, 0)),
            pl.BlockSpec((1, ROW_TILE, D_MODEL), lambda b, i: (b, i, 0)),
            pl.BlockSpec((SB_WIDTH, D_MODEL), lambda b, i: (0, 0)),
            pl.BlockSpec((1, D_MODEL), lambda b, i: (0, 0)),
        ],
        out_specs=pl.BlockSpec((1, ROW_TILE, D_MODEL), lambda b, i: (b, i, 0)),
        out_shape=jax.ShapeDtypeStruct((batch, rows, D_MODEL), F32),
        compiler_params=_params("parallel", "parallel"),
        name="sb_out_proj",
    )(o_gated, h, w_bf, post)


def _rope_tables(seq):
    half = HEAD_DIM // 2
    pos = jnp.concatenate([N_META + jnp.arange(seq), jnp.zeros((META_PAD,), jnp.int32), jnp.arange(N_META)])
    inv = ROPE_THETA ** (-jnp.arange(half, dtype=jnp.float32) / half)
    ang = pos.astype(jnp.float32)[:, None] * inv[None, :]
    cos = jnp.cos(ang)
    sin = jnp.sin(ang)
    cos_t = jnp.tile(jnp.concatenate([cos, cos], axis=-1), (1, LANES // HEAD_DIM))
    sin_t = jnp.tile(jnp.concatenate([-sin, sin], axis=-1), (1, LANES // HEAD_DIM))
    return cos_t, sin_t


def kernel(x, meta_tokens, ab_pre_norm, ab_w_in, ab_sinks, ab_conv_w, ab_conv_b, ab_conv_ln_g, ab_conv_ln_b, ab_w_pw2, ab_w_out, ab_post_norm, sb_pre_norm, sb_w_in, sb_w_out, sb_post_norm):
    batch, seq, _ = x.shape
    assert seq % ROW_TILE == 0
    tiles_per_seq = seq // ROW_TILE
    n_blocks = seq // BLOCK

    x_flat = x.reshape(batch * seq, D_MODEL)
    meta_blk = jnp.concatenate([jnp.zeros((META_PAD, D_MODEL), x.dtype), meta_tokens.astype(x.dtype)], axis=0)
    cos_t, sin_t = _rope_tables(seq)
    pre0, post0, pre1 = ab_pre_norm[0][None], ab_post_norm[0][None], sb_pre_norm[0][None]
    w_in0, w_pw2, w_out0 = ab_w_in[0].astype(BF16), ab_w_pw2[0].astype(BF16), ab_w_out[0].astype(BF16)
    w_in1, w_out1 = sb_w_in[0].astype(BF16), sb_w_out[0].astype(BF16)

    qkv, rest = _ab_in(x_flat, pre0, w_in0, cos_t, sin_t, ROW_TILE, lambda r: r % tiles_per_seq)
    qkv_m, rest_m = _ab_in(meta_blk, pre0, w_in0, cos_t, sin_t, BLOCK, lambda r: n_blocks)
    qkv_m, rest_m = qkv_m[None], rest_m[None]
    conv_w = jnp.concatenate([ab_conv_w[0], jnp.zeros((1, CONV_CHANNELS), F32)], axis=0)
    mix_args = (conv_w, ab_conv_b[0][None], ab_conv_ln_g[0][None], ab_conv_ln_b[0][None], w_pw2)
    mix = _ab_mix(ab_sinks[0], qkv.reshape(batch, seq, -1), rest.reshape(batch, seq, -1), qkv_m, rest_m, *mix_args, 1)
    mix_m = _ab_mix(ab_sinks[0], qkv_m, rest_m, qkv_m, rest_m, *mix_args, 0)

    mid_args = (w_out0, post0, pre1, w_in1)
    _, qkv1_m, _ = _mid(mix_m[0], meta_blk, *mid_args, BLOCK, 1)
    h1, qkv1, gate1 = _mid(mix.reshape(batch * seq, AB_MIX), x_flat, *mid_args, ROW_TILE, tiles_per_seq, qkv1_m)

    r = jnp.arange(BLOCK)
    tri = jnp.concatenate([(r[:, None] >= r[None, :]).astype(BF16), jnp.ones((BLOCK, LANES), BF16)], axis=1)
    tri = jnp.concatenate([tri, tri], axis=0)
    o_gated = _sb_attn(qkv1, gate1.reshape(batch, seq, -1), tri)
    return _sb_out(o_gated, h1.reshape(batch, seq, D_MODEL), w_out1, sb_post_norm[0][None])
```

```python
import functools

import jax
import jax.numpy as jnp
from jax import lax
from jax.experimental import pallas as pl
from jax.experimental.pallas import tpu as pltpu

D_MODEL = 1024
N_META = 16
BLOCK = 128
META_PAD = BLOCK - N_META
HEAD_DIM = 64
ROPE_THETA = 10000.0
NORM_EPS = 1e-6
NEG_INF = -1e30
SWA_HEADS = 8
SWA_KV_HEADS = 2
SWA_GROUP = SWA_HEADS // SWA_KV_HEADS
SWA_WINDOW = 128
SWA_WIDTH = SWA_HEADS * HEAD_DIM
SWA_KV_WIDTH = SWA_KV_HEADS * HEAD_DIM
CONV_CHANNELS = 512
CONV_WIDTH = 31
CONV_LN_EPS = 1e-5
SB_HEADS = 16
SB_WIDTH = SB_HEADS * HEAD_DIM
AB_IN = 2 * SWA_WIDTH + 2 * SWA_KV_WIDTH + 3 * CONV_CHANNELS
AB_MIX = SWA_WIDTH + CONV_CHANNELS
SB_IN = 4 * SB_WIDTH

LANES = 128
SUBLANES = 8
ROW_TILE = 512
VMEM_LIMIT = 48 * 1024 * 1024
CONV_HALO = -(-(CONV_WIDTH - 1) // SUBLANES) * SUBLANES
CONV_WIN_START = BLOCK - CONV_HALO
CONV_WIN_ROWS = BLOCK + CONV_HALO
CONV_LEAD = CONV_HALO - (CONV_WIDTH - 1)

SB_PAIRS = SB_WIDTH // LANES
SB_Q_PER_STEP = 4
SB_SUM_GROUP = 8
SB_MASS_CUTOFF = 104.0
LOG2_E = 1.4426950408889634

F32 = jnp.float32
BF16 = jnp.bfloat16


def _rms_normed(x, gain):
    return x * lax.rsqrt(jnp.mean(x * x, axis=-1, keepdims=True) + NORM_EPS) * gain


def _silu(x):
    return x * jax.nn.sigmoid(x)


def _params(*semantics):
    return pltpu.CompilerParams(dimension_semantics=semantics, vmem_limit_bytes=VMEM_LIMIT)


def _ab_in_kernel(h_ref, gain_ref, w_ref, cos_ref, sin_ref, qkv_ref, rest_ref):
    hn = _rms_normed(h_ref[...], gain_ref[...]).astype(BF16)

    def proj(lo, hi):
        return jnp.dot(hn, w_ref[:, lo:hi], preferred_element_type=F32)

    cos = cos_ref[...]
    sin = sin_ref[...]
    lane = lax.broadcasted_iota(jnp.int32, cos.shape, 1)
    first_half = (lane % HEAD_DIM) < (HEAD_DIM // 2)

    def rope(t):
        rot = jnp.where(first_half, pltpu.roll(t, LANES - HEAD_DIM // 2, 1), pltpu.roll(t, HEAD_DIM // 2, 1))
        return t * cos + rot * sin

    q = proj(0, SWA_WIDTH)
    for c in range(SWA_WIDTH // LANES):
        cols = slice(c * LANES, (c + 1) * LANES)
        qkv_ref[:, cols] = (rope(q[:, cols]) * HEAD_DIM ** -0.5).astype(BF16)
    v_lo = SWA_WIDTH + SWA_KV_WIDTH
    kv = proj(SWA_WIDTH, v_lo + SWA_KV_WIDTH)
    qkv_ref[:, SWA_WIDTH:v_lo] = rope(kv[:, :SWA_KV_WIDTH]).astype(BF16)
    qkv_ref[:, v_lo:v_lo + SWA_KV_WIDTH] = kv[:, SWA_KV_WIDTH:].astype(BF16)

    ga_lo = v_lo + SWA_KV_WIDTH
    rest_ref[:, 0:SWA_WIDTH] = _silu(proj(ga_lo, ga_lo + SWA_WIDTH))
    glu_lo = ga_lo + SWA_WIDTH
    glu_a = proj(glu_lo, glu_lo + CONV_CHANNELS)
    glu_b = proj(glu_lo + CONV_CHANNELS, glu_lo + 2 * CONV_CHANNELS)
    rest_ref[:, SWA_WIDTH:SWA_WIDTH + CONV_CHANNELS] = glu_a * jax.nn.sigmoid(glu_b)
    gb_lo = glu_lo + 2 * CONV_CHANNELS
    rest_ref[:, SWA_WIDTH + CONV_CHANNELS:] = _silu(proj(gb_lo, gb_lo + CONV_CHANNELS))


def _ab_in(h_flat, gain, w_bf, cos_t, sin_t, row_tile, table_block):
    rows = h_flat.shape[0]
    qkv_w = SWA_WIDTH + 2 * SWA_KV_WIDTH
    rest_w = SWA_WIDTH + 2 * CONV_CHANNELS
    return pl.pallas_call(
        _ab_in_kernel,
        grid=(rows // row_tile,),
        in_specs=[
            pl.BlockSpec((row_tile, D_MODEL), lambda r: (r, 0)),
            pl.BlockSpec((1, D_MODEL), lambda r: (0, 0)),
            pl.BlockSpec((D_MODEL, AB_IN), lambda r: (0, 0)),
            pl.BlockSpec((row_tile, LANES), lambda r: (table_block(r), 0)),
            pl.BlockSpec((row_tile, LANES), lambda r: (table_block(r), 0)),
        ],
        out_specs=[
            pl.BlockSpec((row_tile, qkv_w), lambda r: (r, 0)),
            pl.BlockSpec((row_tile, rest_w), lambda r: (r, 0)),
        ],
        out_shape=[
            jax.ShapeDtypeStruct((rows, qkv_w), BF16),
            jax.ShapeDtypeStruct((rows, rest_w), F32),
        ],
        compiler_params=_params("parallel"),
        name="ab_in_proj",
    )(h_flat, gain, w_bf, cos_t, sin_t)


def _ab_mix_kernel(first_ref_blk, sinks_ref, qkv_ref, kv_prev_ref, kv_meta_ref, rest_ref, u_prev_ref, u_meta_ref,
                   conv_w_ref, conv_b_ref, ln_g_ref, ln_b_ref, w_pw2_ref,
                   out_ref, ext_ref, rot_ref, s_ref, p_ref):
    ref_blk = pl.program_id(1) + first_ref_blk
    prev_is_meta = ref_blk == 1
    qkv = qkv_ref[0]
    kv_meta = kv_meta_ref[0]
    kv_prev = jnp.where(prev_is_meta, kv_meta, kv_prev_ref[0])

    k_cat = jnp.concatenate([kv_meta[:, :LANES], kv_prev[:, :LANES], qkv[:, SWA_WIDTH:SWA_WIDTH + LANES]], axis=0)
    v_cat = jnp.concatenate([kv_meta[:, LANES:], kv_prev[:, LANES:], qkv[:, SWA_WIDTH + LANES:]], axis=0)
    k_swap = jnp.concatenate([k_cat[:, HEAD_DIM:], k_cat[:, :HEAD_DIM]], axis=1)
    v_swap = jnp.concatenate([v_cat[:, HEAD_DIM:], v_cat[:, :HEAD_DIM]], axis=1)

    row = lax.broadcasted_iota(jnp.int32, (BLOCK, BLOCK), 0)
    col = lax.broadcasted_iota(jnp.int32, (BLOCK, BLOCK), 1)
    qpos = ref_blk * BLOCK + row
    meta_ok = (col >= META_PAD) & (qpos - col >= SWA_WINDOW)
    ppos = (ref_blk - 1) * BLOCK + col
    prev_ok = (ppos >= META_PAD) & (qpos - ppos < SWA_WINDOW)
    cpos = ref_blk * BLOCK + col
    cur_ok = (cpos >= META_PAD) & (qpos >= cpos)
    mask = jnp.concatenate([meta_ok, prev_ok, cur_ok], axis=1)

    lane = lax.broadcasted_iota(jnp.int32, (BLOCK, LANES), 1)
    low_half = lane < HEAD_DIM
    zero_bf = jnp.zeros((), BF16)

    for head in range(SWA_HEADS):
        parity = head % 2
        q_pair = qkv[:, (head // 2) * LANES:(head // 2 + 1) * LANES]
        q_h = jnp.where(low_half if parity == 0 else ~low_half, q_pair, zero_bf)
        k_use = k_cat if head // SWA_GROUP == parity else k_swap
        s_ref[head] = lax.dot_general(q_h, k_use, (((1,), (1,)), ((), ())), preferred_element_type=F32)
    inv_denoms = []
    for head in range(SWA_HEADS):
        s = jnp.where(mask, s_ref[head], NEG_INF)
        sink = sinks_ref[head]
        m = jnp.maximum(jnp.max(s, axis=-1, keepdims=True), sink)
        p = jnp.exp(s - m)
        inv_denoms.append(1.0 / (jnp.sum(p, axis=-1, keepdims=True) + jnp.exp(sink - m)))
        p_ref[head] = p.astype(BF16)
    rest = rest_ref[0]
    for pair in range(SWA_HEADS // 2):
        outs = []
        for parity in range(2):
            head = 2 * pair + parity
            v_use = v_cat if head // SWA_GROUP == parity else v_swap
            outs.append(jnp.dot(p_ref[head], v_use, preferred_element_type=F32) * inv_denoms[head])
        cols = slice(pair * LANES, (pair + 1) * LANES)
        out_ref[0, :, cols] = (jnp.where(low_half, outs[0], outs[1]) * rest[:, cols]).astype(BF16)

    u_prev = jnp.where(prev_is_meta, u_meta_ref[0], u_prev_ref[0])
    ext_ref[0:BLOCK, :] = jnp.where(ref_blk > 0, u_prev, 0.0)
    ext_ref[BLOCK:2 * BLOCK, :] = rest[:, SWA_WIDTH:SWA_WIDTH + CONV_CHANNELS]
    window = ext_ref[CONV_WIN_START:, :]
    for b in range(1, SUBLANES):
        rot_ref[b - 1] = pltpu.roll(window, CONV_WIN_ROWS - b, 0)
    conv_groups = []
    for cg in range(CONV_CHANNELS // LANES):
        cols = slice(cg * LANES, (cg + 1) * LANES)
        acc = jnp.zeros((BLOCK, LANES), F32) + conv_b_ref[:, cols]
        for w in range(CONV_WIDTH):
            a, b = divmod(CONV_LEAD + w, SUBLANES)
            if b == 0:
                taps = ext_ref[CONV_WIN_START + SUBLANES * a:CONV_WIN_START + SUBLANES * a + BLOCK, cols]
            else:
                taps = rot_ref[b - 1, SUBLANES * a:SUBLANES * a + BLOCK, cols]
            acc = acc + taps * conv_w_ref[w:w + 1, cols]
        conv_groups.append(acc)
    conv = jnp.concatenate(conv_groups, axis=1)

    mu = jnp.mean(conv, axis=-1, keepdims=True)
    xc = conv - mu
    ln = xc * lax.rsqrt(jnp.mean(xc * xc, axis=-1, keepdims=True) + CONV_LN_EPS) * ln_g_ref[...] + ln_b_ref[...]
    c_act = _silu(ln).astype(BF16)
    c_branch = jnp.dot(c_act, w_pw2_ref[...], preferred_element_type=F32) * rest[:, SWA_WIDTH + CONV_CHANNELS:]
    out_ref[0, :, SWA_WIDTH:] = c_branch.astype(BF16)


def _ab_mix(sinks, qkv, rest, qkv_meta, rest_meta, conv_w, conv_b, ln_g, ln_b, w_pw2, first_ref_blk):
    batch, rows, qkv_w = qkv.shape
    rest_w = rest.shape[-1]
    kv_blk = (SWA_WIDTH // (2 * SWA_KV_WIDTH))
    prev_blk = lambda i: jnp.maximum(i - 1, 0)
    const2 = lambda b, i: (0, 0)
    return pl.pallas_call(
        functools.partial(_ab_mix_kernel, first_ref_blk),
        grid=(batch, rows // BLOCK),
        in_specs=[
            pl.BlockSpec(memory_space=pltpu.SMEM),
            pl.BlockSpec((1, BLOCK, qkv_w), lambda b, i: (b, i, 0)),
            pl.BlockSpec((1, BLOCK, 2 * SWA_KV_WIDTH), lambda b, i: (b, prev_blk(i), kv_blk)),
            pl.BlockSpec((1, BLOCK, 2 * SWA_KV_WIDTH), lambda b, i: (0, 0, kv_blk)),
            pl.BlockSpec((1, BLOCK, rest_w), lambda b, i: (b, i, 0)),
            pl.BlockSpec((1, BLOCK, CONV_CHANNELS), lambda b, i: (b, prev_blk(i), 1)),
            pl.BlockSpec((1, BLOCK, CONV_CHANNELS), lambda b, i: (0, 0, 1)),
            pl.BlockSpec((CONV_WIDTH + 1, CONV_CHANNELS), const2),
            pl.BlockSpec((1, CONV_CHANNELS), const2),
            pl.BlockSpec((1, CONV_CHANNELS), const2),
            pl.BlockSpec((1, CONV_CHANNELS), const2),
            pl.BlockSpec((CONV_CHANNELS, CONV_CHANNELS), const2),
        ],
        out_specs=pl.BlockSpec((1, BLOCK, AB_MIX), lambda b, i: (b, i, 0)),
        out_shape=jax.ShapeDtypeStruct((batch, rows, AB_MIX), BF16),
        scratch_shapes=[
            pltpu.VMEM((2 * BLOCK, CONV_CHANNELS), F32),
            pltpu.VMEM((SUBLANES - 1, CONV_WIN_ROWS, CONV_CHANNELS), F32),
            pltpu.VMEM((SWA_HEADS, BLOCK, 3 * BLOCK), F32),
            pltpu.VMEM((SWA_HEADS, BLOCK, 3 * BLOCK), BF16),
        ],
        compiler_params=_params("parallel", "parallel"),
        name="ab_mix",
    )(sinks, qkv, qkv, qkv_meta, rest, rest, rest_meta, conv_w, conv_b, ln_g, ln_b, w_pw2)


def _mid_kernel(mix_ref, h_ref, w_out_ref, post_ref, gain_ref, w_in_ref, h1_ref, qkv_ref, gate_ref):
    y = jnp.dot(mix_ref[...], w_out_ref[...], preferred_element_type=F32)
    h1 = h_ref[...] + _rms_normed(y, post_ref[...])
    h1_ref[...] = h1
    hn = _rms_normed(h1, gain_ref[...]).astype(BF16)
    chunk = 4 * LANES
    even_head = (lax.broadcasted_iota(jnp.int32, (hn.shape[0], chunk), 1) % LANES) < HEAD_DIM

    def proj(lo):
        return jnp.dot(hn, w_in_ref[:, lo:lo + chunk], preferred_element_type=F32)

    for c in range(SB_WIDTH // chunk):
        lo = c * chunk
        qkv_ref[0, :, lo:lo + chunk] = (proj(lo) * HEAD_DIM ** -0.5).astype(BF16)
        qkv_ref[0, :, SB_WIDTH + lo:SB_WIDTH + lo + chunk] = proj(SB_WIDTH + lo).astype(BF16)
        v = proj(2 * SB_WIDTH + lo)
        qkv_ref[0, :, 2 * SB_WIDTH + lo:2 * SB_WIDTH + lo + chunk] = jnp.where(even_head, v, 0.0).astype(BF16)
        qkv_ref[0, :, 3 * SB_WIDTH + lo:3 * SB_WIDTH + lo + chunk] = jnp.where(even_head, 0.0, v).astype(BF16)
        gate_ref[:, lo:lo + chunk] = _silu(proj(3 * SB_WIDTH + lo))


def _mid_tail_kernel(n_tiles, mix_ref, h_ref, w_out_ref, post_ref, gain_ref, w_in_ref, tail_ref,
                     h1_ref, qkv_ref, gate_ref):
    @pl.when(pl.program_id(0) < n_tiles)
    def _():
        _mid_kernel(mix_ref, h_ref, w_out_ref, post_ref, gain_ref, w_in_ref, h1_ref, qkv_ref, gate_ref)

    @pl.when(pl.program_id(0) >= n_tiles)
    def _():
        qkv_ref[0, 0:tail_ref.shape[1], :] = tail_ref[0]


def _mid(mix_flat, h_flat, w_out_bf, post, gain, w_in_bf, row_tile, tiles_per_seq, qkv_tail=None):
    rows = h_flat.shape[0]
    n_tiles = rows // row_tile
    n_seq = n_tiles // tiles_per_seq
    seq_rows = tiles_per_seq * row_tile
    const = lambda shape: pl.BlockSpec(shape, lambda r: (0,) * len(shape))
    weights = [const((AB_MIX, D_MODEL)), const((1, D_MODEL)), const((1, D_MODEL)), const((D_MODEL, SB_IN))]
    if qkv_tail is None:
        tile = lambda r: r
        qkv_index = lambda r: (r // tiles_per_seq, r % tiles_per_seq, 0)
        body, steps, tail_specs, tail_args, tail_rows = _mid_kernel, n_tiles, [], (), 0
    else:
        tile = lambda r: jnp.minimum(r, n_tiles - 1)
        qkv_index = lambda r: (jnp.where(r < n_tiles, r // tiles_per_seq, r - n_tiles),
                               jnp.where(r < n_tiles, r % tiles_per_seq, tiles_per_seq), 0)
        body, steps = functools.partial(_mid_tail_kernel, n_tiles), n_tiles + n_seq
        tail_specs, tail_args, tail_rows = [const(qkv_tail.shape)], (qkv_tail,), qkv_tail.shape[1]
    row_spec = lambda width: pl.BlockSpec((row_tile, width), lambda r: (tile(r), 0))
    return pl.pallas_call(
        body,
        grid=(steps,),
        in_specs=[row_spec(AB_MIX), row_spec(D_MODEL)] + weights + tail_specs,
        out_specs=[row_spec(D_MODEL), pl.BlockSpec((1, row_tile, 4 * SB_WIDTH), qkv_index), row_spec(SB_WIDTH)],
        out_shape=[
            jax.ShapeDtypeStruct((rows, D_MODEL), F32),
            jax.ShapeDtypeStruct((n_seq, seq_rows + tail_rows, 4 * SB_WIDTH), BF16),
            jax.ShapeDtypeStruct((rows, SB_WIDTH), F32),
        ],
        compiler_params=_params("arbitrary"),
        name="ab_out_sb_in_proj",
    )(mix_flat, h_flat, w_out_bf, post, gain, w_in_bf, *tail_args)


def _sb_attn_kernel(q_ref, *refs):
    def one_query_block(sub, _):
        q_blk = pl.program_id(1) * SB_Q_PER_STEP + sub + 1
        _sb_query_block(q_blk, pl.ds(pl.multiple_of(sub * BLOCK, BLOCK), BLOCK), q_ref, *refs)
        return 0

    lax.fori_loop(0, SB_Q_PER_STEP, one_query_block, 0)


def _sb_query_block(q_blk, q_rows, q_ref, k_ref, v_even_ref, v_odd_ref, gate_ref, tri_ref, out_ref,
                    carry_ref, acc_ref, qh_ref, z_ref, split_ref, sums_ref, w_ref, bias_ref):
    meta_off = k_ref.shape[1] - BLOCK
    lane = lax.broadcasted_iota(jnp.int32, (BLOCK, LANES), 1)
    row = lax.broadcasted_iota(jnp.int32, (BLOCK, LANES), 0)
    low_half = lane < HEAD_DIM
    zero_bf = jnp.zeros((), BF16)

    qpos = q_blk * BLOCK + row
    carry_ref[...] = jnp.zeros_like(carry_ref)
    acc_ref[...] = jnp.zeros_like(acc_ref)
    for n in range(2 * SB_PAIRS):
        q_pair = q_ref[0, q_rows, (n // 2) * LANES:(n // 2 + 1) * LANES]
        qh_ref[n] = jnp.where(low_half, q_pair, zero_bf) if n % 2 == 0 else jnp.where(low_half, zero_bf, q_pair)

    def visit(k_blk):
        off = pl.multiple_of(jnp.where(k_blk == 0, meta_off, (k_blk - 1) * BLOCK), BLOCK)
        kpos = k_blk * BLOCK + lane

        def scores(p):
            k_pair = k_ref[0, pl.ds(off, BLOCK), p * LANES:(p + 1) * LANES]
            q_both = qh_ref[2 * p:2 * p + 2].reshape(2 * BLOCK, LANES)
            z = lax.dot_general(q_both, k_pair, (((1,), (1,)), ((), ())), preferred_element_type=F32)
            z_ref[2 * p:2 * p + 2] = z.reshape(2, BLOCK, LANES)

        bias_ref[...] = jnp.where((kpos >= META_PAD) & (kpos < qpos), 0.0, NEG_INF)

        def softplus_split(n):
            z = z_ref[n] + bias_ref[...]
            z_ref[n] = z - carry_ref[n]
            sp = jnp.maximum(z, 0.0) + jnp.log(1.0 + jnp.exp2(jnp.abs(z) * -LOG2_E))
            hi = sp.astype(BF16)
            split_ref[n, :, 0:LANES] = hi
            split_ref[n, :, LANES:] = (sp - hi.astype(F32)).astype(BF16)

        def suffix_sums(g):
            heads_g = slice(g * SB_SUM_GROUP, (g + 1) * SB_SUM_GROUP)
            sums = jnp.dot(split_ref[heads_g].reshape(SB_SUM_GROUP * BLOCK, 2 * LANES), tri_ref[...],
                           preferred_element_type=F32)
            sums_ref[heads_g] = sums.reshape(SB_SUM_GROUP, BLOCK, 2 * LANES)

        def carry_update(n, least):
            new_carry = carry_ref[n] + sums_ref[n, :, LANES:]
            carry_ref[n] = new_carry
            return new_carry if least is None else jnp.minimum(least, new_carry)

        def weights(n):
            w = jnp.exp(z_ref[n] - sums_ref[n, :, 0:LANES])
            w_ref[n // 2, :, (n % 2) * LANES:(n % 2 + 1) * LANES] = w.astype(BF16)

        def accumulate(p):
            cols = slice(p * LANES, (p + 1) * LANES)
            v_bd = jnp.concatenate([v_even_ref[0, pl.ds(off, BLOCK), cols], v_odd_ref[0, pl.ds(off, BLOCK), cols]], axis=0)
            acc_ref[p] += jnp.dot(w_ref[p], v_bd, preferred_element_type=F32)

        pairs, heads = range(SB_PAIRS), range(2 * SB_PAIRS)
        for p in pairs:
            scores(p)
        for n in heads:
            softplus_split(n)
        for g in range(2 * SB_PAIRS // SB_SUM_GROUP):
            suffix_sums(g)
        least = None
        for n in heads:
            least = carry_update(n, least)
        least_mass = jnp.min(least)
        for n in heads:
            weights(n)
        for p in pairs:
            accumulate(p)
        return least_mass

    def keep_going(state):
        k_blk, least_mass = state
        return (k_blk >= 0) & (least_mass < SB_MASS_CUTOFF)

    def body(state):
        k_blk, _ = state
        return k_blk - 1, visit(k_blk)

    lax.while_loop(keep_going, body, (q_blk, jnp.float32(0.0)))

    for p in range(SB_PAIRS):
        cols = slice(p * LANES, (p + 1) * LANES)
        out_ref[0, q_rows, cols] = (acc_ref[p] * gate_ref[0, q_rows, cols]).astype(BF16)


def _sb_attn(qkv, gate, tri):
    batch, lp, _ = qkv.shape
    seq = lp - BLOCK
    q_tile = SB_Q_PER_STEP * BLOCK
    seq_cols = lambda col: pl.BlockSpec((1, lp, SB_WIDTH), lambda b, i: (b, 0, col))
    return pl.pallas_call(
        _sb_attn_kernel,
        grid=(batch, seq // q_tile),
        in_specs=[
            pl.BlockSpec((1, q_tile, SB_WIDTH), lambda b, i: (b, i, 0)),
            seq_cols(1), seq_cols(2), seq_cols(3),
            pl.BlockSpec((1, q_tile, SB_WIDTH), lambda b, i: (b, i, 0)),
            pl.BlockSpec((2 * BLOCK, 2 * LANES), lambda b, i: (0, 0)),
        ],
        out_specs=pl.BlockSpec((1, q_tile, SB_WIDTH), lambda b, i: (b, i, 0)),
        out_shape=jax.ShapeDtypeStruct((batch, seq, SB_WIDTH), BF16),
        scratch_shapes=[
            pltpu.VMEM((2 * SB_PAIRS, BLOCK, LANES), F32),
            pltpu.VMEM((SB_PAIRS, BLOCK, LANES), F32),
            pltpu.VMEM((2 * SB_PAIRS, BLOCK, LANES), BF16),
            pltpu.VMEM((2 * SB_PAIRS, BLOCK, LANES), F32),
            pltpu.VMEM((2 * SB_PAIRS, BLOCK, 2 * LANES), BF16),
            pltpu.VMEM((2 * SB_PAIRS, BLOCK, 2 * LANES), F32),
            pltpu.VMEM((SB_PAIRS, BLOCK, 2 * LANES), BF16),
            pltpu.VMEM((BLOCK, LANES), F32),
        ],
        compiler_params=_params("parallel", "arbitrary"),
        name="sb_attn",
    )(qkv, qkv, qkv, qkv, gate, tri)


def _sb_out_kernel(o_ref, h_ref, w_ref, post_ref, out_ref):
    y = jnp.dot(o_ref[0], w_ref[...], preferred_element_type=F32)
    out_ref[0] = h_ref[0] + _rms_normed(y, post_ref[...])


def _sb_out(o_gated, h, w_bf, post):
    batch, rows, _ = o_gated.shape
    return pl.pallas_call(
        _sb_out_kernel,
        grid=(batch, rows // ROW_TILE),
        in_specs=[
            pl.BlockSpec((1, ROW_TILE, SB_WIDTH), lambda b, i: (b, i, 0)),
            pl.BlockSpec((1, ROW_TILE, D_MODEL), lambda b, i: (b, i, 0)),
            pl.BlockSpec((SB_WIDTH, D_MODEL), lambda b, i: (0, 0)),
            pl.BlockSpec((1, D_MODEL), lambda b, i: (0, 0)),
        ],
        out_specs=pl.BlockSpec((1, ROW_TILE, D_MODEL), lambda b, i: (b, i, 0)),
        out_shape=jax.ShapeDtypeStruct((batch, rows, D_MODEL), F32),
        compiler_params=_params("parallel", "parallel"),
        name="sb_out_proj",
    )(o_gated, h, w_bf, post)


def _rope_tables(seq):
    half = HEAD_DIM // 2
    pos = jnp.concatenate([N_META + jnp.arange(seq), jnp.zeros((META_PAD,), jnp.int32), jnp.arange(N_META)])
    inv = ROPE_THETA ** (-jnp.arange(half, dtype=jnp.float32) / half)
    ang = pos.astype(jnp.float32)[:, None] * inv[None, :]
    cos = jnp.cos(ang)
    sin = jnp.sin(ang)
    cos_t = jnp.tile(jnp.concatenate([cos, cos], axis=-1), (1, LANES // HEAD_DIM))
    sin_t = jnp.tile(jnp.concatenate([-sin, sin], axis=-1), (1, LANES // HEAD_DIM))
    return cos_t, sin_t


def kernel(x, meta_tokens, ab_pre_norm, ab_w_in, ab_sinks, ab_conv_w, ab_conv_b, ab_conv_ln_g, ab_conv_ln_b, ab_w_pw2, ab_w_out, ab_post_norm, sb_pre_norm, sb_w_in, sb_w_out, sb_post_norm):
    batch, seq, _ = x.shape
    assert seq % ROW_TILE == 0
    tiles_per_seq = seq // ROW_TILE
    n_blocks = seq // BLOCK

    x_flat = x.reshape(batch * seq, D_MODEL)
    meta_blk = jnp.concatenate([jnp.zeros((META_PAD, D_MODEL), x.dtype), meta_tokens.astype(x.dtype)], axis=0)
    cos_t, sin_t = _rope_tables(seq)
    pre0, post0, pre1 = ab_pre_norm[0][None], ab_post_norm[0][None], sb_pre_norm[0][None]
    w_in0, w_pw2, w_out0 = ab_w_in[0].astype(BF16), ab_w_pw2[0].astype(BF16), ab_w_out[0].astype(BF16)
    w_in1, w_out1 = sb_w_in[0].astype(BF16), sb_w_out[0].astype(BF16)

    qkv, rest = _ab_in(x_flat, pre0, w_in0, cos_t, sin_t, ROW_TILE, lambda r: r % tiles_per_seq)
    qkv_m, rest_m = _ab_in(meta_blk, pre0, w_in0, cos_t, sin_t, BLOCK, lambda r: n_blocks)
    qkv_m, rest_m = qkv_m[None], rest_m[None]
    conv_w = jnp.concatenate([ab_conv_w[0], jnp.zeros((1, CONV_CHANNELS), F32)], axis=0)
    mix_args = (conv_w, ab_conv_b[0][None], ab_conv_ln_g[0][None], ab_conv_ln_b[0][None], w_pw2)
    mix = _ab_mix(ab_sinks[0], qkv.reshape(batch, seq, -1), rest.reshape(batch, seq, -1), qkv_m, rest_m, *mix_args, 1)
    mix_m = _ab_mix(ab_sinks[0], qkv_m, rest_m, qkv_m, rest_m, *mix_args, 0)

    mid_args = (w_out0, post0, pre1, w_in1)
    _, qkv1_m, _ = _mid(mix_m[0], meta_blk, *mid_args, BLOCK, 1)
    h1, qkv1, gate1 = _mid(mix.reshape(batch * seq, AB_MIX), x_flat, *mid_args, ROW_TILE, tiles_per_seq, qkv1_m)

    r = jnp.arange(BLOCK)
    tri = jnp.concatenate([(r[:, None] >= r[None, :]).astype(BF16), jnp.ones((BLOCK, LANES), BF16)], axis=1)
    tri = jnp.concatenate([tri, tri], axis=0)
    o_gated = _sb_attn(qkv1, gate1.reshape(batch, seq, -1), tri)
    return _sb_out(o_gated, h1.reshape(batch, seq, D_MODEL), w_out1, sb_post_norm[0][None])
```

```python
import functools

import jax
import jax.numpy as jnp
from jax import lax
from jax.experimental import pallas as pl
from jax.experimental.pallas import tpu as pltpu

D_MODEL = 1024
N_META = 16
BLOCK = 128
META_PAD = BLOCK - N_META
HEAD_DIM = 64
ROPE_THETA = 10000.0
NORM_EPS = 1e-6
NEG_INF = -1e30
SWA_HEADS = 8
SWA_KV_HEADS = 2
SWA_GROUP = SWA_HEADS // SWA_KV_HEADS
SWA_WINDOW = 128
SWA_WIDTH = SWA_HEADS * HEAD_DIM
SWA_KV_WIDTH = SWA_KV_HEADS * HEAD_DIM
CONV_CHANNELS = 512
CONV_WIDTH = 31
CONV_LN_EPS = 1e-5
SB_HEADS = 16
SB_WIDTH = SB_HEADS * HEAD_DIM
AB_IN = 2 * SWA_WIDTH + 2 * SWA_KV_WIDTH + 3 * CONV_CHANNELS
AB_MIX = SWA_WIDTH + CONV_CHANNELS
SB_IN = 4 * SB_WIDTH

LANES = 128
SUBLANES = 8
ROW_TILE = 512
VMEM_LIMIT = 48 * 1024 * 1024
CONV_HALO = -(-(CONV_WIDTH - 1) // SUBLANES) * SUBLANES
CONV_WIN_START = BLOCK - CONV_HALO
CONV_WIN_ROWS = BLOCK + CONV_HALO
CONV_LEAD = CONV_HALO - (CONV_WIDTH - 1)

SB_PAIRS = SB_WIDTH // LANES
SB_Q_PER_STEP = 4
SB_SUM_GROUP = 8
SB_MASS_CUTOFF = 104.0
LOG2_E = 1.4426950408889634

F32 = jnp.float32
BF16 = jnp.bfloat16


def _rms_normed(x, gain):
    return x * lax.rsqrt(jnp.mean(x * x, axis=-1, keepdims=True) + NORM_EPS) * gain


def _silu(x):
    return x * jax.nn.sigmoid(x)


def _params(*semantics):
    return pltpu.CompilerParams(dimension_semantics=semantics, vmem_limit_bytes=VMEM_LIMIT)


def _ab_in_kernel(h_ref, gain_ref, w_ref, cos_ref, sin_ref, qkv_ref, rest_ref):
    hn = _rms_normed(h_ref[...], gain_ref[...]).astype(BF16)

    def proj(lo, hi):
        return jnp.dot(hn, w_ref[:, lo:hi], preferred_element_type=F32)

    cos = cos_ref[...]
    sin = sin_ref[...]
    lane = lax.broadcasted_iota(jnp.int32, cos.shape, 1)
    first_half = (lane % HEAD_DIM) < (HEAD_DIM // 2)

    def rope(t):
        rot = jnp.where(first_half, pltpu.roll(t, LANES - HEAD_DIM // 2, 1), pltpu.roll(t, HEAD_DIM // 2, 1))
        return t * cos + rot * sin

    q = proj(0, SWA_WIDTH)
    for c in range(SWA_WIDTH // LANES):
        cols = slice(c * LANES, (c + 1) * LANES)
        qkv_ref[:, cols] = (rope(q[:, cols]) * HEAD_DIM ** -0.5).astype(BF16)
    v_lo = SWA_WIDTH + SWA_KV_WIDTH
    kv = proj(SWA_WIDTH, v_lo + SWA_KV_WIDTH)
    qkv_ref[:, SWA_WIDTH:v_lo] = rope(kv[:, :SWA_KV_WIDTH]).astype(BF16)
    qkv_ref[:, v_lo:v_lo + SWA_KV_WIDTH] = kv[:, SWA_KV_WIDTH:].astype(BF16)

    ga_lo = v_lo + SWA_KV_WIDTH
    rest_ref[:, 0:SWA_WIDTH] = _silu(proj(ga_lo, ga_lo + SWA_WIDTH))
    glu_lo = ga_lo + SWA_WIDTH
    glu_a = proj(glu_lo, glu_lo + CONV_CHANNELS)
    glu_b = proj(glu_lo + CONV_CHANNELS, glu_lo + 2 * CONV_CHANNELS)
    rest_ref[:, SWA_WIDTH:SWA_WIDTH + CONV_CHANNELS] = glu_a * jax.nn.sigmoid(glu_b)
    gb_lo = glu_lo + 2 * CONV_CHANNELS
    rest_ref[:, SWA_WIDTH + CONV_CHANNELS:] = _silu(proj(gb_lo, gb_lo + CONV_CHANNELS))


def _ab_in(h_flat, gain, w_bf, cos_t, sin_t, row_tile, table_block):
    rows = h_flat.shape[0]
    qkv_w = SWA_WIDTH + 2 * SWA_KV_WIDTH
    rest_w = SWA_WIDTH + 2 * CONV_CHANNELS
    return pl.pallas_call(
        _ab_in_kernel,
        grid=(rows // row_tile,),
        in_specs=[
            pl.BlockSpec((row_tile, D_MODEL), lambda r: (r, 0)),
            pl.BlockSpec((1, D_MODEL), lambda r: (0, 0)),
            pl.BlockSpec((D_MODEL, AB_IN), lambda r: (0, 0)),
            pl.BlockSpec((row_tile, LANES), lambda r: (table_block(r), 0)),
            pl.BlockSpec((row_tile, LANES), lambda r: (table_block(r), 0)),
        ],
        out_specs=[
            pl.BlockSpec((row_tile, qkv_w), lambda r: (r, 0)),
            pl.BlockSpec((row_tile, rest_w), lambda r: (r, 0)),
        ],
        out_shape=[
            jax.ShapeDtypeStruct((rows, qkv_w), BF16),
            jax.ShapeDtypeStruct((rows, rest_w), F32),
        ],
        compiler_params=_params("parallel"),
        name="ab_in_proj",
    )(h_flat, gain, w_bf, cos_t, sin_t)


def _ab_mix_kernel(first_ref_blk, sinks_ref, qkv_ref, kv_prev_ref, kv_meta_ref, rest_ref, u_prev_ref, u_meta_ref,
                   conv_w_ref, conv_b_ref, ln_g_ref, ln_b_ref, w_pw2_ref,
                   out_ref, ext_ref, rot_ref, s_ref, p_ref):
    ref_blk = pl.program_id(1) + first_ref_blk
    prev_is_meta = ref_blk == 1
    qkv = qkv_ref[0]
    kv_meta = kv_meta_ref[0]
    kv_prev = jnp.where(prev_is_meta, kv_meta, kv_prev_ref[0])

    k_cat = jnp.concatenate([kv_meta[:, :LANES], kv_prev[:, :LANES], qkv[:, SWA_WIDTH:SWA_WIDTH + LANES]], axis=0)
    v_cat = jnp.concatenate([kv_meta[:, LANES:], kv_prev[:, LANES:], qkv[:, SWA_WIDTH + LANES:]], axis=0)
    k_swap = jnp.concatenate([k_cat[:, HEAD_DIM:], k_cat[:, :HEAD_DIM]], axis=1)
    v_swap = jnp.concatenate([v_cat[:, HEAD_DIM:], v_cat[:, :HEAD_DIM]], axis=1)

    row = lax.broadcasted_iota(jnp.int32, (BLOCK, BLOCK), 0)
    col = lax.broadcasted_iota(jnp.int32, (BLOCK, BLOCK), 1)
    qpos = ref_blk * BLOCK + row
    meta_ok = (col >= META_PAD) & (qpos - col >= SWA_WINDOW)
    ppos = (ref_blk - 1) * BLOCK + col
    prev_ok = (ppos >= META_PAD) & (qpos - ppos < SWA_WINDOW)
    cpos = ref_blk * BLOCK + col
    cur_ok = (cpos >= META_PAD) & (qpos >= cpos)
    mask = jnp.concatenate([meta_ok, prev_ok, cur_ok], axis=1)

    lane = lax.broadcasted_iota(jnp.int32, (BLOCK, LANES), 1)
    low_half = lane < HEAD_DIM
    zero_bf = jnp.zeros((), BF16)

    direct = [h for h in range(SWA_HEADS) if h // SWA_GROUP == h % 2]
    order = direct + [h for h in range(SWA_HEADS) if h not in direct]
    n_direct = len(direct)

    def head_q(head):
        q_pair = qkv[:, (head // 2) * LANES:(head // 2 + 1) * LANES]
        return jnp.where(low_half if head % 2 == 0 else ~low_half, q_pair, zero_bf)

    for slots, k_use in ((slice(0, n_direct), k_cat), (slice(n_direct, SWA_HEADS), k_swap)):
        q_stack = jnp.concatenate([head_q(h) for h in order[slots]], axis=0)
        s = lax.dot_general(q_stack, k_use, (((1,), (1,)), ((), ())), preferred_element_type=F32)
        s_ref[slots] = s.reshape(-1, BLOCK, 3 * BLOCK)
    inv_denoms = {}
    for slot, head in enumerate(order):
        s = jnp.where(mask, s_ref[slot], NEG_INF)
        sink = sinks_ref[head]
        m = jnp.maximum(jnp.max(s, axis=-1, keepdims=True), sink)
        p = jnp.exp(s - m)
        inv_denoms[head] = 1.0 / (jnp.sum(p, axis=-1, keepdims=True) + jnp.exp(sink - m))
        p_ref[slot] = p.astype(BF16)
    outs = {}
    for slots, v_use in ((slice(0, n_direct), v_cat), (slice(n_direct, SWA_HEADS), v_swap)):
        o = jnp.dot(p_ref[slots].reshape(-1, 3 * BLOCK), v_use, preferred_element_type=F32)
        for j, head in enumerate(order[slots]):
            outs[head] = o[j * BLOCK:(j + 1) * BLOCK] * inv_denoms[head]
    rest = rest_ref[0]
    for pair in range(SWA_HEADS // 2):
        cols = slice(pair * LANES, (pair + 1) * LANES)
        out_ref[0, :, cols] = (jnp.where(low_half, outs[2 * pair], outs[2 * pair + 1]) * rest[:, cols]).astype(BF16)

    u_prev = jnp.where(prev_is_meta, u_meta_ref[0], u_prev_ref[0])
    ext_ref[0:BLOCK, :] = jnp.where(ref_blk > 0, u_prev, 0.0)
    ext_ref[BLOCK:2 * BLOCK, :] = rest[:, SWA_WIDTH:SWA_WIDTH + CONV_CHANNELS]
    window = ext_ref[CONV_WIN_START:, :]
    for b in range(1, SUBLANES):
        rot_ref[b - 1] = pltpu.roll(window, CONV_WIN_ROWS - b, 0)
    conv_groups = []
    for cg in range(CONV_CHANNELS // LANES):
        cols = slice(cg * LANES, (cg + 1) * LANES)
        acc = jnp.zeros((BLOCK, LANES), F32) + conv_b_ref[:, cols]
        for w in range(CONV_WIDTH):
            a, b = divmod(CONV_LEAD + w, SUBLANES)
            if b == 0:
                taps = ext_ref[CONV_WIN_START + SUBLANES * a:CONV_WIN_START + SUBLANES * a + BLOCK, cols]
            else:
                taps = rot_ref[b - 1, SUBLANES * a:SUBLANES * a + BLOCK, cols]
            acc = acc + taps * conv_w_ref[w:w + 1, cols]
        conv_groups.append(acc)
    conv = jnp.concatenate(conv_groups, axis=1)

    mu = jnp.mean(conv, axis=-1, keepdims=True)
    xc = conv - mu
    ln = xc * lax.rsqrt(jnp.mean(xc * xc, axis=-1, keepdims=True) + CONV_LN_EPS) * ln_g_ref[...] + ln_b_ref[...]
    c_act = _silu(ln).astype(BF16)
    c_branch = jnp.dot(c_act, w_pw2_ref[...], preferred_element_type=F32) * rest[:, SWA_WIDTH + CONV_CHANNELS:]
    out_ref[0, :, SWA_WIDTH:] = c_branch.astype(BF16)


def _ab_mix(sinks, qkv, rest, qkv_meta, rest_meta, conv_w, conv_b, ln_g, ln_b, w_pw2, first_ref_blk):
    batch, rows, qkv_w = qkv.shape
    rest_w = rest.shape[-1]
    kv_blk = (SWA_WIDTH // (2 * SWA_KV_WIDTH))
    prev_blk = lambda i: jnp.maximum(i - 1, 0)
    const2 = lambda b, i: (0, 0)
    return pl.pallas_call(
        functools.partial(_ab_mix_kernel, first_ref_blk),
        grid=(batch, rows // BLOCK),
        in_specs=[
            pl.BlockSpec(memory_space=pltpu.SMEM),
            pl.BlockSpec((1, BLOCK, qkv_w), lambda b, i: (b, i, 0)),
            pl.BlockSpec((1, BLOCK, 2 * SWA_KV_WIDTH), lambda b, i: (b, prev_blk(i), kv_blk)),
            pl.BlockSpec((1, BLOCK, 2 * SWA_KV_WIDTH), lambda b, i: (0, 0, kv_blk)),
            pl.BlockSpec((1, BLOCK, rest_w), lambda b, i: (b, i, 0)),
            pl.BlockSpec((1, BLOCK, CONV_CHANNELS), lambda b, i: (b, prev_blk(i), 1)),
            pl.BlockSpec((1, BLOCK, CONV_CHANNELS), lambda b, i: (0, 0, 1)),
            pl.BlockSpec((CONV_WIDTH + 1, CONV_CHANNELS), const2),
            pl.BlockSpec((1, CONV_CHANNELS), const2),
            pl.BlockSpec((1, CONV_CHANNELS), const2),
            pl.BlockSpec((1, CONV_CHANNELS), const2),
            pl.BlockSpec((CONV_CHANNELS, CONV_CHANNELS), const2),
        ],
        out_specs=pl.BlockSpec((1, BLOCK, AB_MIX), lambda b, i: (b, i, 0)),
        out_shape=jax.ShapeDtypeStruct((batch, rows, AB_MIX), BF16),
        scratch_shapes=[
            pltpu.VMEM((2 * BLOCK, CONV_CHANNELS), F32),
            pltpu.VMEM((SUBLANES - 1, CONV_WIN_ROWS, CONV_CHANNELS), F32),
            pltpu.VMEM((SWA_HEADS, BLOCK, 3 * BLOCK), F32),
            pltpu.VMEM((SWA_HEADS, BLOCK, 3 * BLOCK), BF16),
        ],
        compiler_params=_params("parallel", "parallel"),
        name="ab_mix",
    )(sinks, qkv, qkv, qkv_meta, rest, rest, rest_meta, conv_w, conv_b, ln_g, ln_b, w_pw2)


def _mid_kernel(mix_ref, h_ref, w_out_ref, post_ref, gain_ref, w_in_ref, h1_ref, qkv_ref, gate_ref):
    y = jnp.dot(mix_ref[...], w_out_ref[...], preferred_element_type=F32)
    h1 = h_ref[...] + _rms_normed(y, post_ref[...])
    h1_ref[...] = h1
    hn = _rms_normed(h1, gain_ref[...]).astype(BF16)
    chunk = 4 * LANES
    even_head = (lax.broadcasted_iota(jnp.int32, (hn.shape[0], chunk), 1) % LANES) < HEAD_DIM

    def proj(lo):
        return jnp.dot(hn, w_in_ref[:, lo:lo + chunk], preferred_element_type=F32)

    for c in range(SB_WIDTH // chunk):
        lo = c * chunk
        qkv_ref[0, :, lo:lo + chunk] = (proj(lo) * HEAD_DIM ** -0.5).astype(BF16)
        qkv_ref[0, :, SB_WIDTH + lo:SB_WIDTH + lo + chunk] = proj(SB_WIDTH + lo).astype(BF16)
        v = proj(2 * SB_WIDTH + lo)
        qkv_ref[0, :, 2 * SB_WIDTH + lo:2 * SB_WIDTH + lo + chunk] = jnp.where(even_head, v, 0.0).astype(BF16)
        qkv_ref[0, :, 3 * SB_WIDTH + lo:3 * SB_WIDTH + lo + chunk] = jnp.where(even_head, 0.0, v).astype(BF16)
        gate_ref[:, lo:lo + chunk] = _silu(proj(3 * SB_WIDTH + lo))


def _mid_tail_kernel(n_tiles, mix_ref, h_ref, w_out_ref, post_ref, gain_ref, w_in_ref, tail_ref,
                     h1_ref, qkv_ref, gate_ref):
    @pl.when(pl.program_id(0) < n_tiles)
    def _():
        _mid_kernel(mix_ref, h_ref, w_out_ref, post_ref, gain_ref, w_in_ref, h1_ref, qkv_ref, gate_ref)

    @pl.when(pl.program_id(0) >= n_tiles)
    def _():
        qkv_ref[0, 0:tail_ref.shape[1], :] = tail_ref[0]


def _mid(mix_flat, h_flat, w_out_bf, post, gain, w_in_bf, row_tile, tiles_per_seq, qkv_tail=None):
    rows = h_flat.shape[0]
    n_tiles = rows // row_tile
    n_seq = n_tiles // tiles_per_seq
    seq_rows = tiles_per_seq * row_tile
    const = lambda shape: pl.BlockSpec(shape, lambda r: (0,) * len(shape))
    weights = [const((AB_MIX, D_MODEL)), const((1, D_MODEL)), const((1, D_MODEL)), const((D_MODEL, SB_IN))]
    if qkv_tail is None:
        tile = lambda r: r
        qkv_index = lambda r: (r // tiles_per_seq, r % tiles_per_seq, 0)
        body, steps, tail_specs, tail_args, tail_rows = _mid_kernel, n_tiles, [], (), 0
    else:
        tile = lambda r: jnp.minimum(r, n_tiles - 1)
        qkv_index = lambda r: (jnp.where(r < n_tiles, r // tiles_per_seq, r - n_tiles),
                               jnp.where(r < n_tiles, r % tiles_per_seq, tiles_per_seq), 0)
        body, steps = functools.partial(_mid_tail_kernel, n_tiles), n_tiles + n_seq
        tail_specs, tail_args, tail_rows = [const(qkv_tail.shape)], (qkv_tail,), qkv_tail.shape[1]
    row_spec = lambda width: pl.BlockSpec((row_tile, width), lambda r: (tile(r), 0))
    return pl.pallas_call(
        body,
        grid=(steps,),
        in_specs=[row_spec(AB_MIX), row_spec(D_MODEL)] + weights + tail_specs,
        out_specs=[row_spec(D_MODEL), pl.BlockSpec((1, row_tile, 4 * SB_WIDTH), qkv_index), row_spec(SB_WIDTH)],
        out_shape=[
            jax.ShapeDtypeStruct((rows, D_MODEL), F32),
            jax.ShapeDtypeStruct((n_seq, seq_rows + tail_rows, 4 * SB_WIDTH), BF16),
            jax.ShapeDtypeStruct((rows, SB_WIDTH), F32),
        ],
        compiler_params=_params("arbitrary"),
        name="ab_out_sb_in_proj",
    )(mix_flat, h_flat, w_out_bf, post, gain, w_in_bf, *tail_args)


def _sb_attn_kernel(q_ref, *refs):
    def one_query_block(sub, _):
        q_blk = pl.program_id(1) * SB_Q_PER_STEP + sub + 1
        _sb_query_block(q_blk, pl.ds(pl.multiple_of(sub * BLOCK, BLOCK), BLOCK), q_ref, *refs)
        return 0

    lax.fori_loop(0, SB_Q_PER_STEP, one_query_block, 0)


def _sb_query_block(q_blk, q_rows, q_ref, k_ref, v_even_ref, v_odd_ref, gate_ref, tri_ref, out_ref,
                    carry_ref, acc_ref, qh_ref, z_ref, split_ref, sums_ref, w_ref, bias_ref):
    meta_off = k_ref.shape[1] - BLOCK
    lane = lax.broadcasted_iota(jnp.int32, (BLOCK, LANES), 1)
    row = lax.broadcasted_iota(jnp.int32, (BLOCK, LANES), 0)
    low_half = lane < HEAD_DIM
    zero_bf = jnp.zeros((), BF16)

    qpos = q_blk * BLOCK + row
    carry_ref[...] = jnp.zeros_like(carry_ref)
    acc_ref[...] = jnp.zeros_like(acc_ref)
    for n in range(2 * SB_PAIRS):
        q_pair = q_ref[0, q_rows, (n // 2) * LANES:(n // 2 + 1) * LANES]
        qh_ref[n] = jnp.where(low_half, q_pair, zero_bf) if n % 2 == 0 else jnp.where(low_half, zero_bf, q_pair)

    def visit(k_blk):
        off = pl.multiple_of(jnp.where(k_blk == 0, meta_off, (k_blk - 1) * BLOCK), BLOCK)
        kpos = k_blk * BLOCK + lane

        def scores(p):
            k_pair = k_ref[0, pl.ds(off, BLOCK), p * LANES:(p + 1) * LANES]
            q_both = qh_ref[2 * p:2 * p + 2].reshape(2 * BLOCK, LANES)
            z = lax.dot_general(q_both, k_pair, (((1,), (1,)), ((), ())), preferred_element_type=F32)
            z_ref[2 * p:2 * p + 2] = z.reshape(2, BLOCK, LANES)

        bias_ref[...] = jnp.where((kpos >= META_PAD) & (kpos < qpos), 0.0, NEG_INF)

        def softplus_split(n):
            z = z_ref[n] + bias_ref[...]
            z_ref[n] = z - carry_ref[n]
            sp = jnp.maximum(z, 0.0) + jnp.log(1.0 + jnp.exp2(jnp.abs(z) * -LOG2_E))
            hi = sp.astype(BF16)
            split_ref[n, :, 0:LANES] = hi
            split_ref[n, :, LANES:] = (sp - hi.astype(F32)).astype(BF16)

        def suffix_sums(g):
            heads_g = slice(g * SB_SUM_GROUP, (g + 1) * SB_SUM_GROUP)
            sums = jnp.dot(split_ref[heads_g].reshape(SB_SUM_GROUP * BLOCK, 2 * LANES), tri_ref[...],
                           preferred_element_type=F32)
            sums_ref[heads_g] = sums.reshape(SB_SUM_GROUP, BLOCK, 2 * LANES)

        def carry_update(n, least):
            new_carry = carry_ref[n] + sums_ref[n, :, LANES:]
            carry_ref[n] = new_carry
            return new_carry if least is None else jnp.minimum(least, new_carry)

        def weights(n):
            w = jnp.exp(z_ref[n] - sums_ref[n, :, 0:LANES])
            w_ref[n // 2, :, (n % 2) * LANES:(n % 2 + 1) * LANES] = w.astype(BF16)

        def accumulate(p):
            cols = slice(p * LANES, (p + 1) * LANES)
            v_bd = jnp.concatenate([v_even_ref[0, pl.ds(off, BLOCK), cols], v_odd_ref[0, pl.ds(off, BLOCK), cols]], axis=0)
            acc_ref[p] += jnp.dot(w_ref[p], v_bd, preferred_element_type=F32)

        pairs, heads = range(SB_PAIRS), range(2 * SB_PAIRS)
        for p in pairs:
            scores(p)
        for n in heads:
            softplus_split(n)
        for g in range(2 * SB_PAIRS // SB_SUM_GROUP):
            suffix_sums(g)
        least = None
        for n in heads:
            least = carry_update(n, least)
        least_mass = jnp.min(least)
        for n in heads:
            weights(n)
        for p in pairs:
            accumulate(p)
        return least_mass

    def keep_going(state):
        k_blk, least_mass = state
        return (k_blk >= 0) & (least_mass < SB_MASS_CUTOFF)

    def body(state):
        k_blk, _ = state
        return k_blk - 1, visit(k_blk)

    lax.while_loop(keep_going, body, (q_blk, jnp.float32(0.0)))

    for p in range(SB_PAIRS):
        cols = slice(p * LANES, (p + 1) * LANES)
        out_ref[0, q_rows, cols] = (acc_ref[p] * gate_ref[0, q_rows, cols]).astype(BF16)


def _sb_attn(qkv, gate, tri):
    batch, lp, _ = qkv.shape
    seq = lp - BLOCK
    q_tile = SB_Q_PER_STEP * BLOCK
    seq_cols = lambda col: pl.BlockSpec((1, lp, SB_WIDTH), lambda b, i: (b, 0, col))
    return pl.pallas_call(
        _sb_attn_kernel,
        grid=(batch, seq // q_tile),
        in_specs=[
            pl.BlockSpec((1, q_tile, SB_WIDTH), lambda b, i: (b, i, 0)),
            seq_cols(1), seq_cols(2), seq_cols(3),
            pl.BlockSpec((1, q_tile, SB_WIDTH), lambda b, i: (b, i, 0)),
            pl.BlockSpec((2 * BLOCK, 2 * LANES), lambda b, i: (0, 0)),
        ],
        out_specs=pl.BlockSpec((1, q_tile, SB_WIDTH), lambda b, i: (b, i, 0)),
        out_shape=jax.ShapeDtypeStruct((batch, seq, SB_WIDTH), BF16),
        scratch_shapes=[
            pltpu.VMEM((2 * SB_PAIRS, BLOCK, LANES), F32),
            pltpu.VMEM((SB_PAIRS, BLOCK, LANES), F32),
            pltpu.VMEM((2 * SB_PAIRS, BLOCK, LANES), BF16),
            pltpu.VMEM((2 * SB_PAIRS, BLOCK, LANES), F32),
            pltpu.VMEM((2 * SB_PAIRS, BLOCK, 2 * LANES), BF16),
            pltpu.VMEM((2 * SB_PAIRS, BLOCK, 2 * LANES), F32),
            pltpu.VMEM((SB_PAIRS, BLOCK, 2 * LANES), BF16),
            pltpu.VMEM((BLOCK, LANES), F32),
        ],
        compiler_params=_params("parallel", "arbitrary"),
        name="sb_attn",
    )(qkv, qkv, qkv, qkv, gate, tri)


def _sb_out_kernel(o_ref, h_ref, w_ref, post_ref, out_ref):
    y = jnp.dot(o_ref[0], w_ref[...], preferred_element_type=F32)
    out_ref[0] = h_ref[0] + _rms_normed(y, post_ref[...])


def _sb_out(o_gated, h, w_bf, post):
    batch, rows, _ = o_gated.shape
    return pl.pallas_call(
        _sb_out_kernel,
        grid=(batch, rows // ROW_TILE),
        in_specs=[
            pl.BlockSpec((1, ROW_TILE, SB_WIDTH), lambda b, i: (b, i, 0)),
            pl.BlockSpec((1, ROW_TILE, D_MODEL), lambda b, i: (b, i, 0)),
            pl.BlockSpec((SB_WIDTH, D_MODEL), lambda b, i: (0, 0)),
            pl.BlockSpec((1, D_MODEL), lambda b, i: (0, 0)),
        ],
        out_specs=pl.BlockSpec((1, ROW_TILE, D_MODEL), lambda b, i: (b, i, 0)),
        out_shape=jax.ShapeDtypeStruct((batch, rows, D_MODEL), F32),
        compiler_params=_params("parallel", "parallel"),
        name="sb_out_proj",
    )(o_gated, h, w_bf, post)


def _rope_tables(seq):
    half = HEAD_DIM // 2
    pos = jnp.concatenate([N_META + jnp.arange(seq), jnp.zeros((META_PAD,), jnp.int32), jnp.arange(N_META)])
    inv = ROPE_THETA ** (-jnp.arange(half, dtype=jnp.float32) / half)
    ang = pos.astype(jnp.float32)[:, None] * inv[None, :]
    cos = jnp.cos(ang)
    sin = jnp.sin(ang)
    cos_t = jnp.tile(jnp.concatenate([cos, cos], axis=-1), (1, LANES // HEAD_DIM))
    sin_t = jnp.tile(jnp.concatenate([-sin, sin], axis=-1), (1, LANES // HEAD_DIM))
    return cos_t, sin_t


def kernel(x, meta_tokens, ab_pre_norm, ab_w_in, ab_sinks, ab_conv_w, ab_conv_b, ab_conv_ln_g, ab_conv_ln_b, ab_w_pw2, ab_w_out, ab_post_norm, sb_pre_norm, sb_w_in, sb_w_out, sb_post_norm):
    batch, seq, _ = x.shape
    assert seq % ROW_TILE == 0
    tiles_per_seq = seq // ROW_TILE
    n_blocks = seq // BLOCK

    x_flat = x.reshape(batch * seq, D_MODEL)
    meta_blk = jnp.concatenate([jnp.zeros((META_PAD, D_MODEL), x.dtype), meta_tokens.astype(x.dtype)], axis=0)
    cos_t, sin_t = _rope_tables(seq)
    pre0, post0, pre1 = ab_pre_norm[0][None], ab_post_norm[0][None], sb_pre_norm[0][None]
    w_in0, w_pw2, w_out0 = ab_w_in[0].astype(BF16), ab_w_pw2[0].astype(BF16), ab_w_out[0].astype(BF16)
    w_in1, w_out1 = sb_w_in[0].astype(BF16), sb_w_out[0].astype(BF16)

    qkv, rest = _ab_in(x_flat, pre0, w_in0, cos_t, sin_t, ROW_TILE, lambda r: r % tiles_per_seq)
    qkv_m, rest_m = _ab_in(meta_blk, pre0, w_in0, cos_t, sin_t, BLOCK, lambda r: n_blocks)
    qkv_m, rest_m = qkv_m[None], rest_m[None]
    conv_w = jnp.concatenate([ab_conv_w[0], jnp.zeros((1, CONV_CHANNELS), F32)], axis=0)
    mix_args = (conv_w, ab_conv_b[0][None], ab_conv_ln_g[0][None], ab_conv_ln_b[0][None], w_pw2)
    mix = _ab_mix(ab_sinks[0], qkv.reshape(batch, seq, -1), rest.reshape(batch, seq, -1), qkv_m, rest_m, *mix_args, 1)
    mix_m = _ab_mix(ab_sinks[0], qkv_m, rest_m, qkv_m, rest_m, *mix_args, 0)

    mid_args = (w_out0, post0, pre1, w_in1)
    _, qkv1_m, _ = _mid(mix_m[0], meta_blk, *mid_args, BLOCK, 1)
    h1, qkv1, gate1 = _mid(mix.reshape(batch * seq, AB_MIX), x_flat, *mid_args, ROW_TILE, tiles_per_seq, qkv1_m)

    r = jnp.arange(BLOCK)
    tri = jnp.concatenate([(r[:, None] >= r[None, :]).astype(BF16), jnp.ones((BLOCK, LANES), BF16)], axis=1)
    tri = jnp.concatenate([tri, tri], axis=0)
    o_gated = _sb_attn(qkv1, gate1.reshape(batch, seq, -1), tri)
    return _sb_out(o_gated, h1.reshape(batch, seq, D_MODEL), w_out1, sb_post_norm[0][None])
```

```python
import functools

import jax
import jax.numpy as jnp
from jax import lax
from jax.experimental import pallas as pl
from jax.experimental.pallas import tpu as pltpu

D_MODEL = 1024
N_META = 16
BLOCK = 128
META_PAD = BLOCK - N_META
HEAD_DIM = 64
ROPE_THETA = 10000.0
NORM_EPS = 1e-6
NEG_INF = -1e30
SWA_HEADS = 8
SWA_KV_HEADS = 2
SWA_GROUP = SWA_HEADS // SWA_KV_HEADS
SWA_WINDOW = 128
SWA_WIDTH = SWA_HEADS * HEAD_DIM
SWA_KV_WIDTH = SWA_KV_HEADS * HEAD_DIM
CONV_CHANNELS = 512
CONV_WIDTH = 31
CONV_LN_EPS = 1e-5
SB_HEADS = 16
SB_WIDTH = SB_HEADS * HEAD_DIM
AB_IN = 2 * SWA_WIDTH + 2 * SWA_KV_WIDTH + 3 * CONV_CHANNELS
AB_MIX = SWA_WIDTH + CONV_CHANNELS
SB_IN = 4 * SB_WIDTH

LANES = 128
SUBLANES = 8
ROW_TILE = 512
VMEM_LIMIT = 48 * 1024 * 1024
CONV_HALO = -(-(CONV_WIDTH - 1) // SUBLANES) * SUBLANES
CONV_WIN_START = BLOCK - CONV_HALO
CONV_WIN_ROWS = BLOCK + CONV_HALO
CONV_LEAD = CONV_HALO - (CONV_WIDTH - 1)

SB_PAIRS = SB_WIDTH // LANES
SB_Q_PER_STEP = 4
SB_SUM_GROUP = 8
SB_MASS_CUTOFF = 104.0
LOG2_E = 1.4426950408889634

F32 = jnp.float32
BF16 = jnp.bfloat16


def _rms_normed(x, gain):
    return x * lax.rsqrt(jnp.mean(x * x, axis=-1, keepdims=True) + NORM_EPS) * gain


def _silu(x):
    return x * jax.nn.sigmoid(x)


def _params(*semantics):
    return pltpu.CompilerParams(dimension_semantics=semantics, vmem_limit_bytes=VMEM_LIMIT)


def _ab_in_kernel(h_ref, gain_ref, w_ref, cos_ref, sin_ref, qkv_ref, rest_ref):
    hn = _rms_normed(h_ref[...], gain_ref[...]).astype(BF16)

    def proj(lo, hi):
        return jnp.dot(hn, w_ref[:, lo:hi], preferred_element_type=F32)

    cos = cos_ref[...]
    sin = sin_ref[...]
    lane = lax.broadcasted_iota(jnp.int32, cos.shape, 1)
    first_half = (lane % HEAD_DIM) < (HEAD_DIM // 2)

    def rope(t):
        rot = jnp.where(first_half, pltpu.roll(t, LANES - HEAD_DIM // 2, 1), pltpu.roll(t, HEAD_DIM // 2, 1))
        return t * cos + rot * sin

    q = proj(0, SWA_WIDTH)
    for c in range(SWA_WIDTH // LANES):
        cols = slice(c * LANES, (c + 1) * LANES)
        qkv_ref[:, cols] = (rope(q[:, cols]) * HEAD_DIM ** -0.5).astype(BF16)
    v_lo = SWA_WIDTH + SWA_KV_WIDTH
    kv = proj(SWA_WIDTH, v_lo + SWA_KV_WIDTH)
    qkv_ref[:, SWA_WIDTH:v_lo] = rope(kv[:, :SWA_KV_WIDTH]).astype(BF16)
    qkv_ref[:, v_lo:v_lo + SWA_KV_WIDTH] = kv[:, SWA_KV_WIDTH:].astype(BF16)

    ga_lo = v_lo + SWA_KV_WIDTH
    rest_ref[:, 0:SWA_WIDTH] = _silu(proj(ga_lo, ga_lo + SWA_WIDTH))
    glu_lo = ga_lo + SWA_WIDTH
    glu_a = proj(glu_lo, glu_lo + CONV_CHANNELS)
    glu_b = proj(glu_lo + CONV_CHANNELS, glu_lo + 2 * CONV_CHANNELS)
    rest_ref[:, SWA_WIDTH:SWA_WIDTH + CONV_CHANNELS] = glu_a * jax.nn.sigmoid(glu_b)
    gb_lo = glu_lo + 2 * CONV_CHANNELS
    rest_ref[:, SWA_WIDTH + CONV_CHANNELS:] = _silu(proj(gb_lo, gb_lo + CONV_CHANNELS))


def _ab_in(h_flat, gain, w_bf, cos_t, sin_t, row_tile, table_block):
    rows = h_flat.shape[0]
    qkv_w = SWA_WIDTH + 2 * SWA_KV_WIDTH
    rest_w = SWA_WIDTH + 2 * CONV_CHANNELS
    return pl.pallas_call(
        _ab_in_kernel,
        grid=(rows // row_tile,),
        in_specs=[
            pl.BlockSpec((row_tile, D_MODEL), lambda r: (r, 0)),
            pl.BlockSpec((1, D_MODEL), lambda r: (0, 0)),
            pl.BlockSpec((D_MODEL, AB_IN), lambda r: (0, 0)),
            pl.BlockSpec((row_tile, LANES), lambda r: (table_block(r), 0)),
            pl.BlockSpec((row_tile, LANES), lambda r: (table_block(r), 0)),
        ],
        out_specs=[
            pl.BlockSpec((row_tile, qkv_w), lambda r: (r, 0)),
            pl.BlockSpec((row_tile, rest_w), lambda r: (r, 0)),
        ],
        out_shape=[
            jax.ShapeDtypeStruct((rows, qkv_w), BF16),
            jax.ShapeDtypeStruct((rows, rest_w), F32),
        ],
        compiler_params=_params("parallel"),
        name="ab_in_proj",
    )(h_flat, gain, w_bf, cos_t, sin_t)


def _ab_mix_kernel(first_ref_blk, sinks_ref, qkv_ref, kv_prev_ref, kv_meta_ref, rest_ref, u_prev_ref, u_meta_ref,
                   conv_w_ref, conv_b_ref, ln_g_ref, ln_b_ref, w_pw2_ref,
                   out_ref, ext_ref, rot_ref, s_ref, p_ref):
    ref_blk = pl.program_id(1) + first_ref_blk
    prev_is_meta = ref_blk == 1
    qkv = qkv_ref[0]
    kv_meta = kv_meta_ref[0]
    kv_prev = jnp.where(prev_is_meta, kv_meta, kv_prev_ref[0])

    k_cat = jnp.concatenate([kv_meta[:, :LANES], kv_prev[:, :LANES], qkv[:, SWA_WIDTH:SWA_WIDTH + LANES]], axis=0)
    v_cat = jnp.concatenate([kv_meta[:, LANES:], kv_prev[:, LANES:], qkv[:, SWA_WIDTH + LANES:]], axis=0)
    k_swap = jnp.concatenate([k_cat[:, HEAD_DIM:], k_cat[:, :HEAD_DIM]], axis=1)
    v_swap = jnp.concatenate([v_cat[:, HEAD_DIM:], v_cat[:, :HEAD_DIM]], axis=1)

    row = lax.broadcasted_iota(jnp.int32, (BLOCK, BLOCK), 0)
    col = lax.broadcasted_iota(jnp.int32, (BLOCK, BLOCK), 1)
    qpos = ref_blk * BLOCK + row
    meta_ok = (col >= META_PAD) & (qpos - col >= SWA_WINDOW)
    ppos = (ref_blk - 1) * BLOCK + col
    prev_ok = (ppos >= META_PAD) & (qpos - ppos < SWA_WINDOW)
    cpos = ref_blk * BLOCK + col
    cur_ok = (cpos >= META_PAD) & (qpos >= cpos)
    mask = jnp.concatenate([meta_ok, prev_ok, cur_ok], axis=1)

    lane = lax.broadcasted_iota(jnp.int32, (BLOCK, LANES), 1)
    low_half = lane < HEAD_DIM
    zero_bf = jnp.zeros((), BF16)

    direct = [h for h in range(SWA_HEADS) if h // SWA_GROUP == h % 2]
    order = direct + [h for h in range(SWA_HEADS) if h not in direct]
    n_direct = len(direct)

    def head_q(head):
        q_pair = qkv[:, (head // 2) * LANES:(head // 2 + 1) * LANES]
        return jnp.where(low_half if head % 2 == 0 else ~low_half, q_pair, zero_bf)

    for slots, k_use in ((slice(0, n_direct), k_cat), (slice(n_direct, SWA_HEADS), k_swap)):
        q_stack = jnp.concatenate([head_q(h) for h in order[slots]], axis=0)
        s = lax.dot_general(q_stack, k_use, (((1,), (1,)), ((), ())), preferred_element_type=F32)
        s_ref[slots] = s.reshape(-1, BLOCK, 3 * BLOCK)
    inv_denoms = {}
    for slot, head in enumerate(order):
        s = jnp.where(mask, s_ref[slot], NEG_INF)
        sink = sinks_ref[head]
        m = jnp.maximum(jnp.max(s, axis=-1, keepdims=True), sink)
        p = jnp.exp(s - m)
        inv_denoms[head] = 1.0 / (jnp.sum(p, axis=-1, keepdims=True) + jnp.exp(sink - m))
        p_ref[slot] = p.astype(BF16)
    outs = {}
    for slots, v_use in ((slice(0, n_direct), v_cat), (slice(n_direct, SWA_HEADS), v_swap)):
        o = jnp.dot(p_ref[slots].reshape(-1, 3 * BLOCK), v_use, preferred_element_type=F32)
        for j, head in enumerate(order[slots]):
            outs[head] = o[j * BLOCK:(j + 1) * BLOCK] * inv_denoms[head]
    rest = rest_ref[0]
    for pair in range(SWA_HEADS // 2):
        cols = slice(pair * LANES, (pair + 1) * LANES)
        out_ref[0, :, cols] = (jnp.where(low_half, outs[2 * pair], outs[2 * pair + 1]) * rest[:, cols]).astype(BF16)

    u_prev = jnp.where(prev_is_meta, u_meta_ref[0], u_prev_ref[0])
    ext_ref[0:BLOCK, :] = jnp.where(ref_blk > 0, u_prev, 0.0)
    ext_ref[BLOCK:2 * BLOCK, :] = rest[:, SWA_WIDTH:SWA_WIDTH + CONV_CHANNELS]
    window = ext_ref[CONV_WIN_START:, :]
    for b in range(1, SUBLANES):
        rot_ref[b - 1] = pltpu.roll(window, CONV_WIN_ROWS - b, 0)
    conv_groups = []
    for cg in range(CONV_CHANNELS // LANES):
        cols = slice(cg * LANES, (cg + 1) * LANES)
        acc = jnp.zeros((BLOCK, LANES), F32) + conv_b_ref[:, cols]
        for w in range(CONV_WIDTH):
            a, b = divmod(CONV_LEAD + w, SUBLANES)
            if b == 0:
                taps = ext_ref[CONV_WIN_START + SUBLANES * a:CONV_WIN_START + SUBLANES * a + BLOCK, cols]
            else:
                taps = rot_ref[b - 1, SUBLANES * a:SUBLANES * a + BLOCK, cols]
            acc = acc + taps * conv_w_ref[w:w + 1, cols]
        conv_groups.append(acc)
    conv = jnp.concatenate(conv_groups, axis=1)

    mu = jnp.mean(conv, axis=-1, keepdims=True)
    xc = conv - mu
    ln = xc * lax.rsqrt(jnp.mean(xc * xc, axis=-1, keepdims=True) + CONV_LN_EPS) * ln_g_ref[...] + ln_b_ref[...]
    c_act = _silu(ln).astype(BF16)
    c_branch = jnp.dot(c_act, w_pw2_ref[...], preferred_element_type=F32) * rest[:, SWA_WIDTH + CONV_CHANNELS:]
    out_ref[0, :, SWA_WIDTH:] = c_branch.astype(BF16)


def _ab_mix(sinks, qkv, rest, qkv_meta, rest_meta, conv_w, conv_b, ln_g, ln_b, w_pw2, first_ref_blk):
    batch, rows, qkv_w = qkv.shape
    rest_w = rest.shape[-1]
    kv_blk = (SWA_WIDTH // (2 * SWA_KV_WIDTH))
    prev_blk = lambda i: jnp.maximum(i - 1, 0)
    const2 = lambda b, i: (0, 0)
    return pl.pallas_call(
        functools.partial(_ab_mix_kernel, first_ref_blk),
        grid=(batch, rows // BLOCK),
        in_specs=[
            pl.BlockSpec(memory_space=pltpu.SMEM),
            pl.BlockSpec((1, BLOCK, qkv_w), lambda b, i: (b, i, 0)),
            pl.BlockSpec((1, BLOCK, 2 * SWA_KV_WIDTH), lambda b, i: (b, prev_blk(i), kv_blk)),
            pl.BlockSpec((1, BLOCK, 2 * SWA_KV_WIDTH), lambda b, i: (0, 0, kv_blk)),
            pl.BlockSpec((1, BLOCK, rest_w), lambda b, i: (b, i, 0)),
            pl.BlockSpec((1, BLOCK, CONV_CHANNELS), lambda b, i: (b, prev_blk(i), 1)),
            pl.BlockSpec((1, BLOCK, CONV_CHANNELS), lambda b, i: (0, 0, 1)),
            pl.BlockSpec((CONV_WIDTH + 1, CONV_CHANNELS), const2),
            pl.BlockSpec((1, CONV_CHANNELS), const2),
            pl.BlockSpec((1, CONV_CHANNELS), const2),
            pl.BlockSpec((1, CONV_CHANNELS), const2),
            pl.BlockSpec((CONV_CHANNELS, CONV_CHANNELS), const2),
        ],
        out_specs=pl.BlockSpec((1, BLOCK, AB_MIX), lambda b, i: (b, i, 0)),
        out_shape=jax.ShapeDtypeStruct((batch, rows, AB_MIX), BF16),
        scratch_shapes=[
            pltpu.VMEM((2 * BLOCK, CONV_CHANNELS), F32),
            pltpu.VMEM((SUBLANES - 1, CONV_WIN_ROWS, CONV_CHANNELS), F32),
            pltpu.VMEM((SWA_HEADS, BLOCK, 3 * BLOCK), F32),
            pltpu.VMEM((SWA_HEADS, BLOCK, 3 * BLOCK), BF16),
        ],
        compiler_params=_params("parallel", "parallel"),
        name="ab_mix",
    )(sinks, qkv, qkv, qkv_meta, rest, rest, rest_meta, conv_w, conv_b, ln_g, ln_b, w_pw2)


def _mid_kernel(mix_ref, h_ref, w_out_ref, post_ref, gain_ref, w_in_ref, h1_ref, qkv_ref, gate_ref):
    y = jnp.dot(mix_ref[...], w_out_ref[...], preferred_element_type=F32)
    h1 = h_ref[...] + _rms_normed(y, post_ref[...])
    h1_ref[...] = h1
    hn = _rms_normed(h1, gain_ref[...]).astype(BF16)
    chunk = 4 * LANES

    def proj(lo):
        return jnp.dot(hn, w_in_ref[:, lo:lo + chunk], preferred_element_type=F32)

    for c in range(SB_WIDTH // chunk):
        lo = c * chunk
        qkv_ref[0, :, lo:lo + chunk] = (proj(lo) * HEAD_DIM ** -0.5).astype(BF16)
    for c in range(SB_WIDTH // chunk, 3 * SB_WIDTH // chunk):
        lo = c * chunk
        qkv_ref[0, :, lo:lo + chunk] = proj(lo).astype(BF16)
    for c in range(SB_WIDTH // chunk):
        lo = c * chunk
        gate_ref[:, lo:lo + chunk] = _silu(proj(3 * SB_WIDTH + lo))


def _mid_tail_kernel(n_tiles, mix_ref, h_ref, w_out_ref, post_ref, gain_ref, w_in_ref, tail_ref,
                     h1_ref, qkv_ref, gate_ref):
    @pl.when(pl.program_id(0) < n_tiles)
    def _():
        _mid_kernel(mix_ref, h_ref, w_out_ref, post_ref, gain_ref, w_in_ref, h1_ref, qkv_ref, gate_ref)

    @pl.when(pl.program_id(0) >= n_tiles)
    def _():
        qkv_ref[0, 0:tail_ref.shape[1], :] = tail_ref[0]


def _mid(mix_flat, h_flat, w_out_bf, post, gain, w_in_bf, row_tile, tiles_per_seq, qkv_tail=None):
    rows = h_flat.shape[0]
    n_tiles = rows // row_tile
    n_seq = n_tiles // tiles_per_seq
    seq_rows = tiles_per_seq * row_tile
    const = lambda shape: pl.BlockSpec(shape, lambda r: (0,) * len(shape))
    weights = [const((AB_MIX, D_MODEL)), const((1, D_MODEL)), const((1, D_MODEL)), const((D_MODEL, SB_IN))]
    if qkv_tail is None:
        tile = lambda r: r
        qkv_index = lambda r: (r // tiles_per_seq, r % tiles_per_seq, 0)
        body, steps, tail_specs, tail_args, tail_rows = _mid_kernel, n_tiles, [], (), 0
    else:
        tile = lambda r: jnp.minimum(r, n_tiles - 1)
        qkv_index = lambda r: (jnp.where(r < n_tiles, r // tiles_per_seq, r - n_tiles),
                               jnp.where(r < n_tiles, r % tiles_per_seq, tiles_per_seq), 0)
        body, steps = functools.partial(_mid_tail_kernel, n_tiles), n_tiles + n_seq
        tail_specs, tail_args, tail_rows = [const(qkv_tail.shape)], (qkv_tail,), qkv_tail.shape[1]
    row_spec = lambda width: pl.BlockSpec((row_tile, width), lambda r: (tile(r), 0))
    return pl.pallas_call(
        body,
        grid=(steps,),
        in_specs=[row_spec(AB_MIX), row_spec(D_MODEL)] + weights + tail_specs,
        out_specs=[row_spec(D_MODEL), pl.BlockSpec((1, row_tile, 3 * SB_WIDTH), qkv_index), row_spec(SB_WIDTH)],
        out_shape=[
            jax.ShapeDtypeStruct((rows, D_MODEL), F32),
            jax.ShapeDtypeStruct((n_seq, seq_rows + tail_rows, 3 * SB_WIDTH), BF16),
            jax.ShapeDtypeStruct((rows, SB_WIDTH), F32),
        ],
        compiler_params=_params("arbitrary"),
        name="ab_out_sb_in_proj",
    )(mix_flat, h_flat, w_out_bf, post, gain, w_in_bf, *tail_args)


def _sb_attn_kernel(q_ref, *refs):
    def one_query_block(sub, _):
        q_blk = pl.program_id(1) * SB_Q_PER_STEP + sub + 1
        _sb_query_block(q_blk, pl.ds(pl.multiple_of(sub * BLOCK, BLOCK), BLOCK), q_ref, *refs)
        return 0

    lax.fori_loop(0, SB_Q_PER_STEP, one_query_block, 0)


def _sb_query_block(q_blk, q_rows, q_ref, k_ref, v_ref, gate_ref, tri_ref, out_ref,
                    carry_ref, acc_ref, qh_ref, z_ref, split_ref, sums_ref, w_ref, bias_ref):
    meta_off = k_ref.shape[1] - BLOCK
    lane = lax.broadcasted_iota(jnp.int32, (BLOCK, LANES), 1)
    row = lax.broadcasted_iota(jnp.int32, (BLOCK, LANES), 0)
    low_half = lane < HEAD_DIM
    zero_bf = jnp.zeros((), BF16)

    qpos = q_blk * BLOCK + row
    carry_ref[...] = jnp.zeros_like(carry_ref)
    acc_ref[...] = jnp.zeros_like(acc_ref)
    for n in range(2 * SB_PAIRS):
        q_pair = q_ref[0, q_rows, (n // 2) * LANES:(n // 2 + 1) * LANES]
        qh_ref[n] = jnp.where(low_half, q_pair, zero_bf) if n % 2 == 0 else jnp.where(low_half, zero_bf, q_pair)

    def visit(k_blk):
        off = pl.multiple_of(jnp.where(k_blk == 0, meta_off, (k_blk - 1) * BLOCK), BLOCK)
        kpos = k_blk * BLOCK + lane

        def scores(p):
            k_pair = k_ref[0, pl.ds(off, BLOCK), p * LANES:(p + 1) * LANES]
            q_both = qh_ref[2 * p:2 * p + 2].reshape(2 * BLOCK, LANES)
            z = lax.dot_general(q_both, k_pair, (((1,), (1,)), ((), ())), preferred_element_type=F32)
            z_ref[2 * p:2 * p + 2] = z.reshape(2, BLOCK, LANES)

        bias_ref[...] = jnp.where((kpos >= META_PAD) & (kpos < qpos), 0.0, NEG_INF)

        def softplus_split(n):
            z = z_ref[n] + bias_ref[...]
            z_ref[n] = z - carry_ref[n]
            sp = jnp.maximum(z, 0.0) + jnp.log(1.0 + jnp.exp2(jnp.abs(z) * -LOG2_E))
            hi = sp.astype(BF16)
            split_ref[n, :, 0:LANES] = hi
            split_ref[n, :, LANES:] = (sp - hi.astype(F32)).astype(BF16)

        def suffix_sums(g):
            heads_g = slice(g * SB_SUM_GROUP, (g + 1) * SB_SUM_GROUP)
            sums = jnp.dot(split_ref[heads_g].reshape(SB_SUM_GROUP * BLOCK, 2 * LANES), tri_ref[...],
                           preferred_element_type=F32)
            sums_ref[heads_g] = sums.reshape(SB_SUM_GROUP, BLOCK, 2 * LANES)

        def carry_update(n, least):
            new_carry = carry_ref[n] + sums_ref[n, :, LANES:]
            carry_ref[n] = new_carry
            return new_carry if least is None else jnp.minimum(least, new_carry)

        def weights(n):
            w = jnp.exp(z_ref[n] - sums_ref[n, :, 0:LANES])
            w_ref[n] = w.astype(BF16)

        def accumulate(p):
            cols = slice(p * LANES, (p + 1) * LANES)
            o = jnp.dot(w_ref[2 * p:2 * p + 2].reshape(2 * BLOCK, LANES), v_ref[0, pl.ds(off, BLOCK), cols],
                        preferred_element_type=F32)
            acc_ref[p] += jnp.where(low_half, o[:BLOCK], o[BLOCK:])

        pairs, heads = range(SB_PAIRS), range(2 * SB_PAIRS)
        for p in pairs:
            scores(p)
        for n in heads:
            softplus_split(n)
        for g in range(2 * SB_PAIRS // SB_SUM_GROUP):
            suffix_sums(g)
        least = None
        for n in heads:
            least = carry_update(n, least)
        least_mass = jnp.min(least)
        for n in heads:
            weights(n)
        for p in pairs:
            accumulate(p)
        return least_mass

    def keep_going(state):
        k_blk, least_mass = state
        return (k_blk >= 0) & (least_mass < SB_MASS_CUTOFF)

    def body(state):
        k_blk, _ = state
        return k_blk - 1, visit(k_blk)

    lax.while_loop(keep_going, body, (q_blk, jnp.float32(0.0)))

    for p in range(SB_PAIRS):
        cols = slice(p * LANES, (p + 1) * LANES)
        out_ref[0, q_rows, cols] = (acc_ref[p] * gate_ref[0, q_rows, cols]).astype(BF16)


def _sb_attn(qkv, gate, tri):
    batch, lp, _ = qkv.shape
    seq = lp - BLOCK
    q_tile = SB_Q_PER_STEP * BLOCK
    seq_cols = lambda col: pl.BlockSpec((1, lp, SB_WIDTH), lambda b, i: (b, 0, col))
    return pl.pallas_call(
        _sb_attn_kernel,
        grid=(batch, seq // q_tile),
        in_specs=[
            pl.BlockSpec((1, q_tile, SB_WIDTH), lambda b, i: (b, i, 0)),
            seq_cols(1), seq_cols(2),
            pl.BlockSpec((1, q_tile, SB_WIDTH), lambda b, i: (b, i, 0)),
            pl.BlockSpec((2 * BLOCK, 2 * LANES), lambda b, i: (0, 0)),
        ],
        out_specs=pl.BlockSpec((1, q_tile, SB_WIDTH), lambda b, i: (b, i, 0)),
        out_shape=jax.ShapeDtypeStruct((batch, seq, SB_WIDTH), BF16),
        scratch_shapes=[
            pltpu.VMEM((2 * SB_PAIRS, BLOCK, LANES), F32),
            pltpu.VMEM((SB_PAIRS, BLOCK, LANES), F32),
            pltpu.VMEM((2 * SB_PAIRS, BLOCK, LANES), BF16),
            pltpu.VMEM((2 * SB_PAIRS, BLOCK, LANES), F32),
            pltpu.VMEM((2 * SB_PAIRS, BLOCK, 2 * LANES), BF16),
            pltpu.VMEM((2 * SB_PAIRS, BLOCK, 2 * LANES), F32),
            pltpu.VMEM((2 * SB_PAIRS, BLOCK, LANES), BF16),
            pltpu.VMEM((BLOCK, LANES), F32),
        ],
        compiler_params=_params("parallel", "arbitrary"),
        name="sb_attn",
    )(qkv, qkv, qkv, gate, tri)


def _sb_out_kernel(o_ref, h_ref, w_ref, post_ref, out_ref):
    y = jnp.dot(o_ref[0], w_ref[...], preferred_element_type=F32)
    out_ref[0] = h_ref[0] + _rms_normed(y, post_ref[...])


def _sb_out(o_gated, h, w_bf, post):
    batch, rows, _ = o_gated.shape
    return pl.pallas_call(
        _sb_out_kernel,
        grid=(batch, rows // ROW_TILE),
        in_specs=[
            pl.BlockSpec((1, ROW_TILE, SB_WIDTH), lambda b, i: (b, i, 0)),
            pl.BlockSpec((1, ROW_TILE, D_MODEL), lambda b, i: (b, i, 0)),
            pl.BlockSpec((SB_WIDTH, D_MODEL), lambda b, i: (0, 0)),
            pl.BlockSpec((1, D_MODEL), lambda b, i: (0, 0)),
        ],
        out_specs=pl.BlockSpec((1, ROW_TILE, D_MODEL), lambda b, i: (b, i, 0)),
        out_shape=jax.ShapeDtypeStruct((batch, rows, D_MODEL), F32),
        compiler_params=_params("parallel", "parallel"),
        name="sb_out_proj",
    )(o_gated, h, w_bf, post)


def _rope_tables(seq):
    half = HEAD_DIM // 2
    pos = jnp.concatenate([N_META + jnp.arange(seq), jnp.zeros((META_PAD,), jnp.int32), jnp.arange(N_META)])
    inv = ROPE_THETA ** (-jnp.arange(half, dtype=jnp.float32) / half)
    ang = pos.astype(jnp.float32)[:, None] * inv[None, :]
    cos = jnp.cos(ang)
    sin = jnp.sin(ang)
    cos_t = jnp.tile(jnp.concatenate([cos, cos], axis=-1), (1, LANES // HEAD_DIM))
    sin_t = jnp.tile(jnp.concatenate([-sin, sin], axis=-1), (1, LANES // HEAD_DIM))
    return cos_t, sin_t


def kernel(x, meta_tokens, ab_pre_norm, ab_w_in, ab_sinks, ab_conv_w, ab_conv_b, ab_conv_ln_g, ab_conv_ln_b, ab_w_pw2, ab_w_out, ab_post_norm, sb_pre_norm, sb_w_in, sb_w_out, sb_post_norm):
    batch, seq, _ = x.shape
    assert seq % ROW_TILE == 0
    tiles_per_seq = seq // ROW_TILE
    n_blocks = seq // BLOCK

    x_flat = x.reshape(batch * seq, D_MODEL)
    meta_blk = jnp.concatenate([jnp.zeros((META_PAD, D_MODEL), x.dtype), meta_tokens.astype(x.dtype)], axis=0)
    cos_t, sin_t = _rope_tables(seq)
    pre0, post0, pre1 = ab_pre_norm[0][None], ab_post_norm[0][None], sb_pre_norm[0][None]
    w_in0, w_pw2, w_out0 = ab_w_in[0].astype(BF16), ab_w_pw2[0].astype(BF16), ab_w_out[0].astype(BF16)
    w_in1, w_out1 = sb_w_in[0].astype(BF16), sb_w_out[0].astype(BF16)

    qkv, rest = _ab_in(x_flat, pre0, w_in0, cos_t, sin_t, ROW_TILE, lambda r: r % tiles_per_seq)
    qkv_m, rest_m = _ab_in(meta_blk, pre0, w_in0, cos_t, sin_t, BLOCK, lambda r: n_blocks)
    qkv_m, rest_m = qkv_m[None], rest_m[None]
    conv_w = jnp.concatenate([ab_conv_w[0], jnp.zeros((1, CONV_CHANNELS), F32)], axis=0)
    mix_args = (conv_w, ab_conv_b[0][None], ab_conv_ln_g[0][None], ab_conv_ln_b[0][None], w_pw2)
    mix = _ab_mix(ab_sinks[0], qkv.reshape(batch, seq, -1), rest.reshape(batch, seq, -1), qkv_m, rest_m, *mix_args, 1)
    mix_m = _ab_mix(ab_sinks[0], qkv_m, rest_m, qkv_m, rest_m, *mix_args, 0)

    mid_args = (w_out0, post0, pre1, w_in1)
    _, qkv1_m, _ = _mid(mix_m[0], meta_blk, *mid_args, BLOCK, 1)
    h1, qkv1, gate1 = _mid(mix.reshape(batch * seq, AB_MIX), x_flat, *mid_args, ROW_TILE, tiles_per_seq, qkv1_m)

    r = jnp.arange(BLOCK)
    tri = jnp.concatenate([(r[:, None] >= r[None, :]).astype(BF16), jnp.ones((BLOCK, LANES), BF16)], axis=1)
    tri = jnp.concatenate([tri, tri], axis=0)
    o_gated = _sb_attn(qkv1, gate1.reshape(batch, seq, -1), tri)
    return _sb_out(o_gated, h1.reshape(batch, seq, D_MODEL), w_out1, sb_post_norm[0][None])
```

```python
import functools

import jax
import jax.numpy as jnp
from jax import lax
from jax.experimental import pallas as pl
from jax.experimental.pallas import tpu as pltpu

D_MODEL = 1024
N_META = 16
BLOCK = 128
META_PAD = BLOCK - N_META
HEAD_DIM = 64
ROPE_THETA = 10000.0
NORM_EPS = 1e-6
NEG_INF = -1e30
SWA_HEADS = 8
SWA_KV_HEADS = 2
SWA_GROUP = SWA_HEADS // SWA_KV_HEADS
SWA_WINDOW = 128
SWA_WIDTH = SWA_HEADS * HEAD_DIM
SWA_KV_WIDTH = SWA_KV_HEADS * HEAD_DIM
CONV_CHANNELS = 512
CONV_WIDTH = 31
CONV_LN_EPS = 1e-5
SB_HEADS = 16
SB_WIDTH = SB_HEADS * HEAD_DIM
AB_IN = 2 * SWA_WIDTH + 2 * SWA_KV_WIDTH + 3 * CONV_CHANNELS
AB_MIX = SWA_WIDTH + CONV_CHANNELS
SB_IN = 4 * SB_WIDTH

LANES = 128
SUBLANES = 8
ROW_TILE = 512
AB_IN_ROW_TILE = 1024
SB_OUT_ROW_TILE = 1024
VMEM_LIMIT = 48 * 1024 * 1024
CONV_HALO = -(-(CONV_WIDTH - 1) // SUBLANES) * SUBLANES
CONV_WIN_START = BLOCK - CONV_HALO
CONV_WIN_ROWS = BLOCK + CONV_HALO
CONV_LEAD = CONV_HALO - (CONV_WIDTH - 1)

SB_PAIRS = SB_WIDTH // LANES
SB_Q_PER_STEP = 4
SB_SUM_GROUP = 16
SB_MASS_CUTOFF = 104.0
LOG2_E = 1.4426950408889634

F32 = jnp.float32
BF16 = jnp.bfloat16


def _rms_normed(x, gain):
    return x * lax.rsqrt(jnp.mean(x * x, axis=-1, keepdims=True) + NORM_EPS) * gain


def _silu(x):
    return x * jax.nn.sigmoid(x)


def _params(*semantics):
    return pltpu.CompilerParams(dimension_semantics=semantics, vmem_limit_bytes=VMEM_LIMIT)


def _ab_in_kernel(h_ref, gain_ref, w_ref, cos_ref, sin_ref, qkv_ref, rest_ref):
    hn = _rms_normed(h_ref[...], gain_ref[...]).astype(BF16)

    def proj(lo, hi):
        return jnp.dot(hn, w_ref[:, lo:hi], preferred_element_type=F32)

    cos = cos_ref[...]
    sin = sin_ref[...]
    lane = lax.broadcasted_iota(jnp.int32, cos.shape, 1)
    first_half = (lane % HEAD_DIM) < (HEAD_DIM // 2)

    def rope(t):
        rot = jnp.where(first_half, pltpu.roll(t, LANES - HEAD_DIM // 2, 1), pltpu.roll(t, HEAD_DIM // 2, 1))
        return t * cos + rot * sin

    q = proj(0, SWA_WIDTH)
    for c in range(SWA_WIDTH // LANES):
        cols = slice(c * LANES, (c + 1) * LANES)
        qkv_ref[:, cols] = (rope(q[:, cols]) * HEAD_DIM ** -0.5).astype(BF16)
    v_lo = SWA_WIDTH + SWA_KV_WIDTH
    kv = proj(SWA_WIDTH, v_lo + SWA_KV_WIDTH)
    qkv_ref[:, SWA_WIDTH:v_lo] = rope(kv[:, :SWA_KV_WIDTH]).astype(BF16)
    qkv_ref[:, v_lo:v_lo + SWA_KV_WIDTH] = kv[:, SWA_KV_WIDTH:].astype(BF16)

    ga_lo = v_lo + SWA_KV_WIDTH
    rest_ref[:, 0:SWA_WIDTH] = _silu(proj(ga_lo, ga_lo + SWA_WIDTH))
    glu_lo = ga_lo + SWA_WIDTH
    glu_a = proj(glu_lo, glu_lo + CONV_CHANNELS)
    glu_b = proj(glu_lo + CONV_CHANNELS, glu_lo + 2 * CONV_CHANNELS)
    rest_ref[:, SWA_WIDTH:SWA_WIDTH + CONV_CHANNELS] = glu_a * jax.nn.sigmoid(glu_b)
    gb_lo = glu_lo + 2 * CONV_CHANNELS
    rest_ref[:, SWA_WIDTH + CONV_CHANNELS:] = _silu(proj(gb_lo, gb_lo + CONV_CHANNELS))


def _ab_in(h_flat, gain, w_bf, cos_t, sin_t, row_tile, table_block):
    rows = h_flat.shape[0]
    qkv_w = SWA_WIDTH + 2 * SWA_KV_WIDTH
    rest_w = SWA_WIDTH + 2 * CONV_CHANNELS
    return pl.pallas_call(
        _ab_in_kernel,
        grid=(rows // row_tile,),
        in_specs=[
            pl.BlockSpec((row_tile, D_MODEL), lambda r: (r, 0)),
            pl.BlockSpec((1, D_MODEL), lambda r: (0, 0)),
            pl.BlockSpec((D_MODEL, AB_IN), lambda r: (0, 0)),
            pl.BlockSpec((row_tile, LANES), lambda r: (table_block(r), 0)),
            pl.BlockSpec((row_tile, LANES), lambda r: (table_block(r), 0)),
        ],
        out_specs=[
            pl.BlockSpec((row_tile, qkv_w), lambda r: (r, 0)),
            pl.BlockSpec((row_tile, rest_w), lambda r: (r, 0)),
        ],
        out_shape=[
            jax.ShapeDtypeStruct((rows, qkv_w), BF16),
            jax.ShapeDtypeStruct((rows, rest_w), F32),
        ],
        compiler_params=_params("parallel"),
        name="ab_in_proj",
    )(h_flat, gain, w_bf, cos_t, sin_t)


def _ab_mix_kernel(first_ref_blk, sinks_ref, qkv_ref, kv_prev_ref, kv_meta_ref, rest_ref, u_prev_ref, u_meta_ref,
                   conv_w_ref, conv_b_ref, ln_g_ref, ln_b_ref, w_pw2_ref,
                   out_ref, ext_ref, rot_ref, s_ref, p_ref):
    ref_blk = pl.program_id(1) + first_ref_blk
    prev_is_meta = ref_blk == 1
    qkv = qkv_ref[0]
    kv_meta = kv_meta_ref[0]
    kv_prev = jnp.where(prev_is_meta, kv_meta, kv_prev_ref[0])

    k_cat = jnp.concatenate([kv_meta[:, :LANES], kv_prev[:, :LANES], qkv[:, SWA_WIDTH:SWA_WIDTH + LANES]], axis=0)
    v_cat = jnp.concatenate([kv_meta[:, LANES:], kv_prev[:, LANES:], qkv[:, SWA_WIDTH + LANES:]], axis=0)
    k_swap = jnp.concatenate([k_cat[:, HEAD_DIM:], k_cat[:, :HEAD_DIM]], axis=1)
    v_swap = jnp.concatenate([v_cat[:, HEAD_DIM:], v_cat[:, :HEAD_DIM]], axis=1)

    row = lax.broadcasted_iota(jnp.int32, (BLOCK, BLOCK), 0)
    col = lax.broadcasted_iota(jnp.int32, (BLOCK, BLOCK), 1)
    qpos = ref_blk * BLOCK + row
    meta_ok = (col >= META_PAD) & (qpos - col >= SWA_WINDOW)
    ppos = (ref_blk - 1) * BLOCK + col
    prev_ok = (ppos >= META_PAD) & (qpos - ppos < SWA_WINDOW)
    cpos = ref_blk * BLOCK + col
    cur_ok = (cpos >= META_PAD) & (qpos >= cpos)
    mask = jnp.concatenate([meta_ok, prev_ok, cur_ok], axis=1)

    lane = lax.broadcasted_iota(jnp.int32, (BLOCK, LANES), 1)
    low_half = lane < HEAD_DIM
    zero_bf = jnp.zeros((), BF16)

    direct = [h for h in range(SWA_HEADS) if h // SWA_GROUP == h % 2]
    order = direct + [h for h in range(SWA_HEADS) if h not in direct]
    n_direct = len(direct)

    def head_q(head):
        q_pair = qkv[:, (head // 2) * LANES:(head // 2 + 1) * LANES]
        return jnp.where(low_half if head % 2 == 0 else ~low_half, q_pair, zero_bf)

    for slots, k_use in ((slice(0, n_direct), k_cat), (slice(n_direct, SWA_HEADS), k_swap)):
        q_stack = jnp.concatenate([head_q(h) for h in order[slots]], axis=0)
        s = lax.dot_general(q_stack, k_use, (((1,), (1,)), ((), ())), preferred_element_type=F32)
        s_ref[slots] = s.reshape(-1, BLOCK, 3 * BLOCK)
    inv_denoms = {}
    for slot, head in enumerate(order):
        s = jnp.where(mask, s_ref[slot], NEG_INF)
        sink = sinks_ref[head]
        m = jnp.maximum(jnp.max(s, axis=-1, keepdims=True), sink)
        p = jnp.exp(s - m)
        inv_denoms[head] = 1.0 / (jnp.sum(p, axis=-1, keepdims=True) + jnp.exp(sink - m))
        p_ref[slot] = p.astype(BF16)
    outs = {}
    for slots, v_use in ((slice(0, n_direct), v_cat), (slice(n_direct, SWA_HEADS), v_swap)):
        o = jnp.dot(p_ref[slots].reshape(-1, 3 * BLOCK), v_use, preferred_element_type=F32)
        for j, head in enumerate(order[slots]):
            outs[head] = o[j * BLOCK:(j + 1) * BLOCK] * inv_denoms[head]
    rest = rest_ref[0]
    for pair in range(SWA_HEADS // 2):
        cols = slice(pair * LANES, (pair + 1) * LANES)
        out_ref[0, :, cols] = (jnp.where(low_half, outs[2 * pair], outs[2 * pair + 1]) * rest[:, cols]).astype(BF16)

    u_prev = jnp.where(prev_is_meta, u_meta_ref[0], u_prev_ref[0])
    ext_ref[0:BLOCK, :] = jnp.where(ref_blk > 0, u_prev, 0.0)
    ext_ref[BLOCK:2 * BLOCK, :] = rest[:, SWA_WIDTH:SWA_WIDTH + CONV_CHANNELS]
    window = ext_ref[CONV_WIN_START:, :]
    for b in range(1, SUBLANES):
        rot_ref[b - 1] = pltpu.roll(window, CONV_WIN_ROWS - b, 0)
    conv_groups = []
    for cg in range(CONV_CHANNELS // LANES):
        cols = slice(cg * LANES, (cg + 1) * LANES)
        acc = jnp.zeros((BLOCK, LANES), F32) + conv_b_ref[:, cols]
        for w in range(CONV_WIDTH):
            a, b = divmod(CONV_LEAD + w, SUBLANES)
            if b == 0:
                taps = ext_ref[CONV_WIN_START + SUBLANES * a:CONV_WIN_START + SUBLANES * a + BLOCK, cols]
            else:
                taps = rot_ref[b - 1, SUBLANES * a:SUBLANES * a + BLOCK, cols]
            acc = acc + taps * conv_w_ref[w:w + 1, cols]
        conv_groups.append(acc)
    conv = jnp.concatenate(conv_groups, axis=1)

    mu = jnp.mean(conv, axis=-1, keepdims=True)
    xc = conv - mu
    ln = xc * lax.rsqrt(jnp.mean(xc * xc, axis=-1, keepdims=True) + CONV_LN_EPS) * ln_g_ref[...] + ln_b_ref[...]
    c_act = _silu(ln).astype(BF16)
    c_branch = jnp.dot(c_act, w_pw2_ref[...], preferred_element_type=F32) * rest[:, SWA_WIDTH + CONV_CHANNELS:]
    out_ref[0, :, SWA_WIDTH:] = c_branch.astype(BF16)


def _ab_mix(sinks, qkv, rest, qkv_meta, rest_meta, conv_w, conv_b, ln_g, ln_b, w_pw2, first_ref_blk):
    batch, rows, qkv_w = qkv.shape
    rest_w = rest.shape[-1]
    kv_blk = (SWA_WIDTH // (2 * SWA_KV_WIDTH))
    prev_blk = lambda i: jnp.maximum(i - 1, 0)
    const2 = lambda b, i: (0, 0)
    return pl.pallas_call(
        functools.partial(_ab_mix_kernel, first_ref_blk),
        grid=(batch, rows // BLOCK),
        in_specs=[
            pl.BlockSpec(memory_space=pltpu.SMEM),
            pl.BlockSpec((1, BLOCK, qkv_w), lambda b, i: (b, i, 0)),
            pl.BlockSpec((1, BLOCK, 2 * SWA_KV_WIDTH), lambda b, i: (b, prev_blk(i), kv_blk)),
            pl.BlockSpec((1, BLOCK, 2 * SWA_KV_WIDTH), lambda b, i: (0, 0, kv_blk)),
            pl.BlockSpec((1, BLOCK, rest_w), lambda b, i: (b, i, 0)),
            pl.BlockSpec((1, BLOCK, CONV_CHANNELS), lambda b, i: (b, prev_blk(i), 1)),
            pl.BlockSpec((1, BLOCK, CONV_CHANNELS), lambda b, i: (0, 0, 1)),
            pl.BlockSpec((CONV_WIDTH + 1, CONV_CHANNELS), const2),
            pl.BlockSpec((1, CONV_CHANNELS), const2),
            pl.BlockSpec((1, CONV_CHANNELS), const2),
            pl.BlockSpec((1, CONV_CHANNELS), const2),
            pl.BlockSpec((CONV_CHANNELS, CONV_CHANNELS), const2),
        ],
        out_specs=pl.BlockSpec((1, BLOCK, AB_MIX), lambda b, i: (b, i, 0)),
        out_shape=jax.ShapeDtypeStruct((batch, rows, AB_MIX), BF16),
        scratch_shapes=[
            pltpu.VMEM((2 * BLOCK, CONV_CHANNELS), F32),
            pltpu.VMEM((SUBLANES - 1, CONV_WIN_ROWS, CONV_CHANNELS), F32),
            pltpu.VMEM((SWA_HEADS, BLOCK, 3 * BLOCK), F32),
            pltpu.VMEM((SWA_HEADS, BLOCK, 3 * BLOCK), BF16),
        ],
        compiler_params=_params("parallel", "parallel"),
        name="ab_mix",
    )(sinks, qkv, qkv, qkv_meta, rest, rest, rest_meta, conv_w, conv_b, ln_g, ln_b, w_pw2)


def _mid_kernel(mix_ref, h_ref, w_out_ref, post_ref, gain_ref, w_in_ref, h1_ref, qkv_ref, gate_ref):
    y = jnp.dot(mix_ref[...], w_out_ref[...], preferred_element_type=F32)
    h1 = h_ref[...] + _rms_normed(y, post_ref[...])
    h1_ref[...] = h1
    hn = _rms_normed(h1, gain_ref[...]).astype(BF16)
    chunk = 4 * LANES

    def proj(lo):
        return jnp.dot(hn, w_in_ref[:, lo:lo + chunk], preferred_element_type=F32)

    for c in range(SB_WIDTH // chunk):
        lo = c * chunk
        qkv_ref[0, :, lo:lo + chunk] = (proj(lo) * HEAD_DIM ** -0.5).astype(BF16)
    for c in range(SB_WIDTH // chunk, 3 * SB_WIDTH // chunk):
        lo = c * chunk
        qkv_ref[0, :, lo:lo + chunk] = proj(lo).astype(BF16)
    for c in range(SB_WIDTH // chunk):
        lo = c * chunk
        gate_ref[:, lo:lo + chunk] = _silu(proj(3 * SB_WIDTH + lo))


def _mid_tail_kernel(n_tiles, mix_ref, h_ref, w_out_ref, post_ref, gain_ref, w_in_ref, tail_ref,
                     h1_ref, qkv_ref, gate_ref):
    @pl.when(pl.program_id(0) < n_tiles)
    def _():
        _mid_kernel(mix_ref, h_ref, w_out_ref, post_ref, gain_ref, w_in_ref, h1_ref, qkv_ref, gate_ref)

    @pl.when(pl.program_id(0) >= n_tiles)
    def _():
        qkv_ref[0, 0:tail_ref.shape[1], :] = tail_ref[0]


def _mid(mix_flat, h_flat, w_out_bf, post, gain, w_in_bf, row_tile, tiles_per_seq, qkv_tail=None):
    rows = h_flat.shape[0]
    n_tiles = rows // row_tile
    n_seq = n_tiles // tiles_per_seq
    seq_rows = tiles_per_seq * row_tile
    const = lambda shape: pl.BlockSpec(shape, lambda r: (0,) * len(shape))
    weights = [const((AB_MIX, D_MODEL)), const((1, D_MODEL)), const((1, D_MODEL)), const((D_MODEL, SB_IN))]
    if qkv_tail is None:
        tile = lambda r: r
        qkv_index = lambda r: (r // tiles_per_seq, r % tiles_per_seq, 0)
        body, steps, tail_specs, tail_args, tail_rows = _mid_kernel, n_tiles, [], (), 0
    else:
        tile = lambda r: jnp.minimum(r, n_tiles - 1)
        qkv_index = lambda r: (jnp.where(r < n_tiles, r // tiles_per_seq, r - n_tiles),
                               jnp.where(r < n_tiles, r % tiles_per_seq, tiles_per_seq), 0)
        body, steps = functools.partial(_mid_tail_kernel, n_tiles), n_tiles + n_seq
        tail_specs, tail_args, tail_rows = [const(qkv_tail.shape)], (qkv_tail,), qkv_tail.shape[1]
    row_spec = lambda width: pl.BlockSpec((row_tile, width), lambda r: (tile(r), 0))
    return pl.pallas_call(
        body,
        grid=(steps,),
        in_specs=[row_spec(AB_MIX), row_spec(D_MODEL)] + weights + tail_specs,
        out_specs=[row_spec(D_MODEL), pl.BlockSpec((1, row_tile, 3 * SB_WIDTH), qkv_index), row_spec(SB_WIDTH)],
        out_shape=[
            jax.ShapeDtypeStruct((rows, D_MODEL), F32),
            jax.ShapeDtypeStruct((n_seq, seq_rows + tail_rows, 3 * SB_WIDTH), BF16),
            jax.ShapeDtypeStruct((rows, SB_WIDTH), F32),
        ],
        compiler_params=_params("arbitrary"),
        name="ab_out_sb_in_proj",
    )(mix_flat, h_flat, w_out_bf, post, gain, w_in_bf, *tail_args)


def _sb_attn_kernel(q_ref, *refs):
    def one_query_block(sub, _):
        q_blk = pl.program_id(1) * SB_Q_PER_STEP + sub + 1
        _sb_query_block(q_blk, pl.ds(pl.multiple_of(sub * BLOCK, BLOCK), BLOCK), q_ref, *refs)
        return 0

    lax.fori_loop(0, SB_Q_PER_STEP, one_query_block, 0)


def _sb_query_block(q_blk, q_rows, q_ref, k_ref, v_ref, gate_ref, tri_ref, out_ref,
                    carry_ref, acc_ref, qh_ref, z_ref, split_ref, sums_ref, w_ref, bias_ref):
    meta_off = k_ref.shape[1] - BLOCK
    lane = lax.broadcasted_iota(jnp.int32, (BLOCK, LANES), 1)
    row = lax.broadcasted_iota(jnp.int32, (BLOCK, LANES), 0)
    low_half = lane < HEAD_DIM
    zero_bf = jnp.zeros((), BF16)

    qpos = q_blk * BLOCK + row
    carry_ref[...] = jnp.zeros_like(carry_ref)
    acc_ref[...] = jnp.zeros_like(acc_ref)
    for n in range(2 * SB_PAIRS):
        q_pair = q_ref[0, q_rows, (n // 2) * LANES:(n // 2 + 1) * LANES]
        qh_ref[n] = jnp.where(low_half, q_pair, zero_bf) if n % 2 == 0 else jnp.where(low_half, zero_bf, q_pair)

    def visit(k_blk):
        off = pl.multiple_of(jnp.where(k_blk == 0, meta_off, (k_blk - 1) * BLOCK), BLOCK)
        kpos = k_blk * BLOCK + lane

        def scores(p):
            k_pair = k_ref[0, pl.ds(off, BLOCK), p * LANES:(p + 1) * LANES]
            q_both = qh_ref[2 * p:2 * p + 2].reshape(2 * BLOCK, LANES)
            z = lax.dot_general(q_both, k_pair, (((1,), (1,)), ((), ())), preferred_element_type=F32)
            z_ref[2 * p:2 * p + 2] = z.reshape(2, BLOCK, LANES)

        bias_ref[...] = jnp.where((kpos >= META_PAD) & (kpos < qpos), 0.0, NEG_INF)

        def softplus_split(n):
            z = z_ref[n] + bias_ref[...]
            z_ref[n] = z - carry_ref[n]
            sp = jnp.maximum(z, 0.0) + jnp.log(1.0 + jnp.exp2(jnp.abs(z) * -LOG2_E))
            hi = sp.astype(BF16)
            split_ref[n, :, 0:LANES] = hi
            split_ref[n, :, LANES:] = (sp - hi.astype(F32)).astype(BF16)

        def suffix_sums(g):
            heads_g = slice(g * SB_SUM_GROUP, (g + 1) * SB_SUM_GROUP)
            sums = jnp.dot(split_ref[heads_g].reshape(SB_SUM_GROUP * BLOCK, 2 * LANES), tri_ref[...],
                           preferred_element_type=F32)
            sums_ref[heads_g] = sums.reshape(SB_SUM_GROUP, BLOCK, 2 * LANES)

        def carry_update(n, least):
            new_carry = carry_ref[n] + sums_ref[n, :, LANES:]
            carry_ref[n] = new_carry
            return new_carry if least is None else jnp.minimum(least, new_carry)

        def weights(n):
            w = jnp.exp(z_ref[n] - sums_ref[n, :, 0:LANES])
            w_ref[n] = w.astype(BF16)

        def accumulate(p):
            cols = slice(p * LANES, (p + 1) * LANES)
            o = jnp.dot(w_ref[2 * p:2 * p + 2].reshape(2 * BLOCK, LANES), v_ref[0, pl.ds(off, BLOCK), cols],
                        preferred_element_type=F32)
            acc_ref[p] += jnp.where(low_half, o[:BLOCK], o[BLOCK:])

        pairs, heads = range(SB_PAIRS), range(2 * SB_PAIRS)
        for p in pairs:
            scores(p)
        for n in heads:
            softplus_split(n)
        for g in range(2 * SB_PAIRS // SB_SUM_GROUP):
            suffix_sums(g)
        least = None
        for n in heads:
            least = carry_update(n, least)
        least_mass = jnp.min(least)
        for n in heads:
            weights(n)
        for p in pairs:
            accumulate(p)
        return least_mass

    def keep_going(state):
        k_blk, least_mass = state
        return (k_blk >= 0) & (least_mass < SB_MASS_CUTOFF)

    def body(state):
        k_blk, _ = state
        return k_blk - 1, visit(k_blk)

    lax.while_loop(keep_going, body, (q_blk, jnp.float32(0.0)))

    for p in range(SB_PAIRS):
        cols = slice(p * LANES, (p + 1) * LANES)
        out_ref[0, q_rows, cols] = (acc_ref[p] * gate_ref[0, q_rows, cols]).astype(BF16)


def _sb_attn(qkv, gate, tri):
    batch, lp, _ = qkv.shape
    seq = lp - BLOCK
    q_tile = SB_Q_PER_STEP * BLOCK
    seq_cols = lambda col: pl.BlockSpec((1, lp, SB_WIDTH), lambda b, i: (b, 0, col))
    return pl.pallas_call(
        _sb_attn_kernel,
        grid=(batch, seq // q_tile),
        in_specs=[
            pl.BlockSpec((1, q_tile, SB_WIDTH), lambda b, i: (b, i, 0)),
            seq_cols(1), seq_cols(2),
            pl.BlockSpec((1, q_tile, SB_WIDTH), lambda b, i: (b, i, 0)),
            pl.BlockSpec((2 * BLOCK, 2 * LANES), lambda b, i: (0, 0)),
        ],
        out_specs=pl.BlockSpec((1, q_tile, SB_WIDTH), lambda b, i: (b, i, 0)),
        out_shape=jax.ShapeDtypeStruct((batch, seq, SB_WIDTH), BF16),
        scratch_shapes=[
            pltpu.VMEM((2 * SB_PAIRS, BLOCK, LANES), F32),
            pltpu.VMEM((SB_PAIRS, BLOCK, LANES), F32),
            pltpu.VMEM((2 * SB_PAIRS, BLOCK, LANES), BF16),
            pltpu.VMEM((2 * SB_PAIRS, BLOCK, LANES), F32),
            pltpu.VMEM((2 * SB_PAIRS, BLOCK, 2 * LANES), BF16),
            pltpu.VMEM((2 * SB_PAIRS, BLOCK, 2 * LANES), F32),
            pltpu.VMEM((2 * SB_PAIRS, BLOCK, LANES), BF16),
            pltpu.VMEM((BLOCK, LANES), F32),
        ],
        compiler_params=_params("parallel", "arbitrary"),
        name="sb_attn",
    )(qkv, qkv, qkv, gate, tri)


def _sb_out_kernel(o_ref, h_ref, w_ref, post_ref, out_ref):
    y = jnp.dot(o_ref[0], w_ref[...], preferred_element_type=F32)
    out_ref[0] = h_ref[0] + _rms_normed(y, post_ref[...])


def _sb_out(o_gated, h, w_bf, post):
    batch, rows, _ = o_gated.shape
    return pl.pallas_call(
        _sb_out_kernel,
        grid=(batch, rows // SB_OUT_ROW_TILE),
        in_specs=[
            pl.BlockSpec((1, SB_OUT_ROW_TILE, SB_WIDTH), lambda b, i: (b, i, 0)),
            pl.BlockSpec((1, SB_OUT_ROW_TILE, D_MODEL), lambda b, i: (b, i, 0)),
            pl.BlockSpec((SB_WIDTH, D_MODEL), lambda b, i: (0, 0)),
            pl.BlockSpec((1, D_MODEL), lambda b, i: (0, 0)),
        ],
        out_specs=pl.BlockSpec((1, SB_OUT_ROW_TILE, D_MODEL), lambda b, i: (b, i, 0)),
        out_shape=jax.ShapeDtypeStruct((batch, rows, D_MODEL), F32),
        compiler_params=_params("parallel", "parallel"),
        name="sb_out_proj",
    )(o_gated, h, w_bf, post)


def _rope_tables(seq):
    half = HEAD_DIM // 2
    pos = jnp.concatenate([N_META + jnp.arange(seq), jnp.zeros((META_PAD,), jnp.int32), jnp.arange(N_META)])
    inv = ROPE_THETA ** (-jnp.arange(half, dtype=jnp.float32) / half)
    ang = pos.astype(jnp.float32)[:, None] * inv[None, :]
    cos = jnp.cos(ang)
    sin = jnp.sin(ang)
    cos_t = jnp.tile(jnp.concatenate([cos, cos], axis=-1), (1, LANES // HEAD_DIM))
    sin_t = jnp.tile(jnp.concatenate([-sin, sin], axis=-1), (1, LANES // HEAD_DIM))
    return cos_t, sin_t


def kernel(x, meta_tokens, ab_pre_norm, ab_w_in, ab_sinks, ab_conv_w, ab_conv_b, ab_conv_ln_g, ab_conv_ln_b, ab_w_pw2, ab_w_out, ab_post_norm, sb_pre_norm, sb_w_in, sb_w_out, sb_post_norm):
    batch, seq, _ = x.shape
    assert seq % ROW_TILE == 0
    tiles_per_seq = seq // ROW_TILE
    n_blocks = seq // BLOCK

    x_flat = x.reshape(batch * seq, D_MODEL)
    meta_blk = jnp.concatenate([jnp.zeros((META_PAD, D_MODEL), x.dtype), meta_tokens.astype(x.dtype)], axis=0)
    cos_t, sin_t = _rope_tables(seq)
    pre0, post0, pre1 = ab_pre_norm[0][None], ab_post_norm[0][None], sb_pre_norm[0][None]
    w_in0, w_pw2, w_out0 = ab_w_in[0].astype(BF16), ab_w_pw2[0].astype(BF16), ab_w_out[0].astype(BF16)
    w_in1, w_out1 = sb_w_in[0].astype(BF16), sb_w_out[0].astype(BF16)

    qkv, rest = _ab_in(x_flat, pre0, w_in0, cos_t, sin_t, AB_IN_ROW_TILE, lambda r: r % (seq // AB_IN_ROW_TILE))
    qkv_m, rest_m = _ab_in(meta_blk, pre0, w_in0, cos_t, sin_t, BLOCK, lambda r: n_blocks)
    qkv_m, rest_m = qkv_m[None], rest_m[None]
    conv_w = jnp.concatenate([ab_conv_w[0], jnp.zeros((1, CONV_CHANNELS), F32)], axis=0)
    mix_args = (conv_w, ab_conv_b[0][None], ab_conv_ln_g[0][None], ab_conv_ln_b[0][None], w_pw2)
    mix = _ab_mix(ab_sinks[0], qkv.reshape(batch, seq, -1), rest.reshape(batch, seq, -1), qkv_m, rest_m, *mix_args, 1)
    mix_m = _ab_mix(ab_sinks[0], qkv_m, rest_m, qkv_m, rest_m, *mix_args, 0)

    mid_args = (w_out0, post0, pre1, w_in1)
    _, qkv1_m, _ = _mid(mix_m[0], meta_blk, *mid_args, BLOCK, 1)
    h1, qkv1, gate1 = _mid(mix.reshape(batch * seq, AB_MIX), x_flat, *mid_args, ROW_TILE, tiles_per_seq, qkv1_m)

    r = jnp.arange(BLOCK)
    tri = jnp.concatenate([(r[:, None] >= r[None, :]).astype(BF16), jnp.ones((BLOCK, LANES), BF16)], axis=1)
    tri = jnp.concatenate([tri, tri], axis=0)
    o_gated = _sb_attn(qkv1, gate1.reshape(batch, seq, -1), tri)
    return _sb_out(o_gated, h1.reshape(batch, seq, D_MODEL), w_out1, sb_post_norm[0][None])
```

```python
import functools

import jax
import jax.numpy as jnp
from jax import lax
from jax.experimental import pallas as pl
from jax.experimental.pallas import tpu as pltpu

D_MODEL = 1024
N_META = 16
BLOCK = 128
META_PAD = BLOCK - N_META
HEAD_DIM = 64
ROPE_THETA = 10000.0
NORM_EPS = 1e-6
NEG_INF = -1e30
SWA_HEADS = 8
SWA_KV_HEADS = 2
SWA_GROUP = SWA_HEADS // SWA_KV_HEADS
SWA_WINDOW = 128
SWA_WIDTH = SWA_HEADS * HEAD_DIM
SWA_KV_WIDTH = SWA_KV_HEADS * HEAD_DIM
CONV_CHANNELS = 512
CONV_WIDTH = 31
CONV_LN_EPS = 1e-5
SB_HEADS = 16
SB_WIDTH = SB_HEADS * HEAD_DIM
AB_IN = 2 * SWA_WIDTH + 2 * SWA_KV_WIDTH + 3 * CONV_CHANNELS
AB_MIX = SWA_WIDTH + CONV_CHANNELS
SB_IN = 4 * SB_WIDTH

LANES = 128
SUBLANES = 8
ROW_TILE = 512
AB_IN_ROW_TILE = 1024
SB_OUT_ROW_TILE = 1024
VMEM_LIMIT = 48 * 1024 * 1024
CONV_HALO = -(-(CONV_WIDTH - 1) // SUBLANES) * SUBLANES
CONV_WIN_START = BLOCK - CONV_HALO
CONV_WIN_ROWS = BLOCK + CONV_HALO
CONV_LEAD = CONV_HALO - (CONV_WIDTH - 1)

SB_PAIRS = SB_WIDTH // LANES
SB_Q_PER_STEP = 4
SB_SUM_GROUP = 8
SB_MASS_CUTOFF = 104.0
LOG2_E = 1.4426950408889634

F32 = jnp.float32
BF16 = jnp.bfloat16


def _rms_normed(x, gain):
    return x * lax.rsqrt(jnp.mean(x * x, axis=-1, keepdims=True) + NORM_EPS) * gain


def _silu(x):
    return x * jax.nn.sigmoid(x)


def _params(*semantics):
    return pltpu.CompilerParams(dimension_semantics=semantics, vmem_limit_bytes=VMEM_LIMIT)


def _ab_in_kernel(h_ref, gain_ref, w_ref, cos_ref, sin_ref, qkv_ref, rest_ref):
    hn = _rms_normed(h_ref[...], gain_ref[...]).astype(BF16)

    def proj(lo, hi):
        return jnp.dot(hn, w_ref[:, lo:hi], preferred_element_type=F32)

    cos = cos_ref[...]
    sin = sin_ref[...]
    lane = lax.broadcasted_iota(jnp.int32, cos.shape, 1)
    first_half = (lane % HEAD_DIM) < (HEAD_DIM // 2)

    def rope(t):
        rot = jnp.where(first_half, pltpu.roll(t, LANES - HEAD_DIM // 2, 1), pltpu.roll(t, HEAD_DIM // 2, 1))
        return t * cos + rot * sin

    q = proj(0, SWA_WIDTH)
    for c in range(SWA_WIDTH // LANES):
        cols = slice(c * LANES, (c + 1) * LANES)
        qkv_ref[:, cols] = (rope(q[:, cols]) * HEAD_DIM ** -0.5).astype(BF16)
    v_lo = SWA_WIDTH + SWA_KV_WIDTH
    kv = proj(SWA_WIDTH, v_lo + SWA_KV_WIDTH)
    qkv_ref[:, SWA_WIDTH:v_lo] = rope(kv[:, :SWA_KV_WIDTH]).astype(BF16)
    qkv_ref[:, v_lo:v_lo + SWA_KV_WIDTH] = kv[:, SWA_KV_WIDTH:].astype(BF16)

    ga_lo = v_lo + SWA_KV_WIDTH
    rest_ref[:, 0:SWA_WIDTH] = _silu(proj(ga_lo, ga_lo + SWA_WIDTH))
    glu_lo = ga_lo + SWA_WIDTH
    glu_a = proj(glu_lo, glu_lo + CONV_CHANNELS)
    glu_b = proj(glu_lo + CONV_CHANNELS, glu_lo + 2 * CONV_CHANNELS)
    rest_ref[:, SWA_WIDTH:SWA_WIDTH + CONV_CHANNELS] = glu_a * jax.nn.sigmoid(glu_b)
    gb_lo = glu_lo + 2 * CONV_CHANNELS
    rest_ref[:, SWA_WIDTH + CONV_CHANNELS:] = _silu(proj(gb_lo, gb_lo + CONV_CHANNELS))


def _ab_in(h_flat, gain, w_bf, cos_t, sin_t, row_tile, table_block):
    rows = h_flat.shape[0]
    qkv_w = SWA_WIDTH + 2 * SWA_KV_WIDTH
    rest_w = SWA_WIDTH + 2 * CONV_CHANNELS
    return pl.pallas_call(
        _ab_in_kernel,
        grid=(rows // row_tile,),
        in_specs=[
            pl.BlockSpec((row_tile, D_MODEL), lambda r: (r, 0)),
            pl.BlockSpec((1, D_MODEL), lambda r: (0, 0)),
            pl.BlockSpec((D_MODEL, AB_IN), lambda r: (0, 0)),
            pl.BlockSpec((row_tile, LANES), lambda r: (table_block(r), 0)),
            pl.BlockSpec((row_tile, LANES), lambda r: (table_block(r), 0)),
        ],
        out_specs=[
            pl.BlockSpec((row_tile, qkv_w), lambda r: (r, 0)),
            pl.BlockSpec((row_tile, rest_w), lambda r: (r, 0)),
        ],
        out_shape=[
            jax.ShapeDtypeStruct((rows, qkv_w), BF16),
            jax.ShapeDtypeStruct((rows, rest_w), F32),
        ],
        compiler_params=_params("parallel"),
        name="ab_in_proj",
    )(h_flat, gain, w_bf, cos_t, sin_t)


def _ab_mix_kernel(first_ref_blk, sinks_ref, qkv_ref, kv_prev_ref, kv_meta_ref, rest_ref, u_prev_ref, u_meta_ref,
                   conv_w_ref, conv_b_ref, ln_g_ref, ln_b_ref, w_pw2_ref,
                   out_ref, ext_ref, rot_ref, s_ref, p_ref):
    ref_blk = pl.program_id(1) + first_ref_blk
    prev_is_meta = ref_blk == 1
    qkv = qkv_ref[0]
    kv_meta = kv_meta_ref[0]
    kv_prev = jnp.where(prev_is_meta, kv_meta, kv_prev_ref[0])

    k_cat = jnp.concatenate([kv_meta[:, :LANES], kv_prev[:, :LANES], qkv[:, SWA_WIDTH:SWA_WIDTH + LANES]], axis=0)
    v_cat = jnp.concatenate([kv_meta[:, LANES:], kv_prev[:, LANES:], qkv[:, SWA_WIDTH + LANES:]], axis=0)
    k_swap = jnp.concatenate([k_cat[:, HEAD_DIM:], k_cat[:, :HEAD_DIM]], axis=1)
    v_swap = jnp.concatenate([v_cat[:, HEAD_DIM:], v_cat[:, :HEAD_DIM]], axis=1)

    row = lax.broadcasted_iota(jnp.int32, (BLOCK, BLOCK), 0)
    col = lax.broadcasted_iota(jnp.int32, (BLOCK, BLOCK), 1)
    qpos = ref_blk * BLOCK + row
    meta_ok = (col >= META_PAD) & (qpos - col >= SWA_WINDOW)
    ppos = (ref_blk - 1) * BLOCK + col
    prev_ok = (ppos >= META_PAD) & (qpos - ppos < SWA_WINDOW)
    cpos = ref_blk * BLOCK + col
    cur_ok = (cpos >= META_PAD) & (qpos >= cpos)
    mask = jnp.concatenate([meta_ok, prev_ok, cur_ok], axis=1)

    lane = lax.broadcasted_iota(jnp.int32, (BLOCK, LANES), 1)
    low_half = lane < HEAD_DIM
    zero_bf = jnp.zeros((), BF16)

    direct = [h for h in range(SWA_HEADS) if h // SWA_GROUP == h % 2]
    order = direct + [h for h in range(SWA_HEADS) if h not in direct]
    n_direct = len(direct)

    def head_q(head):
        q_pair = qkv[:, (head // 2) * LANES:(head // 2 + 1) * LANES]
        return jnp.where(low_half if head % 2 == 0 else ~low_half, q_pair, zero_bf)

    for slots, k_use in ((slice(0, n_direct), k_cat), (slice(n_direct, SWA_HEADS), k_swap)):
        q_stack = jnp.concatenate([head_q(h) for h in order[slots]], axis=0)
        s = lax.dot_general(q_stack, k_use, (((1,), (1,)), ((), ())), preferred_element_type=F32)
        s_ref[slots] = s.reshape(-1, BLOCK, 3 * BLOCK)
    inv_denoms = {}
    for slot, head in enumerate(order):
        s = jnp.where(mask, s_ref[slot], NEG_INF)
        sink = sinks_ref[head]
        m = jnp.maximum(jnp.max(s, axis=-1, keepdims=True), sink)
        p = jnp.exp(s - m)
        inv_denoms[head] = 1.0 / (jnp.sum(p, axis=-1, keepdims=True) + jnp.exp(sink - m))
        p_ref[slot] = p.astype(BF16)
    outs = {}
    for slots, v_use in ((slice(0, n_direct), v_cat), (slice(n_direct, SWA_HEADS), v_swap)):
        o = jnp.dot(p_ref[slots].reshape(-1, 3 * BLOCK), v_use, preferred_element_type=F32)
        for j, head in enumerate(order[slots]):
            outs[head] = o[j * BLOCK:(j + 1) * BLOCK] * inv_denoms[head]
    rest = rest_ref[0]
    for pair in range(SWA_HEADS // 2):
        cols = slice(pair * LANES, (pair + 1) * LANES)
        out_ref[0, :, cols] = (jnp.where(low_half, outs[2 * pair], outs[2 * pair + 1]) * rest[:, cols]).astype(BF16)

    u_prev = jnp.where(prev_is_meta, u_meta_ref[0], u_prev_ref[0])
    ext_ref[0:BLOCK, :] = jnp.where(ref_blk > 0, u_prev, 0.0)
    ext_ref[BLOCK:2 * BLOCK, :] = rest[:, SWA_WIDTH:SWA_WIDTH + CONV_CHANNELS]
    window = ext_ref[CONV_WIN_START:, :]
    for b in range(1, SUBLANES):
        rot_ref[b - 1] = pltpu.roll(window, CONV_WIN_ROWS - b, 0)
    conv_groups = []
    for cg in range(CONV_CHANNELS // LANES):
        cols = slice(cg * LANES, (cg + 1) * LANES)
        acc = jnp.zeros((BLOCK, LANES), F32) + conv_b_ref[:, cols]
        for w in range(CONV_WIDTH):
            a, b = divmod(CONV_LEAD + w, SUBLANES)
            if b == 0:
                taps = ext_ref[CONV_WIN_START + SUBLANES * a:CONV_WIN_START + SUBLANES * a + BLOCK, cols]
            else:
                taps = rot_ref[b - 1, SUBLANES * a:SUBLANES * a + BLOCK, cols]
            acc = acc + taps * conv_w_ref[w:w + 1, cols]
        conv_groups.append(acc)
    conv = jnp.concatenate(conv_groups, axis=1)

    mu = jnp.mean(conv, axis=-1, keepdims=True)
    xc = conv - mu
    ln = xc * lax.rsqrt(jnp.mean(xc * xc, axis=-1, keepdims=True) + CONV_LN_EPS) * ln_g_ref[...] + ln_b_ref[...]
    c_act = _silu(ln).astype(BF16)
    c_branch = jnp.dot(c_act, w_pw2_ref[...], preferred_element_type=F32) * rest[:, SWA_WIDTH + CONV_CHANNELS:]
    out_ref[0, :, SWA_WIDTH:] = c_branch.astype(BF16)


def _ab_mix(sinks, qkv, rest, qkv_meta, rest_meta, conv_w, conv_b, ln_g, ln_b, w_pw2, first_ref_blk):
    batch, rows, qkv_w = qkv.shape
    rest_w = rest.shape[-1]
    kv_blk = (SWA_WIDTH // (2 * SWA_KV_WIDTH))
    prev_blk = lambda i: jnp.maximum(i - 1, 0)
    const2 = lambda b, i: (0, 0)
    return pl.pallas_call(
        functools.partial(_ab_mix_kernel, first_ref_blk),
        grid=(batch, rows // BLOCK),
        in_specs=[
            pl.BlockSpec(memory_space=pltpu.SMEM),
            pl.BlockSpec((1, BLOCK, qkv_w), lambda b, i: (b, i, 0)),
            pl.BlockSpec((1, BLOCK, 2 * SWA_KV_WIDTH), lambda b, i: (b, prev_blk(i), kv_blk)),
            pl.BlockSpec((1, BLOCK, 2 * SWA_KV_WIDTH), lambda b, i: (0, 0, kv_blk)),
            pl.BlockSpec((1, BLOCK, rest_w), lambda b, i: (b, i, 0)),
            pl.BlockSpec((1, BLOCK, CONV_CHANNELS), lambda b, i: (b, prev_blk(i), 1)),
            pl.BlockSpec((1, BLOCK, CONV_CHANNELS), lambda b, i: (0, 0, 1)),
            pl.BlockSpec((CONV_WIDTH + 1, CONV_CHANNELS), const2),
            pl.BlockSpec((1, CONV_CHANNELS), const2),
            pl.BlockSpec((1, CONV_CHANNELS), const2),
            pl.BlockSpec((1, CONV_CHANNELS), const2),
            pl.BlockSpec((CONV_CHANNELS, CONV_CHANNELS), const2),
        ],
        out_specs=pl.BlockSpec((1, BLOCK, AB_MIX), lambda b, i: (b, i, 0)),
        out_shape=jax.ShapeDtypeStruct((batch, rows, AB_MIX), BF16),
        scratch_shapes=[
            pltpu.VMEM((2 * BLOCK, CONV_CHANNELS), F32),
            pltpu.VMEM((SUBLANES - 1, CONV_WIN_ROWS, CONV_CHANNELS), F32),
            pltpu.VMEM((SWA_HEADS, BLOCK, 3 * BLOCK), F32),
            pltpu.VMEM((SWA_HEADS, BLOCK, 3 * BLOCK), BF16),
        ],
        compiler_params=_params("parallel", "parallel"),
        name="ab_mix",
    )(sinks, qkv, qkv, qkv_meta, rest, rest, rest_meta, conv_w, conv_b, ln_g, ln_b, w_pw2)


def _mid_kernel(mix_ref, h_ref, w_out_ref, post_ref, gain_ref, w_in_ref, h1_ref, qkv_ref, gate_ref):
    y = jnp.dot(mix_ref[...], w_out_ref[...], preferred_element_type=F32)
    h1 = h_ref[...] + _rms_normed(y, post_ref[...])
    h1_ref[...] = h1
    hn = _rms_normed(h1, gain_ref[...]).astype(BF16)
    chunk = 4 * LANES

    def proj(lo):
        return jnp.dot(hn, w_in_ref[:, lo:lo + chunk], preferred_element_type=F32)

    for c in range(SB_WIDTH // chunk):
        lo = c * chunk
        qkv_ref[0, :, lo:lo + chunk] = (proj(lo) * HEAD_DIM ** -0.5).astype(BF16)
    for c in range(SB_WIDTH // chunk, 3 * SB_WIDTH // chunk):
        lo = c * chunk
        qkv_ref[0, :, lo:lo + chunk] = proj(lo).astype(BF16)
    for c in range(SB_WIDTH // chunk):
        lo = c * chunk
        gate_ref[:, lo:lo + chunk] = _silu(proj(3 * SB_WIDTH + lo))


def _mid_tail_kernel(n_tiles, mix_ref, h_ref, w_out_ref, post_ref, gain_ref, w_in_ref, tail_ref,
                     h1_ref, qkv_ref, gate_ref):
    @pl.when(pl.program_id(0) < n_tiles)
    def _():
        _mid_kernel(mix_ref, h_ref, w_out_ref, post_ref, gain_ref, w_in_ref, h1_ref, qkv_ref, gate_ref)

    @pl.when(pl.program_id(0) >= n_tiles)
    def _():
        qkv_ref[0, 0:tail_ref.shape[1], :] = tail_ref[0]


def _mid(mix_flat, h_flat, w_out_bf, post, gain, w_in_bf, row_tile, tiles_per_seq, qkv_tail=None):
    rows = h_flat.shape[0]
    n_tiles = rows // row_tile
    n_seq = n_tiles // tiles_per_seq
    seq_rows = tiles_per_seq * row_tile
    const = lambda shape: pl.BlockSpec(shape, lambda r: (0,) * len(shape))
    weights = [const((AB_MIX, D_MODEL)), const((1, D_MODEL)), const((1, D_MODEL)), const((D_MODEL, SB_IN))]
    if qkv_tail is None:
        tile = lambda r: r
        qkv_index = lambda r: (r // tiles_per_seq, r % tiles_per_seq, 0)
        body, steps, tail_specs, tail_args, tail_rows = _mid_kernel, n_tiles, [], (), 0
    else:
        tile = lambda r: jnp.minimum(r, n_tiles - 1)
        qkv_index = lambda r: (jnp.where(r < n_tiles, r // tiles_per_seq, r - n_tiles),
                               jnp.where(r < n_tiles, r % tiles_per_seq, tiles_per_seq), 0)
        body, steps = functools.partial(_mid_tail_kernel, n_tiles), n_tiles + n_seq
        tail_specs, tail_args, tail_rows = [const(qkv_tail.shape)], (qkv_tail,), qkv_tail.shape[1]
    row_spec = lambda width: pl.BlockSpec((row_tile, width), lambda r: (tile(r), 0))
    return pl.pallas_call(
        body,
        grid=(steps,),
        in_specs=[row_spec(AB_MIX), row_spec(D_MODEL)] + weights + tail_specs,
        out_specs=[row_spec(D_MODEL), pl.BlockSpec((1, row_tile, 3 * SB_WIDTH), qkv_index), row_spec(SB_WIDTH)],
        out_shape=[
            jax.ShapeDtypeStruct((rows, D_MODEL), F32),
            jax.ShapeDtypeStruct((n_seq, seq_rows + tail_rows, 3 * SB_WIDTH), BF16),
            jax.ShapeDtypeStruct((rows, SB_WIDTH), F32),
        ],
        compiler_params=_params("arbitrary"),
        name="ab_out_sb_in_proj",
    )(mix_flat, h_flat, w_out_bf, post, gain, w_in_bf, *tail_args)


def _sb_attn_kernel(q_ref, *refs):
    def one_query_block(sub, _):
        q_blk = pl.program_id(1) * SB_Q_PER_STEP + sub + 1
        _sb_query_block(q_blk, pl.ds(pl.multiple_of(sub * BLOCK, BLOCK), BLOCK), q_ref, *refs)
        return 0

    lax.fori_loop(0, SB_Q_PER_STEP, one_query_block, 0)


def _sb_query_block(q_blk, q_rows, q_ref, k_ref, v_ref, gate_ref, tri_ref, out_ref,
                    carry_ref, acc_ref, qh_ref, z_ref, split_ref, sums_ref, w_ref, bias_ref):
    meta_off = k_ref.shape[1] - BLOCK
    lane = lax.broadcasted_iota(jnp.int32, (BLOCK, LANES), 1)
    row = lax.broadcasted_iota(jnp.int32, (BLOCK, LANES), 0)
    low_half = lane < HEAD_DIM
    zero_bf = jnp.zeros((), BF16)

    qpos = q_blk * BLOCK + row
    carry_ref[...] = jnp.zeros_like(carry_ref)
    acc_ref[...] = jnp.zeros_like(acc_ref)
    for n in range(2 * SB_PAIRS):
        q_pair = q_ref[0, q_rows, (n // 2) * LANES:(n // 2 + 1) * LANES]
        qh_ref[n] = jnp.where(low_half, q_pair, zero_bf) if n % 2 == 0 else jnp.where(low_half, zero_bf, q_pair)

    def visit(k_blk):
        off = pl.multiple_of(jnp.where(k_blk == 0, meta_off, (k_blk - 1) * BLOCK), BLOCK)
        kpos = k_blk * BLOCK + lane

        def scores(p):
            k_pair = k_ref[0, pl.ds(off, BLOCK), p * LANES:(p + 1) * LANES]
            q_both = qh_ref[2 * p:2 * p + 2].reshape(2 * BLOCK, LANES)
            z = lax.dot_general(q_both, k_pair, (((1,), (1,)), ((), ())), preferred_element_type=F32)
            z_ref[2 * p:2 * p + 2] = z.reshape(2, BLOCK, LANES)

        bias_ref[...] = jnp.where((kpos >= META_PAD) & (kpos < qpos), 0.0, NEG_INF)

        def softplus_split(n):
            z = z_ref[n] + bias_ref[...]
            z_ref[n] = z - carry_ref[n]
            sp = jnp.maximum(z, 0.0) + jnp.log(1.0 + jnp.exp2(jnp.abs(z) * -LOG2_E))
            hi = sp.astype(BF16)
            split_ref[n, :, 0:LANES] = hi
            split_ref[n, :, LANES:] = (sp - hi.astype(F32)).astype(BF16)

        def suffix_sums(g):
            heads_g = slice(g * SB_SUM_GROUP, (g + 1) * SB_SUM_GROUP)
            sums = jnp.dot(split_ref[heads_g].reshape(SB_SUM_GROUP * BLOCK, 2 * LANES), tri_ref[...],
                           preferred_element_type=F32)
            sums_ref[heads_g] = sums.reshape(SB_SUM_GROUP, BLOCK, 2 * LANES)

        def carry_update(n, least):
            new_carry = carry_ref[n] + sums_ref[n, :, LANES:]
            carry_ref[n] = new_carry
            return new_carry if least is None else jnp.minimum(least, new_carry)

        def weights(n):
            w = jnp.exp(z_ref[n] - sums_ref[n, :, 0:LANES])
            w_ref[n] = w.astype(BF16)

        def accumulate(p):
            cols = slice(p * LANES, (p + 1) * LANES)
            o = jnp.dot(w_ref[2 * p:2 * p + 2].reshape(2 * BLOCK, LANES), v_ref[0, pl.ds(off, BLOCK), cols],
                        preferred_element_type=F32)
            acc_ref[p] += jnp.where(low_half, o[:BLOCK], o[BLOCK:])

        pairs, heads = range(SB_PAIRS), range(2 * SB_PAIRS)
        for p in pairs:
            scores(p)
        for n in heads:
            softplus_split(n)
        for g in range(2 * SB_PAIRS // SB_SUM_GROUP):
            suffix_sums(g)
        least = None
        for n in heads:
            least = carry_update(n, least)
        least_mass = jnp.min(least)
        for n in heads:
            weights(n)
        for p in pairs:
            accumulate(p)
        return least_mass

    def keep_going(state):
        k_blk, least_mass = state
        return (k_blk >= 0) & (least_mass < SB_MASS_CUTOFF)

    def body(state):
        k_blk, _ = state
        return k_blk - 1, visit(k_blk)

    lax.while_loop(keep_going, body, (q_blk, jnp.float32(0.0)))

    for p in range(SB_PAIRS):
        cols = slice(p * LANES, (p + 1) * LANES)
        out_ref[0, q_rows, cols] = (acc_ref[p] * gate_ref[0, q_rows, cols]).astype(BF16)


def _sb_attn(qkv, gate, tri):
    batch, lp, _ = qkv.shape
    seq = lp - BLOCK
    q_tile = SB_Q_PER_STEP * BLOCK
    seq_cols = lambda col: pl.BlockSpec((1, lp, SB_WIDTH), lambda b, i: (b, 0, col))
    return pl.pallas_call(
        _sb_attn_kernel,
        grid=(batch, seq // q_tile),
        in_specs=[
            pl.BlockSpec((1, q_tile, SB_WIDTH), lambda b, i: (b, i, 0)),
            seq_cols(1), seq_cols(2),
            pl.BlockSpec((1, q_tile, SB_WIDTH), lambda b, i: (b, i, 0)),
            pl.BlockSpec((2 * BLOCK, 2 * LANES), lambda b, i: (0, 0)),
        ],
        out_specs=pl.BlockSpec((1, q_tile, SB_WIDTH), lambda b, i: (b, i, 0)),
        out_shape=jax.ShapeDtypeStruct((batch, seq, SB_WIDTH), BF16),
        scratch_shapes=[
            pltpu.VMEM((2 * SB_PAIRS, BLOCK, LANES), F32),
            pltpu.VMEM((SB_PAIRS, BLOCK, LANES), F32),
            pltpu.VMEM((2 * SB_PAIRS, BLOCK, LANES), BF16),
            pltpu.VMEM((2 * SB_PAIRS, BLOCK, LANES), F32),
            pltpu.VMEM((2 * SB_PAIRS, BLOCK, 2 * LANES), BF16),
            pltpu.VMEM((2 * SB_PAIRS, BLOCK, 2 * LANES), F32),
            pltpu.VMEM((2 * SB_PAIRS, BLOCK, LANES), BF16),
            pltpu.VMEM((BLOCK, LANES), F32),
        ],
        compiler_params=_params("parallel", "arbitrary"),
        name="sb_attn",
    )(qkv, qkv, qkv, gate, tri)


def _sb_out_kernel(o_ref, h_ref, w_ref, post_ref, out_ref):
    y = jnp.dot(o_ref[0], w_ref[...], preferred_element_type=F32)
    out_ref[0] = h_ref[0] + _rms_normed(y, post_ref[...])


def _sb_out(o_gated, h, w_bf, post):
    batch, rows, _ = o_gated.shape
    return pl.pallas_call(
        _sb_out_kernel,
        grid=(batch, rows // SB_OUT_ROW_TILE),
        in_specs=[
            pl.BlockSpec((1, SB_OUT_ROW_TILE, SB_WIDTH), lambda b, i: (b, i, 0)),
            pl.BlockSpec((1, SB_OUT_ROW_TILE, D_MODEL), lambda b, i: (b, i, 0)),
            pl.BlockSpec((SB_WIDTH, D_MODEL), lambda b, i: (0, 0)),
            pl.BlockSpec((1, D_MODEL), lambda b, i: (0, 0)),
        ],
        out_specs=pl.BlockSpec((1, SB_OUT_ROW_TILE, D_MODEL), lambda b, i: (b, i, 0)),
        out_shape=jax.ShapeDtypeStruct((batch, rows, D_MODEL), F32),
        compiler_params=_params("parallel", "parallel"),
        name="sb_out_proj",
    )(o_gated, h, w_bf, post)


def _rope_tables(seq):
    half = HEAD_DIM // 2
    pos = jnp.concatenate([N_META + jnp.arange(seq), jnp.zeros((META_PAD,), jnp.int32), jnp.arange(N_META)])
    inv = ROPE_THETA ** (-jnp.arange(half, dtype=jnp.float32) / half)
    ang = pos.astype(jnp.float32)[:, None] * inv[None, :]
    cos = jnp.cos(ang)
    sin = jnp.sin(ang)
    cos_t = jnp.tile(jnp.concatenate([cos, cos], axis=-1), (1, LANES // HEAD_DIM))
    sin_t = jnp.tile(jnp.concatenate([-sin, sin], axis=-1), (1, LANES // HEAD_DIM))
    return cos_t, sin_t


def kernel(x, meta_tokens, ab_pre_norm, ab_w_in, ab_sinks, ab_conv_w, ab_conv_b, ab_conv_ln_g, ab_conv_ln_b, ab_w_pw2, ab_w_out, ab_post_norm, sb_pre_norm, sb_w_in, sb_w_out, sb_post_norm):
    batch, seq, _ = x.shape
    assert seq % ROW_TILE == 0
    tiles_per_seq = seq // ROW_TILE
    n_blocks = seq // BLOCK

    x_flat = x.reshape(batch * seq, D_MODEL)
    meta_blk = jnp.concatenate([jnp.zeros((META_PAD, D_MODEL), x.dtype), meta_tokens.astype(x.dtype)], axis=0)
    cos_t, sin_t = _rope_tables(seq)
    pre0, post0, pre1 = ab_pre_norm[0][None], ab_post_norm[0][None], sb_pre_norm[0][None]
    w_in0, w_pw2, w_out0 = ab_w_in[0].astype(BF16), ab_w_pw2[0].astype(BF16), ab_w_out[0].astype(BF16)
    w_in1, w_out1 = sb_w_in[0].astype(BF16), sb_w_out[0].astype(BF16)

    qkv, rest = _ab_in(x_flat, pre0, w_in0, cos_t, sin_t, AB_IN_ROW_TILE, lambda r: r % (seq // AB_IN_ROW_TILE))
    qkv_m, rest_m = _ab_in(meta_blk, pre0, w_in0, cos_t, sin_t, BLOCK, lambda r: n_blocks)
    qkv_m, rest_m = qkv_m[None], rest_m[None]
    conv_w = jnp.concatenate([ab_conv_w[0], jnp.zeros((1, CONV_CHANNELS), F32)], axis=0)
    mix_args = (conv_w, ab_conv_b[0][None], ab_conv_ln_g[0][None], ab_conv_ln_b[0][None], w_pw2)
    mix = _ab_mix(ab_sinks[0], qkv.reshape(batch, seq, -1), rest.reshape(batch, seq, -1), qkv_m, rest_m, *mix_args, 1)
    mix_m = _ab_mix(ab_sinks[0], qkv_m, rest_m, qkv_m, rest_m, *mix_args, 0)

    mid_args = (w_out0, post0, pre1, w_in1)
    _, qkv1_m, _ = _mid(mix_m[0], meta_blk, *mid_args, BLOCK, 1)
    h1, qkv1, gate1 = _mid(mix.reshape(batch * seq, AB_MIX), x_flat, *mid_args, ROW_TILE, tiles_per_seq, qkv1_m)

    r = jnp.arange(BLOCK)
    tri = jnp.concatenate([(r[:, None] >= r[None, :]).astype(BF16), jnp.ones((BLOCK, LANES), BF16)], axis=1)
    tri = jnp.concatenate([tri, tri], axis=0)
    o_gated = _sb_attn(qkv1, gate1.reshape(batch, seq, -1), tri)
    return _sb_out(o_gated, h1.reshape(batch, seq, D_MODEL), w_out1, sb_post_norm[0][None])
```

```python
import functools

import jax
import jax.numpy as jnp
from jax import lax
from jax.experimental import pallas as pl
from jax.experimental.pallas import tpu as pltpu

D_MODEL = 1024
N_META = 16
BLOCK = 128
META_PAD = BLOCK - N_META
HEAD_DIM = 64
ROPE_THETA = 10000.0
NORM_EPS = 1e-6
NEG_INF = -1e30
SWA_HEADS = 8
SWA_KV_HEADS = 2
SWA_GROUP = SWA_HEADS // SWA_KV_HEADS
SWA_WINDOW = 128
SWA_WIDTH = SWA_HEADS * HEAD_DIM
SWA_KV_WIDTH = SWA_KV_HEADS * HEAD_DIM
CONV_CHANNELS = 512
CONV_WIDTH = 31
CONV_LN_EPS = 1e-5
SB_HEADS = 16
SB_WIDTH = SB_HEADS * HEAD_DIM
AB_IN = 2 * SWA_WIDTH + 2 * SWA_KV_WIDTH + 3 * CONV_CHANNELS
AB_MIX = SWA_WIDTH + CONV_CHANNELS
SB_IN = 4 * SB_WIDTH

LANES = 128
SUBLANES = 8
ROW_TILE = 512
AB_IN_ROW_TILE = 1024
SB_OUT_ROW_TILE = 1024
VMEM_LIMIT = 48 * 1024 * 1024
CONV_HALO = -(-(CONV_WIDTH - 1) // SUBLANES) * SUBLANES
CONV_WIN_START = BLOCK - CONV_HALO
CONV_WIN_ROWS = BLOCK + CONV_HALO
CONV_LEAD = CONV_HALO - (CONV_WIDTH - 1)

SB_PAIRS = SB_WIDTH // LANES
SB_Q_PER_STEP = 8
SB_SUM_GROUP = 8
SB_MASS_CUTOFF = 104.0
LOG2_E = 1.4426950408889634

F32 = jnp.float32
BF16 = jnp.bfloat16


def _rms_normed(x, gain):
    return x * lax.rsqrt(jnp.mean(x * x, axis=-1, keepdims=True) + NORM_EPS) * gain


def _silu(x):
    return x * jax.nn.sigmoid(x)


def _params(*semantics):
    return pltpu.CompilerParams(dimension_semantics=semantics, vmem_limit_bytes=VMEM_LIMIT)


def _ab_in_kernel(h_ref, gain_ref, w_ref, cos_ref, sin_ref, qkv_ref, rest_ref):
    hn = _rms_normed(h_ref[...], gain_ref[...]).astype(BF16)

    def proj(lo, hi):
        return jnp.dot(hn, w_ref[:, lo:hi], preferred_element_type=F32)

    cos = cos_ref[...]
    sin = sin_ref[...]
    lane = lax.broadcasted_iota(jnp.int32, cos.shape, 1)
    first_half = (lane % HEAD_DIM) < (HEAD_DIM // 2)

    def rope(t):
        rot = jnp.where(first_half, pltpu.roll(t, LANES - HEAD_DIM // 2, 1), pltpu.roll(t, HEAD_DIM // 2, 1))
        return t * cos + rot * sin

    q = proj(0, SWA_WIDTH)
    for c in range(SWA_WIDTH // LANES):
        cols = slice(c * LANES, (c + 1) * LANES)
        qkv_ref[:, cols] = (rope(q[:, cols]) * HEAD_DIM ** -0.5).astype(BF16)
    v_lo = SWA_WIDTH + SWA_KV_WIDTH
    kv = proj(SWA_WIDTH, v_lo + SWA_KV_WIDTH)
    qkv_ref[:, SWA_WIDTH:v_lo] = rope(kv[:, :SWA_KV_WIDTH]).astype(BF16)
    qkv_ref[:, v_lo:v_lo + SWA_KV_WIDTH] = kv[:, SWA_KV_WIDTH:].astype(BF16)

    ga_lo = v_lo + SWA_KV_WIDTH
    rest_ref[:, 0:SWA_WIDTH] = _silu(proj(ga_lo, ga_lo + SWA_WIDTH))
    glu_lo = ga_lo + SWA_WIDTH
    glu_a = proj(glu_lo, glu_lo + CONV_CHANNELS)
    glu_b = proj(glu_lo + CONV_CHANNELS, glu_lo + 2 * CONV_CHANNELS)
    rest_ref[:, SWA_WIDTH:SWA_WIDTH + CONV_CHANNELS] = glu_a * jax.nn.sigmoid(glu_b)
    gb_lo = glu_lo + 2 * CONV_CHANNELS
    rest_ref[:, SWA_WIDTH + CONV_CHANNELS:] = _silu(proj(gb_lo, gb_lo + CONV_CHANNELS))


def _ab_in(h_flat, gain, w_bf, cos_t, sin_t, row_tile, table_block):
    rows = h_flat.shape[0]
    qkv_w = SWA_WIDTH + 2 * SWA_KV_WIDTH
    rest_w = SWA_WIDTH + 2 * CONV_CHANNELS
    return pl.pallas_call(
        _ab_in_kernel,
        grid=(rows // row_tile,),
        in_specs=[
            pl.BlockSpec((row_tile, D_MODEL), lambda r: (r, 0)),
            pl.BlockSpec((1, D_MODEL), lambda r: (0, 0)),
            pl.BlockSpec((D_MODEL, AB_IN), lambda r: (0, 0)),
            pl.BlockSpec((row_tile, LANES), lambda r: (table_block(r), 0)),
            pl.BlockSpec((row_tile, LANES), lambda r: (table_block(r), 0)),
        ],
        out_specs=[
            pl.BlockSpec((row_tile, qkv_w), lambda r: (r, 0)),
            pl.BlockSpec((row_tile, rest_w), lambda r: (r, 0)),
        ],
        out_shape=[
            jax.ShapeDtypeStruct((rows, qkv_w), BF16),
            jax.ShapeDtypeStruct((rows, rest_w), F32),
        ],
        compiler_params=_params("parallel"),
        name="ab_in_proj",
    )(h_flat, gain, w_bf, cos_t, sin_t)


def _ab_mix_kernel(first_ref_blk, sinks_ref, qkv_ref, kv_prev_ref, kv_meta_ref, rest_ref, u_prev_ref, u_meta_ref,
                   conv_w_ref, conv_b_ref, ln_g_ref, ln_b_ref, w_pw2_ref,
                   out_ref, ext_ref, rot_ref, s_ref, p_ref):
    ref_blk = pl.program_id(1) + first_ref_blk
    prev_is_meta = ref_blk == 1
    qkv = qkv_ref[0]
    kv_meta = kv_meta_ref[0]
    kv_prev = jnp.where(prev_is_meta, kv_meta, kv_prev_ref[0])

    k_cat = jnp.concatenate([kv_meta[:, :LANES], kv_prev[:, :LANES], qkv[:, SWA_WIDTH:SWA_WIDTH + LANES]], axis=0)
    v_cat = jnp.concatenate([kv_meta[:, LANES:], kv_prev[:, LANES:], qkv[:, SWA_WIDTH + LANES:]], axis=0)
    k_swap = jnp.concatenate([k_cat[:, HEAD_DIM:], k_cat[:, :HEAD_DIM]], axis=1)
    v_swap = jnp.concatenate([v_cat[:, HEAD_DIM:], v_cat[:, :HEAD_DIM]], axis=1)

    row = lax.broadcasted_iota(jnp.int32, (BLOCK, BLOCK), 0)
    col = lax.broadcasted_iota(jnp.int32, (BLOCK, BLOCK), 1)
    qpos = ref_blk * BLOCK + row
    meta_ok = (col >= META_PAD) & (qpos - col >= SWA_WINDOW)
    ppos = (ref_blk - 1) * BLOCK + col
    prev_ok = (ppos >= META_PAD) & (qpos - ppos < SWA_WINDOW)
    cpos = ref_blk * BLOCK + col
    cur_ok = (cpos >= META_PAD) & (qpos >= cpos)
    mask = jnp.concatenate([meta_ok, prev_ok, cur_ok], axis=1)

    lane = lax.broadcasted_iota(jnp.int32, (BLOCK, LANES), 1)
    low_half = lane < HEAD_DIM
    zero_bf = jnp.zeros((), BF16)

    direct = [h for h in range(SWA_HEADS) if h // SWA_GROUP == h % 2]
    order = direct + [h for h in range(SWA_HEADS) if h not in direct]
    n_direct = len(direct)

    def head_q(head):
        q_pair = qkv[:, (head // 2) * LANES:(head // 2 + 1) * LANES]
        return jnp.where(low_half if head % 2 == 0 else ~low_half, q_pair, zero_bf)

    for slots, k_use in ((slice(0, n_direct), k_cat), (slice(n_direct, SWA_HEADS), k_swap)):
        q_stack = jnp.concatenate([head_q(h) for h in order[slots]], axis=0)
        s = lax.dot_general(q_stack, k_use, (((1,), (1,)), ((), ())), preferred_element_type=F32)
        s_ref[slots] = s.reshape(-1, BLOCK, 3 * BLOCK)
    inv_denoms = {}
    for slot, head in enumerate(order):
        s = jnp.where(mask, s_ref[slot], NEG_INF)
        sink = sinks_ref[head]
        m = jnp.maximum(jnp.max(s, axis=-1, keepdims=True), sink)
        p = jnp.exp(s - m)
        inv_denoms[head] = 1.0 / (jnp.sum(p, axis=-1, keepdims=True) + jnp.exp(sink - m))
        p_ref[slot] = p.astype(BF16)
    outs = {}
    for slots, v_use in ((slice(0, n_direct), v_cat), (slice(n_direct, SWA_HEADS), v_swap)):
        o = jnp.dot(p_ref[slots].reshape(-1, 3 * BLOCK), v_use, preferred_element_type=F32)
        for j, head in enumerate(order[slots]):
            outs[head] = o[j * BLOCK:(j + 1) * BLOCK] * inv_denoms[head]
    rest = rest_ref[0]
    for pair in range(SWA_HEADS // 2):
        cols = slice(pair * LANES, (pair + 1) * LANES)
        out_ref[0, :, cols] = (jnp.where(low_half, outs[2 * pair], outs[2 * pair + 1]) * rest[:, cols]).astype(BF16)

    u_prev = jnp.where(prev_is_meta, u_meta_ref[0], u_prev_ref[0])
    ext_ref[0:BLOCK, :] = jnp.where(ref_blk > 0, u_prev, 0.0)
    ext_ref[BLOCK:2 * BLOCK, :] = rest[:, SWA_WIDTH:SWA_WIDTH + CONV_CHANNELS]
    window = ext_ref[CONV_WIN_START:, :]
    for b in range(1, SUBLANES):
        rot_ref[b - 1] = pltpu.roll(window, CONV_WIN_ROWS - b, 0)
    conv_groups = []
    for cg in range(CONV_CHANNELS // LANES):
        cols = slice(cg * LANES, (cg + 1) * LANES)
        acc = jnp.zeros((BLOCK, LANES), F32) + conv_b_ref[:, cols]
        for w in range(CONV_WIDTH):
            a, b = divmod(CONV_LEAD + w, SUBLANES)
            if b == 0:
                taps = ext_ref[CONV_WIN_START + SUBLANES * a:CONV_WIN_START + SUBLANES * a + BLOCK, cols]
            else:
                taps = rot_ref[b - 1, SUBLANES * a:SUBLANES * a + BLOCK, cols]
            acc = acc + taps * conv_w_ref[w:w + 1, cols]
        conv_groups.append(acc)
    conv = jnp.concatenate(conv_groups, axis=1)

    mu = jnp.mean(conv, axis=-1, keepdims=True)
    xc = conv - mu
    ln = xc * lax.rsqrt(jnp.mean(xc * xc, axis=-1, keepdims=True) + CONV_LN_EPS) * ln_g_ref[...] + ln_b_ref[...]
    c_act = _silu(ln).astype(BF16)
    c_branch = jnp.dot(c_act, w_pw2_ref[...], preferred_element_type=F32) * rest[:, SWA_WIDTH + CONV_CHANNELS:]
    out_ref[0, :, SWA_WIDTH:] = c_branch.astype(BF16)


def _ab_mix(sinks, qkv, rest, qkv_meta, rest_meta, conv_w, conv_b, ln_g, ln_b, w_pw2, first_ref_blk):
    batch, rows, qkv_w = qkv.shape
    rest_w = rest.shape[-1]
    kv_blk = (SWA_WIDTH // (2 * SWA_KV_WIDTH))
    prev_blk = lambda i: jnp.maximum(i - 1, 0)
    const2 = lambda b, i: (0, 0)
    return pl.pallas_call(
        functools.partial(_ab_mix_kernel, first_ref_blk),
        grid=(batch, rows // BLOCK),
        in_specs=[
            pl.BlockSpec(memory_space=pltpu.SMEM),
            pl.BlockSpec((1, BLOCK, qkv_w), lambda b, i: (b, i, 0)),
            pl.BlockSpec((1, BLOCK, 2 * SWA_KV_WIDTH), lambda b, i: (b, prev_blk(i), kv_blk)),
            pl.BlockSpec((1, BLOCK, 2 * SWA_KV_WIDTH), lambda b, i: (0, 0, kv_blk)),
            pl.BlockSpec((1, BLOCK, rest_w), lambda b, i: (b, i, 0)),
            pl.BlockSpec((1, BLOCK, CONV_CHANNELS), lambda b, i: (b, prev_blk(i), 1)),
            pl.BlockSpec((1, BLOCK, CONV_CHANNELS), lambda b, i: (0, 0, 1)),
            pl.BlockSpec((CONV_WIDTH + 1, CONV_CHANNELS), const2),
            pl.BlockSpec((1, CONV_CHANNELS), const2),
            pl.BlockSpec((1, CONV_CHANNELS), const2),
            pl.BlockSpec((1, CONV_CHANNELS), const2),
            pl.BlockSpec((CONV_CHANNELS, CONV_CHANNELS), const2),
        ],
        out_specs=pl.BlockSpec((1, BLOCK, AB_MIX), lambda b, i: (b, i, 0)),
        out_shape=jax.ShapeDtypeStruct((batch, rows, AB_MIX), BF16),
        scratch_shapes=[
            pltpu.VMEM((2 * BLOCK, CONV_CHANNELS), F32),
            pltpu.VMEM((SUBLANES - 1, CONV_WIN_ROWS, CONV_CHANNELS), F32),
            pltpu.VMEM((SWA_HEADS, BLOCK, 3 * BLOCK), F32),
            pltpu.VMEM((SWA_HEADS, BLOCK, 3 * BLOCK), BF16),
        ],
        compiler_params=_params("parallel", "parallel"),
        name="ab_mix",
    )(sinks, qkv, qkv, qkv_meta, rest, rest, rest_meta, conv_w, conv_b, ln_g, ln_b, w_pw2)


def _mid_kernel(mix_ref, h_ref, w_out_ref, post_ref, gain_ref, w_in_ref, h1_ref, qkv_ref, gate_ref):
    y = jnp.dot(mix_ref[...], w_out_ref[...], preferred_element_type=F32)
    h1 = h_ref[...] + _rms_normed(y, post_ref[...])
    h1_ref[...] = h1
    hn = _rms_normed(h1, gain_ref[...]).astype(BF16)
    chunk = 4 * LANES

    def proj(lo):
        return jnp.dot(hn, w_in_ref[:, lo:lo + chunk], preferred_element_type=F32)

    for c in range(SB_WIDTH // chunk):
        lo = c * chunk
        qkv_ref[0, :, lo:lo + chunk] = (proj(lo) * HEAD_DIM ** -0.5).astype(BF16)
    for c in range(SB_WIDTH // chunk, 3 * SB_WIDTH // chunk):
        lo = c * chunk
        qkv_ref[0, :, lo:lo + chunk] = proj(lo).astype(BF16)
    for c in range(SB_WIDTH // chunk):
        lo = c * chunk
        gate_ref[:, lo:lo + chunk] = _silu(proj(3 * SB_WIDTH + lo))


def _mid_tail_kernel(n_tiles, mix_ref, h_ref, w_out_ref, post_ref, gain_ref, w_in_ref, tail_ref,
                     h1_ref, qkv_ref, gate_ref):
    @pl.when(pl.program_id(0) < n_tiles)
    def _():
        _mid_kernel(mix_ref, h_ref, w_out_ref, post_ref, gain_ref, w_in_ref, h1_ref, qkv_ref, gate_ref)

    @pl.when(pl.program_id(0) >= n_tiles)
    def _():
        qkv_ref[0, 0:tail_ref.shape[1], :] = tail_ref[0]


def _mid(mix_flat, h_flat, w_out_bf, post, gain, w_in_bf, row_tile, tiles_per_seq, qkv_tail=None):
    rows = h_flat.shape[0]
    n_tiles = rows // row_tile
    n_seq = n_tiles // tiles_per_seq
    seq_rows = tiles_per_seq * row_tile
    const = lambda shape: pl.BlockSpec(shape, lambda r: (0,) * len(shape))
    weights = [const((AB_MIX, D_MODEL)), const((1, D_MODEL)), const((1, D_MODEL)), const((D_MODEL, SB_IN))]
    if qkv_tail is None:
        tile = lambda r: r
        qkv_index = lambda r: (r // tiles_per_seq, r % tiles_per_seq, 0)
        body, steps, tail_specs, tail_args, tail_rows = _mid_kernel, n_tiles, [], (), 0
    else:
        tile = lambda r: jnp.minimum(r, n_tiles - 1)
        qkv_index = lambda r: (jnp.where(r < n_tiles, r // tiles_per_seq, r - n_tiles),
                               jnp.where(r < n_tiles, r % tiles_per_seq, tiles_per_seq), 0)
        body, steps = functools.partial(_mid_tail_kernel, n_tiles), n_tiles + n_seq
        tail_specs, tail_args, tail_rows = [const(qkv_tail.shape)], (qkv_tail,), qkv_tail.shape[1]
    row_spec = lambda width: pl.BlockSpec((row_tile, width), lambda r: (tile(r), 0))
    return pl.pallas_call(
        body,
        grid=(steps,),
        in_specs=[row_spec(AB_MIX), row_spec(D_MODEL)] + weights + tail_specs,
        out_specs=[row_spec(D_MODEL), pl.BlockSpec((1, row_tile, 3 * SB_WIDTH), qkv_index), row_spec(SB_WIDTH)],
        out_shape=[
            jax.ShapeDtypeStruct((rows, D_MODEL), F32),
            jax.ShapeDtypeStruct((n_seq, seq_rows + tail_rows, 3 * SB_WIDTH), BF16),
            jax.ShapeDtypeStruct((rows, SB_WIDTH), F32),
        ],
        compiler_params=_params("arbitrary"),
        name="ab_out_sb_in_proj",
    )(mix_flat, h_flat, w_out_bf, post, gain, w_in_bf, *tail_args)


def _sb_attn_kernel(q_ref, *refs):
    def one_query_block(sub, _):
        q_blk = pl.program_id(1) * SB_Q_PER_STEP + sub + 1
        _sb_query_block(q_blk, pl.ds(pl.multiple_of(sub * BLOCK, BLOCK), BLOCK), q_ref, *refs)
        return 0

    lax.fori_loop(0, SB_Q_PER_STEP, one_query_block, 0)


def _sb_query_block(q_blk, q_rows, q_ref, k_ref, v_ref, gate_ref, tri_ref, out_ref,
                    carry_ref, acc_ref, qh_ref, z_ref, split_ref, sums_ref, w_ref, bias_ref):
    meta_off = k_ref.shape[1] - BLOCK
    lane = lax.broadcasted_iota(jnp.int32, (BLOCK, LANES), 1)
    row = lax.broadcasted_iota(jnp.int32, (BLOCK, LANES), 0)
    low_half = lane < HEAD_DIM
    zero_bf = jnp.zeros((), BF16)

    qpos = q_blk * BLOCK + row
    carry_ref[...] = jnp.zeros_like(carry_ref)
    acc_ref[...] = jnp.zeros_like(acc_ref)
    for n in range(2 * SB_PAIRS):
        q_pair = q_ref[0, q_rows, (n // 2) * LANES:(n // 2 + 1) * LANES]
        qh_ref[n] = jnp.where(low_half, q_pair, zero_bf) if n % 2 == 0 else jnp.where(low_half, zero_bf, q_pair)

    def visit(k_blk):
        off = pl.multiple_of(jnp.where(k_blk == 0, meta_off, (k_blk - 1) * BLOCK), BLOCK)
        kpos = k_blk * BLOCK + lane

        def scores(p):
            k_pair = k_ref[0, pl.ds(off, BLOCK), p * LANES:(p + 1) * LANES]
            q_both = qh_ref[2 * p:2 * p + 2].reshape(2 * BLOCK, LANES)
            z = lax.dot_general(q_both, k_pair, (((1,), (1,)), ((), ())), preferred_element_type=F32)
            z_ref[2 * p:2 * p + 2] = z.reshape(2, BLOCK, LANES)

        bias_ref[...] = jnp.where((kpos >= META_PAD) & (kpos < qpos), 0.0, NEG_INF)

        def softplus_split(n):
            z = z_ref[n] + bias_ref[...]
            z_ref[n] = z - carry_ref[n]
            sp = jnp.maximum(z, 0.0) + jnp.log(1.0 + jnp.exp2(jnp.abs(z) * -LOG2_E))
            hi = sp.astype(BF16)
            split_ref[n, :, 0:LANES] = hi
            split_ref[n, :, LANES:] = (sp - hi.astype(F32)).astype(BF16)

        def suffix_sums(g):
            heads_g = slice(g * SB_SUM_GROUP, (g + 1) * SB_SUM_GROUP)
            sums = jnp.dot(split_ref[heads_g].reshape(SB_SUM_GROUP * BLOCK, 2 * LANES), tri_ref[...],
                           preferred_element_type=F32)
            sums_ref[heads_g] = sums.reshape(SB_SUM_GROUP, BLOCK, 2 * LANES)

        def carry_update(n, least):
            new_carry = carry_ref[n] + sums_ref[n, :, LANES:]
            carry_ref[n] = new_carry
            return new_carry if least is None else jnp.minimum(least, new_carry)

        def weights(n):
            w = jnp.exp(z_ref[n] - sums_ref[n, :, 0:LANES])
            w_ref[n] = w.astype(BF16)

        def accumulate(p):
            cols = slice(p * LANES, (p + 1) * LANES)
            o = jnp.dot(w_ref[2 * p:2 * p + 2].reshape(2 * BLOCK, LANES), v_ref[0, pl.ds(off, BLOCK), cols],
                        preferred_element_type=F32)
            acc_ref[p] += jnp.where(low_half, o[:BLOCK], o[BLOCK:])

        pairs, heads = range(SB_PAIRS), range(2 * SB_PAIRS)
        for p in pairs:
            scores(p)
        for n in heads:
            softplus_split(n)
        for g in range(2 * SB_PAIRS // SB_SUM_GROUP):
            suffix_sums(g)
        least = None
        for n in heads:
            least = carry_update(n, least)
        least_mass = jnp.min(least)
        for n in heads:
            weights(n)
        for p in pairs:
            accumulate(p)
        return least_mass

    def keep_going(state):
        k_blk, least_mass = state
        return (k_blk >= 0) & (least_mass < SB_MASS_CUTOFF)

    def body(state):
        k_blk, _ = state
        return k_blk - 1, visit(k_blk)

    lax.while_loop(keep_going, body, (q_blk, jnp.float32(0.0)))

    for p in range(SB_PAIRS):
        cols = slice(p * LANES, (p + 1) * LANES)
        out_ref[0, q_rows, cols] = (acc_ref[p] * gate_ref[0, q_rows, cols]).astype(BF16)


def _sb_attn(qkv, gate, tri):
    batch, lp, _ = qkv.shape
    seq = lp - BLOCK
    q_tile = SB_Q_PER_STEP * BLOCK
    seq_cols = lambda col: pl.BlockSpec((1, lp, SB_WIDTH), lambda b, i: (b, 0, col))
    return pl.pallas_call(
        _sb_attn_kernel,
        grid=(batch, seq // q_tile),
        in_specs=[
            pl.BlockSpec((1, q_tile, SB_WIDTH), lambda b, i: (b, i, 0)),
            seq_cols(1), seq_cols(2),
            pl.BlockSpec((1, q_tile, SB_WIDTH), lambda b, i: (b, i, 0)),
            pl.BlockSpec((2 * BLOCK, 2 * LANES), lambda b, i: (0, 0)),
        ],
        out_specs=pl.BlockSpec((1, q_tile, SB_WIDTH), lambda b, i: (b, i, 0)),
        out_shape=jax.ShapeDtypeStruct((batch, seq, SB_WIDTH), BF16),
        scratch_shapes=[
            pltpu.VMEM((2 * SB_PAIRS, BLOCK, LANES), F32),
            pltpu.VMEM((SB_PAIRS, BLOCK, LANES), F32),
            pltpu.VMEM((2 * SB_PAIRS, BLOCK, LANES), BF16),
            pltpu.VMEM((2 * SB_PAIRS, BLOCK, LANES), F32),
            pltpu.VMEM((2 * SB_PAIRS, BLOCK, 2 * LANES), BF16),
            pltpu.VMEM((2 * SB_PAIRS, BLOCK, 2 * LANES), F32),
            pltpu.VMEM((2 * SB_PAIRS, BLOCK, LANES), BF16),
            pltpu.VMEM((BLOCK, LANES), F32),
        ],
        compiler_params=_params("parallel", "arbitrary"),
        name="sb_attn",
    )(qkv, qkv, qkv, gate, tri)


def _sb_out_kernel(o_ref, h_ref, w_ref, post_ref, out_ref):
    y = jnp.dot(o_ref[0], w_ref[...], preferred_element_type=F32)
    out_ref[0] = h_ref[0] + _rms_normed(y, post_ref[...])


def _sb_out(o_gated, h, w_bf, post):
    batch, rows, _ = o_gated.shape
    return pl.pallas_call(
        _sb_out_kernel,
        grid=(batch, rows // SB_OUT_ROW_TILE),
        in_specs=[
            pl.BlockSpec((1, SB_OUT_ROW_TILE, SB_WIDTH), lambda b, i: (b, i, 0)),
            pl.BlockSpec((1, SB_OUT_ROW_TILE, D_MODEL), lambda b, i: (b, i, 0)),
            pl.BlockSpec((SB_WIDTH, D_MODEL), lambda b, i: (0, 0)),
            pl.BlockSpec((1, D_MODEL), lambda b, i: (0, 0)),
        ],
        out_specs=pl.BlockSpec((1, SB_OUT_ROW_TILE, D_MODEL), lambda b, i: (b, i, 0)),
        out_shape=jax.ShapeDtypeStruct((batch, rows, D_MODEL), F32),
        compiler_params=_params("parallel", "parallel"),
        name="sb_out_proj",
    )(o_gated, h, w_bf, post)


def _rope_tables(seq):
    half = HEAD_DIM // 2
    pos = jnp.concatenate([N_META + jnp.arange(seq), jnp.zeros((META_PAD,), jnp.int32), jnp.arange(N_META)])
    inv = ROPE_THETA ** (-jnp.arange(half, dtype=jnp.float32) / half)
    ang = pos.astype(jnp.float32)[:, None] * inv[None, :]
    cos = jnp.cos(ang)
    sin = jnp.sin(ang)
    cos_t = jnp.tile(jnp.concatenate([cos, cos], axis=-1), (1, LANES // HEAD_DIM))
    sin_t = jnp.tile(jnp.concatenate([-sin, sin], axis=-1), (1, LANES // HEAD_DIM))
    return cos_t, sin_t


def kernel(x, meta_tokens, ab_pre_norm, ab_w_in, ab_sinks, ab_conv_w, ab_conv_b, ab_conv_ln_g, ab_conv_ln_b, ab_w_pw2, ab_w_out, ab_post_norm, sb_pre_norm, sb_w_in, sb_w_out, sb_post_norm):
    batch, seq, _ = x.shape
    assert seq % ROW_TILE == 0
    tiles_per_seq = seq // ROW_TILE
    n_blocks = seq // BLOCK

    x_flat = x.reshape(batch * seq, D_MODEL)
    meta_blk = jnp.concatenate([jnp.zeros((META_PAD, D_MODEL), x.dtype), meta_tokens.astype(x.dtype)], axis=0)
    cos_t, sin_t = _rope_tables(seq)
    pre0, post0, pre1 = ab_pre_norm[0][None], ab_post_norm[0][None], sb_pre_norm[0][None]
    w_in0, w_pw2, w_out0 = ab_w_in[0].astype(BF16), ab_w_pw2[0].astype(BF16), ab_w_out[0].astype(BF16)
    w_in1, w_out1 = sb_w_in[0].astype(BF16), sb_w_out[0].astype(BF16)

    qkv, rest = _ab_in(x_flat, pre0, w_in0, cos_t, sin_t, AB_IN_ROW_TILE, lambda r: r % (seq // AB_IN_ROW_TILE))
    qkv_m, rest_m = _ab_in(meta_blk, pre0, w_in0, cos_t, sin_t, BLOCK, lambda r: n_blocks)
    qkv_m, rest_m = qkv_m[None], rest_m[None]
    conv_w = jnp.concatenate([ab_conv_w[0], jnp.zeros((1, CONV_CHANNELS), F32)], axis=0)
    mix_args = (conv_w, ab_conv_b[0][None], ab_conv_ln_g[0][None], ab_conv_ln_b[0][None], w_pw2)
    mix = _ab_mix(ab_sinks[0], qkv.reshape(batch, seq, -1), rest.reshape(batch, seq, -1), qkv_m, rest_m, *mix_args, 1)
    mix_m = _ab_mix(ab_sinks[0], qkv_m, rest_m, qkv_m, rest_m, *mix_args, 0)

    mid_args = (w_out0, post0, pre1, w_in1)
    _, qkv1_m, _ = _mid(mix_m[0], meta_blk, *mid_args, BLOCK, 1)
    h1, qkv1, gate1 = _mid(mix.reshape(batch * seq, AB_MIX), x_flat, *mid_args, ROW_TILE, tiles_per_seq, qkv1_m)

    r = jnp.arange(BLOCK)
    tri = jnp.concatenate([(r[:, None] >= r[None, :]).astype(BF16), jnp.ones((BLOCK, LANES), BF16)], axis=1)
    tri = jnp.concatenate([tri, tri], axis=0)
    o_gated = _sb_attn(qkv1, gate1.reshape(batch, seq, -1), tri)
    return _sb_out(o_gated, h1.reshape(batch, seq, D_MODEL), w_out1, sb_post_norm[0][None])
```

```python
import functools

import jax
import jax.numpy as jnp
from jax import lax
from jax.experimental import pallas as pl
from jax.experimental.pallas import tpu as pltpu

D_MODEL = 1024
N_META = 16
BLOCK = 128
META_PAD = BLOCK - N_META
HEAD_DIM = 64
ROPE_THETA = 10000.0
NORM_EPS = 1e-6
NEG_INF = -1e30
SWA_HEADS = 8
SWA_KV_HEADS = 2
SWA_GROUP = SWA_HEADS // SWA_KV_HEADS
SWA_WINDOW = 128
SWA_WIDTH = SWA_HEADS * HEAD_DIM
SWA_KV_WIDTH = SWA_KV_HEADS * HEAD_DIM
CONV_CHANNELS = 512
CONV_WIDTH = 31
CONV_LN_EPS = 1e-5
SB_HEADS = 16
SB_WIDTH = SB_HEADS * HEAD_DIM
AB_IN = 2 * SWA_WIDTH + 2 * SWA_KV_WIDTH + 3 * CONV_CHANNELS
AB_MIX = SWA_WIDTH + CONV_CHANNELS
SB_IN = 4 * SB_WIDTH

LANES = 128
SUBLANES = 8
ROW_TILE = 512
AB_IN_ROW_TILE = 1024
SB_OUT_ROW_TILE = 1024
MID_ROW_TILE = 1024
MID_VMEM_LIMIT = 58 * 1024 * 1024
VMEM_LIMIT = 48 * 1024 * 1024
CONV_HALO = -(-(CONV_WIDTH - 1) // SUBLANES) * SUBLANES
CONV_WIN_START = BLOCK - CONV_HALO
CONV_WIN_ROWS = BLOCK + CONV_HALO
CONV_LEAD = CONV_HALO - (CONV_WIDTH - 1)

SB_PAIRS = SB_WIDTH // LANES
SB_Q_PER_STEP = 8
SB_SUM_GROUP = 8
SB_MASS_CUTOFF = 104.0
LOG2_E = 1.4426950408889634

F32 = jnp.float32
BF16 = jnp.bfloat16


def _rms_normed(x, gain):
    return x * lax.rsqrt(jnp.mean(x * x, axis=-1, keepdims=True) + NORM_EPS) * gain


def _silu(x):
    return x * jax.nn.sigmoid(x)


def _params(*semantics):
    return pltpu.CompilerParams(dimension_semantics=semantics, vmem_limit_bytes=VMEM_LIMIT)


def _ab_in_kernel(h_ref, gain_ref, w_ref, cos_ref, sin_ref, qkv_ref, rest_ref):
    hn = _rms_normed(h_ref[...], gain_ref[...]).astype(BF16)

    def proj(lo, hi):
        return jnp.dot(hn, w_ref[:, lo:hi], preferred_element_type=F32)

    cos = cos_ref[...]
    sin = sin_ref[...]
    lane = lax.broadcasted_iota(jnp.int32, cos.shape, 1)
    first_half = (lane % HEAD_DIM) < (HEAD_DIM // 2)

    def rope(t):
        rot = jnp.where(first_half, pltpu.roll(t, LANES - HEAD_DIM // 2, 1), pltpu.roll(t, HEAD_DIM // 2, 1))
        return t * cos + rot * sin

    q = proj(0, SWA_WIDTH)
    for c in range(SWA_WIDTH // LANES):
        cols = slice(c * LANES, (c + 1) * LANES)
        qkv_ref[:, cols] = (rope(q[:, cols]) * HEAD_DIM ** -0.5).astype(BF16)
    v_lo = SWA_WIDTH + SWA_KV_WIDTH
    kv = proj(SWA_WIDTH, v_lo + SWA_KV_WIDTH)
    qkv_ref[:, SWA_WIDTH:v_lo] = rope(kv[:, :SWA_KV_WIDTH]).astype(BF16)
    qkv_ref[:, v_lo:v_lo + SWA_KV_WIDTH] = kv[:, SWA_KV_WIDTH:].astype(BF16)

    ga_lo = v_lo + SWA_KV_WIDTH
    rest_ref[:, 0:SWA_WIDTH] = _silu(proj(ga_lo, ga_lo + SWA_WIDTH))
    glu_lo = ga_lo + SWA_WIDTH
    glu_a = proj(glu_lo, glu_lo + CONV_CHANNELS)
    glu_b = proj(glu_lo + CONV_CHANNELS, glu_lo + 2 * CONV_CHANNELS)
    rest_ref[:, SWA_WIDTH:SWA_WIDTH + CONV_CHANNELS] = glu_a * jax.nn.sigmoid(glu_b)
    gb_lo = glu_lo + 2 * CONV_CHANNELS
    rest_ref[:, SWA_WIDTH + CONV_CHANNELS:] = _silu(proj(gb_lo, gb_lo + CONV_CHANNELS))


def _ab_in(h_flat, gain, w_bf, cos_t, sin_t, row_tile, table_block):
    rows = h_flat.shape[0]
    qkv_w = SWA_WIDTH + 2 * SWA_KV_WIDTH
    rest_w = SWA_WIDTH + 2 * CONV_CHANNELS
    return pl.pallas_call(
        _ab_in_kernel,
        grid=(rows // row_tile,),
        in_specs=[
            pl.BlockSpec((row_tile, D_MODEL), lambda r: (r, 0)),
            pl.BlockSpec((1, D_MODEL), lambda r: (0, 0)),
            pl.BlockSpec((D_MODEL, AB_IN), lambda r: (0, 0)),
            pl.BlockSpec((row_tile, LANES), lambda r: (table_block(r), 0)),
            pl.BlockSpec((row_tile, LANES), lambda r: (table_block(r), 0)),
        ],
        out_specs=[
            pl.BlockSpec((row_tile, qkv_w), lambda r: (r, 0)),
            pl.BlockSpec((row_tile, rest_w), lambda r: (r, 0)),
        ],
        out_shape=[
            jax.ShapeDtypeStruct((rows, qkv_w), BF16),
            jax.ShapeDtypeStruct((rows, rest_w), F32),
        ],
        compiler_params=_params("parallel"),
        name="ab_in_proj",
    )(h_flat, gain, w_bf, cos_t, sin_t)


def _ab_mix_kernel(first_ref_blk, sinks_ref, qkv_ref, kv_prev_ref, kv_meta_ref, rest_ref, u_prev_ref, u_meta_ref,
                   conv_w_ref, conv_b_ref, ln_g_ref, ln_b_ref, w_pw2_ref,
                   out_ref, ext_ref, rot_ref, s_ref, p_ref):
    ref_blk = pl.program_id(1) + first_ref_blk
    prev_is_meta = ref_blk == 1
    qkv = qkv_ref[0]
    kv_meta = kv_meta_ref[0]
    kv_prev = jnp.where(prev_is_meta, kv_meta, kv_prev_ref[0])

    k_cat = jnp.concatenate([kv_meta[:, :LANES], kv_prev[:, :LANES], qkv[:, SWA_WIDTH:SWA_WIDTH + LANES]], axis=0)
    v_cat = jnp.concatenate([kv_meta[:, LANES:], kv_prev[:, LANES:], qkv[:, SWA_WIDTH + LANES:]], axis=0)
    k_swap = jnp.concatenate([k_cat[:, HEAD_DIM:], k_cat[:, :HEAD_DIM]], axis=1)
    v_swap = jnp.concatenate([v_cat[:, HEAD_DIM:], v_cat[:, :HEAD_DIM]], axis=1)

    row = lax.broadcasted_iota(jnp.int32, (BLOCK, BLOCK), 0)
    col = lax.broadcasted_iota(jnp.int32, (BLOCK, BLOCK), 1)
    qpos = ref_blk * BLOCK + row
    meta_ok = (col >= META_PAD) & (qpos - col >= SWA_WINDOW)
    ppos = (ref_blk - 1) * BLOCK + col
    prev_ok = (ppos >= META_PAD) & (qpos - ppos < SWA_WINDOW)
    cpos = ref_blk * BLOCK + col
    cur_ok = (cpos >= META_PAD) & (qpos >= cpos)
    mask = jnp.concatenate([meta_ok, prev_ok, cur_ok], axis=1)

    lane = lax.broadcasted_iota(jnp.int32, (BLOCK, LANES), 1)
    low_half = lane < HEAD_DIM
    zero_bf = jnp.zeros((), BF16)

    direct = [h for h in range(SWA_HEADS) if h // SWA_GROUP == h % 2]
    order = direct + [h for h in range(SWA_HEADS) if h not in direct]
    n_direct = len(direct)

    def head_q(head):
        q_pair = qkv[:, (head // 2) * LANES:(head // 2 + 1) * LANES]
        return jnp.where(low_half if head % 2 == 0 else ~low_half, q_pair, zero_bf)

    for slots, k_use in ((slice(0, n_direct), k_cat), (slice(n_direct, SWA_HEADS), k_swap)):
        q_stack = jnp.concatenate([head_q(h) for h in order[slots]], axis=0)
        s = lax.dot_general(q_stack, k_use, (((1,), (1,)), ((), ())), preferred_element_type=F32)
        s_ref[slots] = s.reshape(-1, BLOCK, 3 * BLOCK)
    inv_denoms = {}
    for slot, head in enumerate(order):
        s = jnp.where(mask, s_ref[slot], NEG_INF)
        sink = sinks_ref[head]
        m = jnp.maximum(jnp.max(s, axis=-1, keepdims=True), sink)
        p = jnp.exp(s - m)
        inv_denoms[head] = 1.0 / (jnp.sum(p, axis=-1, keepdims=True) + jnp.exp(sink - m))
        p_ref[slot] = p.astype(BF16)
    outs = {}
    for slots, v_use in ((slice(0, n_direct), v_cat), (slice(n_direct, SWA_HEADS), v_swap)):
        o = jnp.dot(p_ref[slots].reshape(-1, 3 * BLOCK), v_use, preferred_element_type=F32)
        for j, head in enumerate(order[slots]):
            outs[head] = o[j * BLOCK:(j + 1) * BLOCK] * inv_denoms[head]
    rest = rest_ref[0]
    for pair in range(SWA_HEADS // 2):
        cols = slice(pair * LANES, (pair + 1) * LANES)
        out_ref[0, :, cols] = (jnp.where(low_half, outs[2 * pair], outs[2 * pair + 1]) * rest[:, cols]).astype(BF16)

    u_prev = jnp.where(prev_is_meta, u_meta_ref[0], u_prev_ref[0])
    ext_ref[0:BLOCK, :] = jnp.where(ref_blk > 0, u_prev, 0.0)
    ext_ref[BLOCK:2 * BLOCK, :] = rest[:, SWA_WIDTH:SWA_WIDTH + CONV_CHANNELS]
    window = ext_ref[CONV_WIN_START:, :]
    for b in range(1, SUBLANES):
        rot_ref[b - 1] = pltpu.roll(window, CONV_WIN_ROWS - b, 0)
    conv_groups = []
    for cg in range(CONV_CHANNELS // LANES):
        cols = slice(cg * LANES, (cg + 1) * LANES)
        acc = jnp.zeros((BLOCK, LANES), F32) + conv_b_ref[:, cols]
        for w in range(CONV_WIDTH):
            a, b = divmod(CONV_LEAD + w, SUBLANES)
            if b == 0:
                taps = ext_ref[CONV_WIN_START + SUBLANES * a:CONV_WIN_START + SUBLANES * a + BLOCK, cols]
            else:
                taps = rot_ref[b - 1, SUBLANES * a:SUBLANES * a + BLOCK, cols]
            acc = acc + taps * conv_w_ref[w:w + 1, cols]
        conv_groups.append(acc)
    conv = jnp.concatenate(conv_groups, axis=1)

    mu = jnp.mean(conv, axis=-1, keepdims=True)
    xc = conv - mu
    ln = xc * lax.rsqrt(jnp.mean(xc * xc, axis=-1, keepdims=True) + CONV_LN_EPS) * ln_g_ref[...] + ln_b_ref[...]
    c_act = _silu(ln).astype(BF16)
    c_branch = jnp.dot(c_act, w_pw2_ref[...], preferred_element_type=F32) * rest[:, SWA_WIDTH + CONV_CHANNELS:]
    out_ref[0, :, SWA_WIDTH:] = c_branch.astype(BF16)


def _ab_mix(sinks, qkv, rest, qkv_meta, rest_meta, conv_w, conv_b, ln_g, ln_b, w_pw2, first_ref_blk):
    batch, rows, qkv_w = qkv.shape
    rest_w = rest.shape[-1]
    kv_blk = (SWA_WIDTH // (2 * SWA_KV_WIDTH))
    prev_blk = lambda i: jnp.maximum(i - 1, 0)
    const2 = lambda b, i: (0, 0)
    return pl.pallas_call(
        functools.partial(_ab_mix_kernel, first_ref_blk),
        grid=(batch, rows // BLOCK),
        in_specs=[
            pl.BlockSpec(memory_space=pltpu.SMEM),
            pl.BlockSpec((1, BLOCK, qkv_w), lambda b, i: (b, i, 0)),
            pl.BlockSpec((1, BLOCK, 2 * SWA_KV_WIDTH), lambda b, i: (b, prev_blk(i), kv_blk)),
            pl.BlockSpec((1, BLOCK, 2 * SWA_KV_WIDTH), lambda b, i: (0, 0, kv_blk)),
            pl.BlockSpec((1, BLOCK, rest_w), lambda b, i: (b, i, 0)),
            pl.BlockSpec((1, BLOCK, CONV_CHANNELS), lambda b, i: (b, prev_blk(i), 1)),
            pl.BlockSpec((1, BLOCK, CONV_CHANNELS), lambda b, i: (0, 0, 1)),
            pl.BlockSpec((CONV_WIDTH + 1, CONV_CHANNELS), const2),
            pl.BlockSpec((1, CONV_CHANNELS), const2),
            pl.BlockSpec((1, CONV_CHANNELS), const2),
            pl.BlockSpec((1, CONV_CHANNELS), const2),
            pl.BlockSpec((CONV_CHANNELS, CONV_CHANNELS), const2),
        ],
        out_specs=pl.BlockSpec((1, BLOCK, AB_MIX), lambda b, i: (b, i, 0)),
        out_shape=jax.ShapeDtypeStruct((batch, rows, AB_MIX), BF16),
        scratch_shapes=[
            pltpu.VMEM((2 * BLOCK, CONV_CHANNELS), F32),
            pltpu.VMEM((SUBLANES - 1, CONV_WIN_ROWS, CONV_CHANNELS), F32),
            pltpu.VMEM((SWA_HEADS, BLOCK, 3 * BLOCK), F32),
            pltpu.VMEM((SWA_HEADS, BLOCK, 3 * BLOCK), BF16),
        ],
        compiler_params=_params("parallel", "parallel"),
        name="ab_mix",
    )(sinks, qkv, qkv, qkv_meta, rest, rest, rest_meta, conv_w, conv_b, ln_g, ln_b, w_pw2)


def _mid_kernel(mix_ref, h_ref, w_out_ref, post_ref, gain_ref, w_in_ref, h1_ref, qkv_ref, gate_ref):
    y = jnp.dot(mix_ref[...], w_out_ref[...], preferred_element_type=F32)
    h1 = h_ref[...] + _rms_normed(y, post_ref[...])
    h1_ref[...] = h1
    hn = _rms_normed(h1, gain_ref[...]).astype(BF16)
    chunk = 4 * LANES

    def proj(lo):
        return jnp.dot(hn, w_in_ref[:, lo:lo + chunk], preferred_element_type=F32)

    for c in range(SB_WIDTH // chunk):
        lo = c * chunk
        qkv_ref[0, :, lo:lo + chunk] = (proj(lo) * HEAD_DIM ** -0.5).astype(BF16)
    for c in range(SB_WIDTH // chunk, 3 * SB_WIDTH // chunk):
        lo = c * chunk
        qkv_ref[0, :, lo:lo + chunk] = proj(lo).astype(BF16)
    for c in range(SB_WIDTH // chunk):
        lo = c * chunk
        gate_ref[:, lo:lo + chunk] = _silu(proj(3 * SB_WIDTH + lo))


def _mid_tail_kernel(n_tiles, mix_ref, h_ref, w_out_ref, post_ref, gain_ref, w_in_ref, tail_ref,
                     h1_ref, qkv_ref, gate_ref):
    @pl.when(pl.program_id(0) < n_tiles)
    def _():
        _mid_kernel(mix_ref, h_ref, w_out_ref, post_ref, gain_ref, w_in_ref, h1_ref, qkv_ref, gate_ref)

    @pl.when(pl.program_id(0) >= n_tiles)
    def _():
        qkv_ref[0, 0:tail_ref.shape[1], :] = tail_ref[0]


def _mid(mix_flat, h_flat, w_out_bf, post, gain, w_in_bf, row_tile, tiles_per_seq, qkv_tail=None):
    rows = h_flat.shape[0]
    n_tiles = rows // row_tile
    n_seq = n_tiles // tiles_per_seq
    seq_rows = tiles_per_seq * row_tile
    const = lambda shape: pl.BlockSpec(shape, lambda r: (0,) * len(shape), pipeline_mode=pl.Buffered(1))
    weights = [const((AB_MIX, D_MODEL)), const((1, D_MODEL)), const((1, D_MODEL)), const((D_MODEL, SB_IN))]
    if qkv_tail is None:
        tile = lambda r: r
        qkv_index = lambda r: (r // tiles_per_seq, r % tiles_per_seq, 0)
        body, steps, tail_specs, tail_args, tail_rows = _mid_kernel, n_tiles, [], (), 0
    else:
        tile = lambda r: jnp.minimum(r, n_tiles - 1)
        qkv_index = lambda r: (jnp.where(r < n_tiles, r // tiles_per_seq, r - n_tiles),
                               jnp.where(r < n_tiles, r % tiles_per_seq, tiles_per_seq), 0)
        body, steps = functools.partial(_mid_tail_kernel, n_tiles), n_tiles + n_seq
        tail_specs, tail_args, tail_rows = [const(qkv_tail.shape)], (qkv_tail,), qkv_tail.shape[1]
    row_spec = lambda width: pl.BlockSpec((row_tile, width), lambda r: (tile(r), 0))
    return pl.pallas_call(
        body,
        grid=(steps,),
        in_specs=[row_spec(AB_MIX), row_spec(D_MODEL)] + weights + tail_specs,
        out_specs=[row_spec(D_MODEL), pl.BlockSpec((1, row_tile, 3 * SB_WIDTH), qkv_index), row_spec(SB_WIDTH)],
        out_shape=[
            jax.ShapeDtypeStruct((rows, D_MODEL), F32),
            jax.ShapeDtypeStruct((n_seq, seq_rows + tail_rows, 3 * SB_WIDTH), BF16),
            jax.ShapeDtypeStruct((rows, SB_WIDTH), F32),
        ],
        compiler_params=pltpu.CompilerParams(dimension_semantics=("arbitrary",), vmem_limit_bytes=MID_VMEM_LIMIT),
        name="ab_out_sb_in_proj",
    )(mix_flat, h_flat, w_out_bf, post, gain, w_in_bf, *tail_args)


def _sb_attn_kernel(q_ref, *refs):
    def one_query_block(sub, _):
        q_blk = pl.program_id(1) * SB_Q_PER_STEP + sub + 1
        _sb_query_block(q_blk, pl.ds(pl.multiple_of(sub * BLOCK, BLOCK), BLOCK), q_ref, *refs)
        return 0

    lax.fori_loop(0, SB_Q_PER_STEP, one_query_block, 0)


def _sb_query_block(q_blk, q_rows, q_ref, k_ref, v_ref, gate_ref, tri_ref, out_ref,
                    carry_ref, acc_ref, qh_ref, z_ref, split_ref, sums_ref, w_ref, bias_ref):
    meta_off = k_ref.shape[1] - BLOCK
    lane = lax.broadcasted_iota(jnp.int32, (BLOCK, LANES), 1)
    row = lax.broadcasted_iota(jnp.int32, (BLOCK, LANES), 0)
    low_half = lane < HEAD_DIM
    zero_bf = jnp.zeros((), BF16)

    qpos = q_blk * BLOCK + row
    carry_ref[...] = jnp.zeros_like(carry_ref)
    acc_ref[...] = jnp.zeros_like(acc_ref)
    for n in range(2 * SB_PAIRS):
        q_pair = q_ref[0, q_rows, (n // 2) * LANES:(n // 2 + 1) * LANES]
        qh_ref[n] = jnp.where(low_half, q_pair, zero_bf) if n % 2 == 0 else jnp.where(low_half, zero_bf, q_pair)

    def visit(k_blk):
        off = pl.multiple_of(jnp.where(k_blk == 0, meta_off, (k_blk - 1) * BLOCK), BLOCK)
        kpos = k_blk * BLOCK + lane

        def scores(p):
            k_pair = k_ref[0, pl.ds(off, BLOCK), p * LANES:(p + 1) * LANES]
            q_both = qh_ref[2 * p:2 * p + 2].reshape(2 * BLOCK, LANES)
            z = lax.dot_general(q_both, k_pair, (((1,), (1,)), ((), ())), preferred_element_type=F32)
            z_ref[2 * p:2 * p + 2] = z.reshape(2, BLOCK, LANES)

        bias_ref[...] = jnp.where((kpos >= META_PAD) & (kpos < qpos), 0.0, NEG_INF)

        def softplus_split(n):
            z = z_ref[n] + bias_ref[...]
            z_ref[n] = z - carry_ref[n]
            sp = jnp.maximum(z, 0.0) + jnp.log(1.0 + jnp.exp2(jnp.abs(z) * -LOG2_E))
            hi = sp.astype(BF16)
            split_ref[n, :, 0:LANES] = hi
            split_ref[n, :, LANES:] = (sp - hi.astype(F32)).astype(BF16)

        def suffix_sums(g):
            heads_g = slice(g * SB_SUM_GROUP, (g + 1) * SB_SUM_GROUP)
            sums = jnp.dot(split_ref[heads_g].reshape(SB_SUM_GROUP * BLOCK, 2 * LANES), tri_ref[...],
                           preferred_element_type=F32)
            sums_ref[heads_g] = sums.reshape(SB_SUM_GROUP, BLOCK, 2 * LANES)

        def carry_update(n, least):
            new_carry = carry_ref[n] + sums_ref[n, :, LANES:]
            carry_ref[n] = new_carry
            return new_carry if least is None else jnp.minimum(least, new_carry)

        def weights(n):
            w = jnp.exp(z_ref[n] - sums_ref[n, :, 0:LANES])
            w_ref[n] = w.astype(BF16)

        def accumulate(p):
            cols = slice(p * LANES, (p + 1) * LANES)
            o = jnp.dot(w_ref[2 * p:2 * p + 2].reshape(2 * BLOCK, LANES), v_ref[0, pl.ds(off, BLOCK), cols],
                        preferred_element_type=F32)
            acc_ref[p] += jnp.where(low_half, o[:BLOCK], o[BLOCK:])

        pairs, heads = range(SB_PAIRS), range(2 * SB_PAIRS)
        for p in pairs:
            scores(p)
        for n in heads:
            softplus_split(n)
        for g in range(2 * SB_PAIRS // SB_SUM_GROUP):
            suffix_sums(g)
        least = None
        for n in heads:
            least = carry_update(n, least)
        least_mass = jnp.min(least)
        for n in heads:
            weights(n)
        for p in pairs:
            accumulate(p)
        return least_mass

    def keep_going(state):
        k_blk, least_mass = state
        return (k_blk >= 0) & (least_mass < SB_MASS_CUTOFF)

    def body(state):
        k_blk, _ = state
        return k_blk - 1, visit(k_blk)

    lax.while_loop(keep_going, body, (q_blk, jnp.float32(0.0)))

    for p in range(SB_PAIRS):
        cols = slice(p * LANES, (p + 1) * LANES)
        out_ref[0, q_rows, cols] = (acc_ref[p] * gate_ref[0, q_rows, cols]).astype(BF16)


def _sb_attn(qkv, gate, tri):
    batch, lp, _ = qkv.shape
    seq = lp - BLOCK
    q_tile = SB_Q_PER_STEP * BLOCK
    seq_cols = lambda col: pl.BlockSpec((1, lp, SB_WIDTH), lambda b, i: (b, 0, col))
    return pl.pallas_call(
        _sb_attn_kernel,
        grid=(batch, seq // q_tile),
        in_specs=[
            pl.BlockSpec((1, q_tile, SB_WIDTH), lambda b, i: (b, i, 0)),
            seq_cols(1), seq_cols(2),
            pl.BlockSpec((1, q_tile, SB_WIDTH), lambda b, i: (b, i, 0)),
            pl.BlockSpec((2 * BLOCK, 2 * LANES), lambda b, i: (0, 0)),
        ],
        out_specs=pl.BlockSpec((1, q_tile, SB_WIDTH), lambda b, i: (b, i, 0)),
        out_shape=jax.ShapeDtypeStruct((batch, seq, SB_WIDTH), BF16),
        scratch_shapes=[
            pltpu.VMEM((2 * SB_PAIRS, BLOCK, LANES), F32),
            pltpu.VMEM((SB_PAIRS, BLOCK, LANES), F32),
            pltpu.VMEM((2 * SB_PAIRS, BLOCK, LANES), BF16),
            pltpu.VMEM((2 * SB_PAIRS, BLOCK, LANES), F32),
            pltpu.VMEM((2 * SB_PAIRS, BLOCK, 2 * LANES), BF16),
            pltpu.VMEM((2 * SB_PAIRS, BLOCK, 2 * LANES), F32),
            pltpu.VMEM((2 * SB_PAIRS, BLOCK, LANES), BF16),
            pltpu.VMEM((BLOCK, LANES), F32),
        ],
        compiler_params=_params("parallel", "arbitrary"),
        name="sb_attn",
    )(qkv, qkv, qkv, gate, tri)


def _sb_out_kernel(o_ref, h_ref, w_ref, post_ref, out_ref):
    y = jnp.dot(o_ref[0], w_ref[...], preferred_element_type=F32)
    out_ref[0] = h_ref[0] + _rms_normed(y, post_ref[...])


def _sb_out(o_gated, h, w_bf, post):
    batch, rows, _ = o_gated.shape
    return pl.pallas_call(
        _sb_out_kernel,
        grid=(batch, rows // SB_OUT_ROW_TILE),
        in_specs=[
            pl.BlockSpec((1, SB_OUT_ROW_TILE, SB_WIDTH), lambda b, i: (b, i, 0)),
            pl.BlockSpec((1, SB_OUT_ROW_TILE, D_MODEL), lambda b, i: (b, i, 0)),
            pl.BlockSpec((SB_WIDTH, D_MODEL), lambda b, i: (0, 0)),
            pl.BlockSpec((1, D_MODEL), lambda b, i: (0, 0)),
        ],
        out_specs=pl.BlockSpec((1, SB_OUT_ROW_TILE, D_MODEL), lambda b, i: (b, i, 0)),
        out_shape=jax.ShapeDtypeStruct((batch, rows, D_MODEL), F32),
        compiler_params=_params("parallel", "parallel"),
        name="sb_out_proj",
    )(o_gated, h, w_bf, post)


def _rope_tables(seq):
    half = HEAD_DIM // 2
    pos = jnp.concatenate([N_META + jnp.arange(seq), jnp.zeros((META_PAD,), jnp.int32), jnp.arange(N_META)])
    inv = ROPE_THETA ** (-jnp.arange(half, dtype=jnp.float32) / half)
    ang = pos.astype(jnp.float32)[:, None] * inv[None, :]
    cos = jnp.cos(ang)
    sin = jnp.sin(ang)
    cos_t = jnp.tile(jnp.concatenate([cos, cos], axis=-1), (1, LANES // HEAD_DIM))
    sin_t = jnp.tile(jnp.concatenate([-sin, sin], axis=-1), (1, LANES // HEAD_DIM))
    return cos_t, sin_t


def kernel(x, meta_tokens, ab_pre_norm, ab_w_in, ab_sinks, ab_conv_w, ab_conv_b, ab_conv_ln_g, ab_conv_ln_b, ab_w_pw2, ab_w_out, ab_post_norm, sb_pre_norm, sb_w_in, sb_w_out, sb_post_norm):
    batch, seq, _ = x.shape
    assert seq % ROW_TILE == 0
    tiles_per_seq = seq // ROW_TILE
    n_blocks = seq // BLOCK

    x_flat = x.reshape(batch * seq, D_MODEL)
    meta_blk = jnp.concatenate([jnp.zeros((META_PAD, D_MODEL), x.dtype), meta_tokens.astype(x.dtype)], axis=0)
    cos_t, sin_t = _rope_tables(seq)
    pre0, post0, pre1 = ab_pre_norm[0][None], ab_post_norm[0][None], sb_pre_norm[0][None]
    w_in0, w_pw2, w_out0 = ab_w_in[0].astype(BF16), ab_w_pw2[0].astype(BF16), ab_w_out[0].astype(BF16)
    w_in1, w_out1 = sb_w_in[0].astype(BF16), sb_w_out[0].astype(BF16)

    qkv, rest = _ab_in(x_flat, pre0, w_in0, cos_t, sin_t, AB_IN_ROW_TILE, lambda r: r % (seq // AB_IN_ROW_TILE))
    qkv_m, rest_m = _ab_in(meta_blk, pre0, w_in0, cos_t, sin_t, BLOCK, lambda r: n_blocks)
    qkv_m, rest_m = qkv_m[None], rest_m[None]
    conv_w = jnp.concatenate([ab_conv_w[0], jnp.zeros((1, CONV_CHANNELS), F32)], axis=0)
    mix_args = (conv_w, ab_conv_b[0][None], ab_conv_ln_g[0][None], ab_conv_ln_b[0][None], w_pw2)
    mix = _ab_mix(ab_sinks[0], qkv.reshape(batch, seq, -1), rest.reshape(batch, seq, -1), qkv_m, rest_m, *mix_args, 1)
    mix_m = _ab_mix(ab_sinks[0], qkv_m, rest_m, qkv_m, rest_m, *mix_args, 0)

    mid_args = (w_out0, post0, pre1, w_in1)
    _, qkv1_m, _ = _mid(mix_m[0], meta_blk, *mid_args, BLOCK, 1)
    h1, qkv1, gate1 = _mid(mix.reshape(batch * seq, AB_MIX), x_flat, *mid_args, MID_ROW_TILE, seq // MID_ROW_TILE, qkv1_m)

    r = jnp.arange(BLOCK)
    tri = jnp.concatenate([(r[:, None] >= r[None, :]).astype(BF16), jnp.ones((BLOCK, LANES), BF16)], axis=1)
    tri = jnp.concatenate([tri, tri], axis=0)
    o_gated = _sb_attn(qkv1, gate1.reshape(batch, seq, -1), tri)
    return _sb_out(o_gated, h1.reshape(batch, seq, D_MODEL), w_out1, sb_post_norm[0][None])
```
